```python
import jax, jax.numpy as jnp
from jax import lax
import numpy as np

D_MODEL = 2048
BATCH = 16
SEQ = 256
DEPTH = 4
DEC_BATCH = 2
DEC_SEQ = 1024
PAST_LEN = 256

GRID_W = 64
N_HEADS = 16
HEAD_DIM = D_MODEL // N_HEADS
ATTN_SCALE = HEAD_DIM ** -0.5
D_POOL = D_MODEL // 2
D_CONV = D_MODEL // 2
POOL_WINDOWS = (2, 4, 8, 16)
POOL_GROUPS = 4
POOL_GC = D_POOL // POOL_GROUPS
CONV_W = 31
NA_KH = 8
NA_KW = 16
NA_QB = 16
NA_KB = 32
CTX_QBLK = 128
N_GROUPS = 4
EXP_PER_GROUP = 4
N_EXPERTS = N_GROUPS * EXP_PER_GROUP
TOP_K = 2
D_EXPERT = D_MODEL // 4
EPS = 1e-6
NEG = -1e30

kernel_name = 'hybrid_dit_pool_conv_natten_hmoe_step'


def _rmsnorm(x, g):
    xf = x.astype(jnp.float32)
    y = xf * lax.rsqrt(jnp.mean(xf * xf, axis=-1, keepdims=True) + EPS)
    return (y * g.astype(jnp.float32)).astype(x.dtype)


def _adaln(cvec, w, b):
    m = jax.nn.silu(cvec) @ w + b
    return [t[:, None, :] for t in jnp.split(m, 6, axis=-1)]


def _pool_mixer(u, w, scale):
    B, T, _ = u.shape
    uf = u.astype(jnp.float32).reshape(B, T, POOL_GROUPS, POOL_GC)
    cs = jnp.concatenate([jnp.zeros((B, 1, POOL_GROUPS, POOL_GC), jnp.float32), jnp.cumsum(uf, axis=1)], axis=1)
    t = np.arange(T)
    outs = []
    for g, win in enumerate(POOL_WINDOWS):
        lo = np.clip(t - win // 2, 0, T - 1)
        hi = np.clip(t + (win - win // 2) - 1, 0, T - 1)
        cnt = (hi - lo + 1).astype(np.float32)
        s = cs[:, hi + 1, g] - cs[:, lo, g]
        outs.append(s / cnt[None, :, None] - uf[:, :, g])
    p = jnp.stack(outs, axis=2).astype(u.dtype)
    y = jnp.einsum('btgc,gce->btge', p, w).reshape(B, T, D_POOL)
    return y * scale


def _conv_mixer(u, w, b, ln_g, ln_b):
    a, gte = jnp.split(u, 2, axis=-1)
    z = a * jax.nn.sigmoid(gte)
    z = lax.conv_general_dilated(z, w[:, None, :], window_strides=(1,),
                                 padding=((CONV_W // 2, CONV_W // 2),),
                                 dimension_numbers=('NWC', 'WIO', 'NWC'),
                                 feature_group_count=D_CONV) + b
    zf = z.astype(jnp.float32)
    mu = jnp.mean(zf, axis=-1, keepdims=True)
    var = jnp.mean(jnp.square(zf - mu), axis=-1, keepdims=True)
    zn = ((zf - mu) * lax.rsqrt(var + EPS) * ln_g + ln_b).astype(u.dtype)
    return jax.nn.silu(zn)


def _even_mixer(h, in_w, pool_w, pool_scale, conv_w, conv_b, ln_g, ln_b, out_w):
    u = h @ in_w
    ya = _pool_mixer(u[..., :D_POOL], pool_w, pool_scale)
    yb = _conv_mixer(u[..., D_POOL:], conv_w, conv_b, ln_g, ln_b)
    return jnp.concatenate([ya, yb], axis=-1) @ out_w


def _qkv(h, w, qg, kg):
    B, T, _ = h.shape
    qkv = (h @ w).reshape(B, T, 3, N_HEADS, HEAD_DIM)
    return _rmsnorm(qkv[:, :, 0], qg), _rmsnorm(qkv[:, :, 1], kg), qkv[:, :, 2]


def _context_attention(q, k, v):
    B, S, H, Dh = q.shape
    qb = q.reshape(B, S // CTX_QBLK, CTX_QBLK, H, Dh).transpose(1, 0, 2, 3, 4)

    def block(qi):
        logits = jnp.einsum('bqhd,bkhd->bhqk', qi, k).astype(jnp.float32) * ATTN_SCALE
        p = jax.nn.softmax(logits, axis=-1).astype(v.dtype)
        return jnp.einsum('bhqk,bkhd->bqhd', p, v)

    o = lax.map(block, qb)
    return o.transpose(1, 0, 2, 3, 4).reshape(B, S, H * Dh)


def _na_tables(rows):
    kh = min(NA_KH, rows)
    n_cb = GRID_W // NA_QB
    r = np.arange(rows)
    j = np.arange(n_cb)
    key_rows = np.clip(r - kh // 2, 0, rows - kh)[:, None] + np.arange(kh)[None, :]
    key_cols = np.clip(j * NA_QB - NA_KW // 2, 0, GRID_W - NA_KB)[:, None] + np.arange(NA_KB)[None, :]
    q_cols = j[:, None] * NA_QB + np.arange(NA_QB)[None, :]
    win0 = np.clip(q_cols - NA_KW // 2, 0, GRID_W - NA_KW)
    col_ok = (key_cols[:, None, :] >= win0[:, :, None]) & (key_cols[:, None, :] < win0[:, :, None] + NA_KW)
    nb = rows * n_cb
    kloc = kh * NA_KB
    shape5 = (rows, n_cb, NA_QB, kh, NA_KB)
    k_idx = (key_rows[:, None, :, None] * GRID_W + key_cols[None, :, None, :]).reshape(nb, kloc)
    mask = np.broadcast_to(col_ok[None, :, :, None, :], shape5).reshape(nb, NA_QB, kloc)
    d_row = np.broadcast_to((key_rows - r[:, None] + NA_KH - 1)[:, None, None, :, None], shape5).reshape(nb, NA_QB, kloc)
    d_col = np.broadcast_to(np.clip(key_cols[:, None, :] - q_cols[:, :, None] + NA_KW - 1, 0, 2 * NA_KW - 2)[None, :, :, None, :], shape5).reshape(nb, NA_QB, kloc)
    return k_idx, mask, d_row, d_col


def _na_attention(q, k, v, k_ctx, v_ctx, rpb):
    B, T, H, Dh = q.shape
    k_idx, mask, d_row, d_col = _na_tables(T // GRID_W)
    nb, kloc = k_idx.shape
    qb = q.reshape(B, nb, NA_QB, H, Dh)
    kl = jnp.take(k, k_idx, axis=1)
    vl = jnp.take(v, k_idx, axis=1)
    bias = rpb[:, d_row, d_col].transpose(1, 0, 2, 3).astype(jnp.float32)
    l_loc = jnp.einsum('bnqhd,bnkhd->bnhqk', qb, kl).astype(jnp.float32) * ATTN_SCALE + bias
    l_loc = jnp.where(mask[None, :, None], l_loc, NEG)
    l_ctx = jnp.einsum('bnqhd,bchd->bnhqc', qb, k_ctx).astype(jnp.float32) * ATTN_SCALE
    p = jax.nn.softmax(jnp.concatenate([l_loc, l_ctx], axis=-1), axis=-1).astype(v.dtype)
    o = (jnp.einsum('bnhqk,bnkhd->bnqhd', p[..., :kloc], vl)
         + jnp.einsum('bnhqc,bchd->bnqhd', p[..., kloc:], v_ctx))
    return o.reshape(B, T, H * Dh)


def _moe(h, rgw, rgb, rew, reb, w1, w3, w2):
    B, T, D = h.shape
    x = h.reshape(B * T, D)
    lg = (x @ rgw).astype(jnp.float32) + rgb.astype(jnp.float32)
    oh_g = jax.nn.one_hot(jnp.argmax(lg, axis=-1), N_GROUPS, dtype=jnp.float32)
    pg = jnp.sum(jax.nn.softmax(lg, axis=-1) * oh_g, axis=-1, keepdims=True)
    le = jnp.einsum('nd,dge->nge', x, rew).astype(jnp.float32) + reb.astype(jnp.float32)
    le = jnp.einsum('nge,ng->ne', le, oh_g)
    tv, ti = lax.top_k(le, TOP_K)
    tw = jax.nn.softmax(tv, axis=-1) * pg
    comb_e = jnp.sum(jax.nn.one_hot(ti, EXP_PER_GROUP, dtype=jnp.float32) * tw[..., None], axis=1)
    comb = (oh_g[:, :, None] * comb_e[:, None, :]).reshape(-1, N_EXPERTS)
    hid = jax.nn.silu(jnp.einsum('nd,xdf->nxf', x, w1)) * jnp.einsum('nd,xdf->nxf', x, w3)
    out = jnp.einsum('nxf,xfd->nd', hid * comb.astype(hid.dtype)[..., None], w2)
    return out.reshape(B, T, D)


def _ffn_sublayer(x, sh, sc, gt, norm_g, rgw, rgb, rew, reb, w1, w3, w2):
    h = _rmsnorm(x, norm_g) * (1 + sc) + sh
    return x + gt * _moe(h, rgw, rgb, rew, reb, w1, w3, w2)


def setup_inputs(seed: int = 0) -> dict:
    key = jax.random.key(seed)
    ks = iter(jax.random.split(key, 40))
    ne = (DEPTH + 1) // 2
    no = DEPTH // 2
    f32 = jnp.float32

    def nrm(shape, scale):
        return jax.random.normal(next(ks), shape, f32) * scale

    def gain(shape):
        return 1.0 + nrm(shape, 0.02)

    return {
        'x_prompt': nrm((BATCH, SEQ, D_MODEL), 1.0),
        'x_sample': nrm((DEC_BATCH, DEC_SEQ, D_MODEL), 1.0),
        'cache_k': nrm((DEC_BATCH, no, PAST_LEN, N_HEADS, HEAD_DIM), 1.0),
        'cache_v': nrm((DEC_BATCH, no, PAST_LEN, N_HEADS, HEAD_DIM), 1.0),
        'c': nrm((DEC_BATCH, D_MODEL), 1.0),
        'c_ctx': nrm((D_MODEL,), 1.0),
        'ada_w': nrm((DEPTH, D_MODEL, 6 * D_MODEL), 0.5 * D_MODEL ** -0.5),
        'ada_b': nrm((DEPTH, 6 * D_MODEL), 0.02),
        'norm_mix_g': gain((DEPTH, D_MODEL)),
        'norm_ffn_g': gain((DEPTH, D_MODEL)),
        'mix_in_w': nrm((ne, D_MODEL, D_POOL + 2 * D_CONV), D_MODEL ** -0.5),
        'pool_w': nrm((ne, POOL_GROUPS, POOL_GC, POOL_GC), POOL_GC ** -0.5),
        'pool_scale': gain((ne, D_POOL)),
        'conv_w': nrm((ne, CONV_W, D_CONV), CONV_W ** -0.5),
        'conv_b': nrm((ne, D_CONV), 0.02),
        'conv_ln_g': gain((ne, D_CONV)),
        'conv_ln_b': nrm((ne, D_CONV), 0.02),
        'mix_out_w': nrm((ne, D_POOL + D_CONV, D_MODEL), (D_POOL + D_CONV) ** -0.5),
        'qkv_w': nrm((no, D_MODEL, 3 * D_MODEL), D_MODEL ** -0.5),
        'q_norm_g': gain((no, HEAD_DIM)),
        'k_norm_g': gain((no, HEAD_DIM)),
        'rpb': nrm((no, N_HEADS, 2 * NA_KH - 1, 2 * NA_KW - 1), 0.1),
        'attn_out_w': nrm((no, D_MODEL, D_MODEL), D_MODEL ** -0.5),
        'router_g_w': nrm((DEPTH, D_MODEL, N_GROUPS), D_MODEL ** -0.5),
        'router_g_b': nrm((DEPTH, N_GROUPS), 0.01),
        'router_e_w': nrm((DEPTH, D_MODEL, N_GROUPS, EXP_PER_GROUP), D_MODEL ** -0.5),
        'router_e_b': nrm((DEPTH, N_GROUPS, EXP_PER_GROUP), 0.01),
        'exp_w1': nrm((DEPTH, N_EXPERTS, D_MODEL, D_EXPERT), D_MODEL ** -0.5),
        'exp_w3': nrm((DEPTH, N_EXPERTS, D_MODEL, D_EXPERT), D_MODEL ** -0.5),
        'exp_w2': nrm((DEPTH, N_EXPERTS, D_EXPERT, D_MODEL), D_EXPERT ** -0.5),
    }


def reference(x_prompt, x_sample, cache_k, cache_v, c, c_ctx, ada_w, ada_b, norm_mix_g, norm_ffn_g,
              mix_in_w, pool_w, pool_scale, conv_w, conv_b, conv_ln_g, conv_ln_b, mix_out_w,
              qkv_w, q_norm_g, k_norm_g, rpb, attn_out_w, router_g_w, router_g_b, router_e_w,
              router_e_b, exp_w1, exp_w3, exp_w2):
    ks, vs = [], []
    xp = x_prompt
    for l in range(DEPTH):
        e = l // 2
        sh1, sc1, g1, sh2, sc2, g2 = _adaln(c_ctx[None, :], ada_w[l], ada_b[l])
        h = _rmsnorm(xp, norm_mix_g[l]) * (1 + sc1) + sh1
        if l % 2 == 0:
            y = _even_mixer(h, mix_in_w[e], pool_w[e], pool_scale[e], conv_w[e], conv_b[e],
                            conv_ln_g[e], conv_ln_b[e], mix_out_w[e])
        else:
            q, k, v = _qkv(h, qkv_w[e], q_norm_g[e], k_norm_g[e])
            ks.append(k)
            vs.append(v)
            y = _context_attention(q, k, v) @ attn_out_w[e]
        xp = xp + g1 * y
        xp = _ffn_sublayer(xp, sh2, sc2, g2, norm_ffn_g[l], router_g_w[l], router_g_b[l],
                           router_e_w[l], router_e_b[l], exp_w1[l], exp_w3[l], exp_w2[l])
    new_cache_k = jnp.stack(ks, axis=1)
    new_cache_v = jnp.stack(vs, axis=1)

    xs = x_sample
    for l in range(DEPTH):
        e = l // 2
        sh1, sc1, g1, sh2, sc2, g2 = _adaln(c, ada_w[l], ada_b[l])
        h = _rmsnorm(xs, norm_mix_g[l]) * (1 + sc1) + sh1
        if l % 2 == 0:
            y = _even_mixer(h, mix_in_w[e], pool_w[e], pool_scale[e], conv_w[e], conv_b[e],
                            conv_ln_g[e], conv_ln_b[e], mix_out_w[e])
        else:
            q, k, v = _qkv(h, qkv_w[e], q_norm_g[e], k_norm_g[e])
            y = _na_attention(q, k, v, cache_k[:, e], cache_v[:, e], rpb[e]) @ attn_out_w[e]
        xs = xs + g1 * y
        xs = _ffn_sublayer(xs, sh2, sc2, g2, norm_ffn_g[l], router_g_w[l], router_g_b[l],
                           router_e_w[l], router_e_b[l], exp_w1[l], exp_w3[l], exp_w2[l])

    return (xp, xs, new_cache_k, new_cache_v)
```

```python
import functools

import numpy as np
import jax
import jax.numpy as jnp
from jax import lax
from jax.experimental import pallas as pl
from jax.experimental.pallas import tpu as pltpu

D = 2048
N_CTX_SEQ = 16
CTX_SEQ = 256
N_LAT_SEQ = 2
LAT_SEQ = 1024
N_CTX = N_CTX_SEQ * CTX_SEQ
N_LAT = N_LAT_SEQ * LAT_SEQ
N_TOK = N_CTX + N_LAT
DEPTH = 4
GRID_W = 64
GRID_H = LAT_SEQ // GRID_W
N_HEADS = 16
HEAD_DIM = 128
ATTN_SCALE = HEAD_DIM ** -0.5
D_POOL = 1024
D_CONV = 1024
POOL_WINDOWS = (2, 4, 8, 16)
POOL_GC = 256
CONV_W = 31
NA_KH = 8
NA_KW = 16
N_GROUPS = 4
EXP_PER_GROUP = 4
N_EXPERTS = 16
D_EXPERT = 512
EPS = 1e-6
NEG = -1e30

SEQ_TILE = 256
HALO = 16
ROUTER_LANES = 128
EXPERT_LANE0 = N_GROUPS
VMEM_LIMIT = 56 * 1024 * 1024

F32 = jnp.float32
BF16 = jnp.bfloat16


def _cond_of_row(row):
    return jnp.where(row < N_CTX, 0, 1 + (row - N_CTX) // LAT_SEQ)


def _mod_spec(layer, which, tm, tn=D, row_axis=1, col_axis=None):
    def index_map(*ids):
        cond = _cond_of_row(ids[row_axis] * tm)
        col = 0 if col_axis is None else ids[col_axis]
        return (layer, cond, which, 0, col)
    return pl.BlockSpec((None, None, None, 1, tn), index_map)


def _silu(x):
    return x / (1.0 + jnp.exp(-x))


def _modnorm(x, g, sc, sh):
    ms = jnp.mean(x * x, axis=-1, keepdims=True)
    y = x * lax.rsqrt(ms + EPS) * g
    return y * (1.0 + sc) + sh


def _ada_kernel(c_ref, w_ref, b_ref, o_ref):
    s = _silu(c_ref[...]).astype(BF16)
    o_ref[...] = jnp.dot(s, w_ref[...].astype(BF16), preferred_element_type=F32) + b_ref[...]


def _adaln_all(cvec8, ada_w, ada_b):
    tn = 1024
    n_out = 6 * D
    out = pl.pallas_call(
        _ada_kernel,
        grid=(DEPTH, n_out // tn),
        in_specs=[
            pl.BlockSpec((8, D), lambda l, j: (0, 0)),
            pl.BlockSpec((None, D, tn), lambda l, j: (l, 0, j)),
            pl.BlockSpec((None, 1, tn), lambda l, j: (l, 0, j)),
        ],
        out_specs=pl.BlockSpec((None, 8, tn), lambda l, j: (l, 0, j)),
        out_shape=jax.ShapeDtypeStruct((DEPTH, 8, n_out), F32),
        compiler_params=pltpu.CompilerParams(
            dimension_semantics=("arbitrary", "arbitrary"), vmem_limit_bytes=VMEM_LIMIT),
        name="adaln",
    )(cvec8, ada_w, ada_b.reshape(DEPTH, 1, n_out))
    return out[:, :3].reshape(DEPTH, 3, 6, 1, D)


def _mm_pro_kernel(x_ref, g_ref, sc_ref, sh_ref, w_ref, o_ref):
    h = _modnorm(x_ref[...], g_ref[...], sc_ref[...], sh_ref[...]).astype(BF16)
    o_ref[...] = jnp.dot(h, w_ref[...], preferred_element_type=F32)


def _qkv_kernel(x_ref, g_ref, sc_ref, sh_ref, w_ref, hg_ref, o_ref, *, tn, n_norm_tiles):
    h = _modnorm(x_ref[...], g_ref[...], sc_ref[...], sh_ref[...]).astype(BF16)
    y = jnp.dot(h, w_ref[...], preferred_element_type=F32)
    j = pl.program_id(0)

    @pl.when(j < n_norm_tiles)
    def _():
        for hh in range(tn // HEAD_DIM):
            ls = slice(hh * HEAD_DIM, (hh + 1) * HEAD_DIM)
            yh = y[:, ls]
            r = lax.rsqrt(jnp.mean(yh * yh, axis=-1, keepdims=True) + EPS)
            o_ref[:, ls] = yh * r * hg_ref[:, ls]

    @pl.when(j >= n_norm_tiles)
    def _():
        o_ref[...] = y


def _mm_prologue(x, mod, layer, norm_g, w_bf16, head_gains=None, tm=512, tn=1024):
    n, _ = x.shape
    f = w_bf16.shape[1]
    in_specs = [
        pl.BlockSpec((tm, D), lambda j, i: (i, 0)),
        pl.BlockSpec((1, D), lambda j, i: (0, 0)),
        _mod_spec(layer, 1, tm),
        _mod_spec(layer, 0, tm),
        pl.BlockSpec((D, tn), lambda j, i: (0, j)),
    ]
    args = [x, norm_g.reshape(1, D), mod, mod, w_bf16]
    if head_gains is None:
        body = _mm_pro_kernel
    else:
        n_norm_tiles = head_gains.shape[1] // tn
        body = functools.partial(_qkv_kernel, tn=tn, n_norm_tiles=n_norm_tiles)
        in_specs.append(pl.BlockSpec((1, tn), lambda j, i: (0, jnp.minimum(j, n_norm_tiles - 1))))
        args.append(head_gains)
    return pl.pallas_call(
        body,
        grid=(f // tn, n // tm),
        in_specs=in_specs,
        out_specs=pl.BlockSpec((tm, tn), lambda j, i: (i, j)),
        out_shape=jax.ShapeDtypeStruct((n, f), F32),
        compiler_params=pltpu.CompilerParams(
            dimension_semantics=("arbitrary", "arbitrary"), vmem_limit_bytes=VMEM_LIMIT),
        name="mm_prologue",
    )(*args)


def _mm_res_kernel(a_ref, w_ref, x_ref, gate_ref, o_ref):
    y = jnp.dot(a_ref[...], w_ref[...], preferred_element_type=F32)
    o_ref[...] = x_ref[...] + gate_ref[...] * y


def _mm_residual(a_bf16, w_bf16, x, mod, layer, tm=512, tn=1024):
    n, k = a_bf16.shape
    return pl.pallas_call(
        _mm_res_kernel,
        grid=(D // tn, n // tm),
        in_specs=[
            pl.BlockSpec((tm, k), lambda j, i: (i, 0)),
            pl.BlockSpec((k, tn), lambda j, i: (0, j)),
            pl.BlockSpec((tm, tn), lambda j, i: (i, j)),
            _mod_spec(layer, 2, tm, tn=tn, col_axis=0),
        ],
        out_specs=pl.BlockSpec((tm, tn), lambda j, i: (i, j)),
        out_shape=jax.ShapeDtypeStruct((n, D), F32),
        compiler_params=pltpu.CompilerParams(
            dimension_semantics=("arbitrary", "arbitrary"), vmem_limit_bytes=VMEM_LIMIT),
        name="mm_residual",
    )(a_bf16, w_bf16, x, mod)


def _seq_kernel(u_ref, top_ref, bot_ref, pw_ref, ps_ref, cw_ref, cb_ref, lg_ref, lb_ref,
                o_ref, zpad_ref, conv_ref):
    i = pl.program_id(0)
    n_ctx_tiles = N_CTX // SEQ_TILE
    tiles_per_lat = LAT_SEQ // SEQ_TILE
    is_lat = i >= n_ctx_tiles
    chunk = jnp.where(is_lat, (i - n_ctx_tiles) % tiles_per_lat, 0)
    top_ok = jnp.logical_and(is_lat, chunk > 0)
    bot_ok = jnp.logical_and(is_lat, chunk < tiles_per_lat - 1)
    seq_len = jnp.where(is_lat, LAT_SEQ, CTX_SEQ)
    t = chunk * SEQ_TILE + lax.broadcasted_iota(jnp.int32, (SEQ_TILE, 1), 0)
    rows = SEQ_TILE + 2 * HALO

    for g, win in enumerate(POOL_WINDOWS):
        ls = slice(g * POOL_GC, (g + 1) * POOL_GC)
        mid = u_ref[:, ls]
        top = jnp.where(top_ok, top_ref[:, ls], 0.0)
        bot = jnp.where(bot_ok, bot_ref[:, ls], 0.0)
        up = jnp.concatenate([top, mid, bot], axis=0)
        s = pltpu.roll(up, 1, 0) + up
        if win >= 4:
            s = pltpu.roll(s, 1, 0) + pltpu.roll(s, rows - 1, 0)
        if win >= 8:
            s = pltpu.roll(s, 2, 0) + pltpu.roll(s, rows - 2, 0)
        if win >= 16:
            s = pltpu.roll(s, 4, 0) + pltpu.roll(s, rows - 4, 0)
        s = s[HALO:HALO + SEQ_TILE]
        lo = jnp.maximum(t - win // 2, 0)
        hi = jnp.minimum(t + (win - win // 2) - 1, seq_len - 1)
        cnt = (hi - lo + 1).astype(F32)
        p = (s / cnt - mid).astype(BF16)
        y = jnp.dot(p, pw_ref[g], preferred_element_type=F32) * ps_ref[:, ls]
        o_ref[:, ls] = y.astype(BF16)

    def glu(ref):
        return ref[:, D_POOL:D_POOL + D_CONV] * (1.0 / (1.0 + jnp.exp(-ref[:, D_POOL + D_CONV:])))

    zpad_ref[0:HALO, :] = jnp.where(top_ok, glu(top_ref), 0.0)
    zpad_ref[HALO:HALO + SEQ_TILE, :] = glu(u_ref)
    zpad_ref[HALO + SEQ_TILE:rows, :] = jnp.where(bot_ok, glu(bot_ref), 0.0)
    for lt in range(D_CONV // 128):
        ls = slice(lt * 128, (lt + 1) * 128)
        zp = zpad_ref[:, ls]
        acc = jnp.zeros((SEQ_TILE, 128), F32)
        for b in range(8):
            sb = zp if b == 0 else pltpu.roll(zp, rows - b, 0)
            for a in range(4):
                off = 8 * a + b
                if 1 <= off <= CONV_W:
                    acc = acc + cw_ref[off - 1:off, ls] * sb[8 * a:8 * a + SEQ_TILE]
        conv_ref[:, ls] = acc + cb_ref[:, ls]
    zc = conv_ref[...]
    mu = jnp.mean(zc, axis=-1, keepdims=True)
    d = zc - mu
    var = jnp.mean(d * d, axis=-1, keepdims=True)
    zn = d * lax.rsqrt(var + EPS) * lg_ref[...] + lb_ref[...]
    o_ref[:, D_POOL:] = _silu(zn).astype(BF16)


def _seq_mixer(u, pool_w_bf16, pool_scale, conv_w, conv_b, ln_g, ln_b):
    n = u.shape[0]
    f_in = u.shape[1]
    n_tiles = n // SEQ_TILE
    hb = SEQ_TILE // HALO
    n_hblocks = n // HALO
    return pl.pallas_call(
        _seq_kernel,
        grid=(n_tiles,),
        in_specs=[
            pl.BlockSpec((SEQ_TILE, f_in), lambda i: (i, 0)),
            pl.BlockSpec((HALO, f_in), lambda i: (jnp.maximum(i * hb - 1, 0), 0)),
            pl.BlockSpec((HALO, f_in), lambda i: (jnp.minimum((i + 1) * hb, n_hblocks - 1), 0)),
            pl.BlockSpec((len(POOL_WINDOWS), POOL_GC, POOL_GC), lambda i: (0, 0, 0)),
            pl.BlockSpec((1, D_POOL), lambda i: (0, 0)),
            pl.BlockSpec((CONV_W, D_CONV), lambda i: (0, 0)),
            pl.BlockSpec((1, D_CONV), lambda i: (0, 0)),
            pl.BlockSpec((1, D_CONV), lambda i: (0, 0)),
            pl.BlockSpec((1, D_CONV), lambda i: (0, 0)),
        ],
        out_specs=pl.BlockSpec((SEQ_TILE, D_POOL + D_CONV), lambda i: (i, 0)),
        out_shape=jax.ShapeDtypeStruct((n, D_POOL + D_CONV), BF16),
        scratch_shapes=[
            pltpu.VMEM((SEQ_TILE + 2 * HALO, D_CONV), F32),
            pltpu.VMEM((SEQ_TILE, D_CONV), F32),
        ],
        compiler_params=pltpu.CompilerParams(
            dimension_semantics=("arbitrary",), vmem_limit_bytes=VMEM_LIMIT),
        name="seq_mixer",
    )(u, u, u, pool_w_bf16, pool_scale.reshape(1, D_POOL), conv_w, conv_b.reshape(1, D_CONV),
      ln_g.reshape(1, D_CONV), ln_b.reshape(1, D_CONV))


def _ctx_attn_kernel(q_ref, k_ref, v_ref, o_ref):
    for h in range(N_HEADS):
        ls = slice(h * HEAD_DIM, (h + 1) * HEAD_DIM)
        q = q_ref[:, ls].astype(BF16)
        k = k_ref[:, ls].astype(BF16)
        v = v_ref[:, ls].astype(BF16)
        s = lax.dot_general(q, k, (((1,), (1,)), ((), ())), preferred_element_type=F32) * ATTN_SCALE
        m = jnp.max(s, axis=-1, keepdims=True)
        e = jnp.exp(s - m)
        l = jnp.sum(e, axis=-1, keepdims=True)
        p = (e / l).astype(BF16)
        o_ref[:, ls] = jnp.dot(p, v, preferred_element_type=F32).astype(BF16)


def _ctx_attention(qkv):
    n = qkv.shape[0]
    return pl.pallas_call(
        _ctx_attn_kernel,
        grid=(N_CTX_SEQ,),
        in_specs=[
            pl.BlockSpec((CTX_SEQ, D), lambda b: (b, 0)),
            pl.BlockSpec((CTX_SEQ, D), lambda b: (b, 1)),
            pl.BlockSpec((CTX_SEQ, D), lambda b: (b, 2)),
        ],
        out_specs=pl.BlockSpec((CTX_SEQ, D), lambda b: (b, 0)),
        out_shape=jax.ShapeDtypeStruct((n, D), BF16),
        compiler_params=pltpu.CompilerParams(
            dimension_semantics=("arbitrary",), vmem_limit_bytes=VMEM_LIMIT),
        name="ctx_attention",
    )(qkv, qkv, qkv)


def _bias_table_kernel(r_ref, e_ref, o_ref):
    x = r_ref[...]
    hi = x.astype(BF16)
    r1 = x - hi.astype(F32)
    mid = r1.astype(BF16)
    lo = (r1 - mid.astype(F32)).astype(BF16)
    e = e_ref[...]
    o_ref[...] = (jnp.dot(hi, e, preferred_element_type=F32)
                  + jnp.dot(mid, e, preferred_element_type=F32)
                  + jnp.dot(lo, e, preferred_element_type=F32))


def _na_static_tables():
    cq = np.arange(GRID_W)[:, None]
    kc = np.arange(GRID_W)[None, :]
    dcol = kc - cq + NA_KW - 1
    n_dc = 2 * NA_KW - 1
    onehot = np.zeros((128, GRID_W, GRID_W), np.float32)
    for dc in range(n_dc):
        onehot[dc] = (dcol == dc)
    win0 = np.clip(cq - NA_KW // 2, 0, GRID_W - NA_KW)
    mask = ((kc >= win0) & (kc < win0 + NA_KW)).astype(np.float32)
    return onehot.reshape(128, GRID_W * GRID_W), np.tile(mask, (1, NA_KH))


def _na_bias_table(rpb_e):
    n_dr = 2 * NA_KH - 1
    n_dc = 2 * NA_KW - 1
    onehot, _ = _na_static_tables()
    rows = N_HEADS * n_dr
    r2 = jnp.pad(rpb_e.reshape(rows, n_dc), ((0, 0), (0, 128 - n_dc)))
    tz = pl.pallas_call(
        _bias_table_kernel,
        grid=(1,),
        in_specs=[pl.BlockSpec((rows, 128), lambda i: (0, 0)),
                  pl.BlockSpec((128, GRID_W * GRID_W), lambda i: (0, 0))],
        out_specs=pl.BlockSpec((rows, GRID_W * GRID_W), lambda i: (0, 0)),
        out_shape=jax.ShapeDtypeStruct((rows, GRID_W * GRID_W), F32),
        compiler_params=pltpu.CompilerParams(vmem_limit_bytes=VMEM_LIMIT),
        name="na_bias_table",
    )(r2, jnp.asarray(onehot, BF16))
    tz = tz.reshape(N_HEADS, n_dr, GRID_W, GRID_W)
    return jnp.concatenate([tz[:, :-1], tz[:, 1:]], axis=-1)


def _na_kernel(q_ref, k_ref, v_ref, kc_ref, vc_ref, tz_ref, mask_ref, o_in_ref, o_ref):
    del o_in_ref
    kc = kc_ref[...].astype(BF16)
    vc = vc_ref[...].astype(BF16)
    mask = mask_ref[...] > 0.5
    band = NA_KH * GRID_W
    for r in range(GRID_H):
        kr0 = min(max(r - NA_KH // 2, 0), GRID_H - NA_KH)
        q = q_ref[r * GRID_W:(r + 1) * GRID_W, :].astype(BF16)
        kb = k_ref[kr0 * GRID_W:kr0 * GRID_W + band, :].astype(BF16)
        vb = v_ref[kr0 * GRID_W:kr0 * GRID_W + band, :].astype(BF16)
        s_loc = lax.dot_general(q, kb, (((1,), (1,)), ((), ())), preferred_element_type=F32) * ATTN_SCALE
        bias = jnp.concatenate(
            [tz_ref[kr0 + j - r + NA_KH - 1] for j in range(0, NA_KH, 2)], axis=-1)
        s_loc = jnp.where(mask, s_loc + bias, NEG)
        s_ctx = lax.dot_general(q, kc, (((1,), (1,)), ((), ())), preferred_element_type=F32) * ATTN_SCALE
        m = jnp.maximum(jnp.max(s_loc, axis=-1, keepdims=True), jnp.max(s_ctx, axis=-1, keepdims=True))
        e_loc = jnp.exp(s_loc - m)
        e_ctx = jnp.exp(s_ctx - m)
        l = jnp.sum(e_loc, axis=-1, keepdims=True) + jnp.sum(e_ctx, axis=-1, keepdims=True)
        o = (jnp.dot((e_loc / l).astype(BF16), vb, preferred_element_type=F32)
             + jnp.dot((e_ctx / l).astype(BF16), vc, preferred_element_type=F32))
        o_ref[r * GRID_W:(r + 1) * GRID_W, :] = o.astype(BF16)


def _na_attention(qkv, cache_k_e, cache_v_e, bias_table, o_ctx):
    _, mask = _na_static_tables()
    lat0 = N_CTX // LAT_SEQ
    n_dr2 = bias_table.shape[1]
    return pl.pallas_call(
        _na_kernel,
        grid=(N_LAT_SEQ, N_HEADS),
        in_specs=[
            pl.BlockSpec((LAT_SEQ, HEAD_DIM), lambda b, h: (lat0 + b, h)),
            pl.BlockSpec((LAT_SEQ, HEAD_DIM), lambda b, h: (lat0 + b, N_HEADS + h)),
            pl.BlockSpec((LAT_SEQ, HEAD_DIM), lambda b, h: (lat0 + b, 2 * N_HEADS + h)),
            pl.BlockSpec((None, CTX_SEQ, HEAD_DIM), lambda b, h: (b, 0, h)),
            pl.BlockSpec((None, CTX_SEQ, HEAD_DIM), lambda b, h: (b, 0, h)),
            pl.BlockSpec((None, n_dr2, GRID_W, 2 * GRID_W), lambda b, h: (h, 0, 0, 0)),
            pl.BlockSpec((GRID_W, NA_KH * GRID_W), lambda b, h: (0, 0)),
            pl.BlockSpec(memory_space=pl.ANY),
        ],
        out_specs=pl.BlockSpec((LAT_SEQ, HEAD_DIM), lambda b, h: (lat0 + b, h)),
        out_shape=jax.ShapeDtypeStruct(o_ctx.shape, o_ctx.dtype),
        input_output_aliases={7: 0},
        compiler_params=pltpu.CompilerParams(
            dimension_semantics=("arbitrary", "arbitrary"), vmem_limit_bytes=VMEM_LIMIT),
        name="na_attention",
    )(qkv, qkv, qkv, cache_k_e, cache_v_e, bias_table, jnp.asarray(mask), o_ctx)


def _split3(x):
    hi = x.astype(BF16)
    r1 = x - hi.astype(F32)
    mid = r1.astype(BF16)
    return hi, mid


def _router_kernel(x_ref, g_ref, sc_ref, sh_ref, rw_hi_ref, rw_lo_ref, rb_ref, h_ref, comb_ref):
    h = _modnorm(x_ref[...], g_ref[...], sc_ref[...], sh_ref[...])
    h_ref[...] = h.astype(BF16)
    h_hi, h_lo = _split3(h)
    logits = (jnp.dot(h_hi, rw_hi_ref[...], preferred_element_type=F32)
              + jnp.dot(h_lo, rw_hi_ref[...], preferred_element_type=F32)
              + jnp.dot(h_hi, rw_lo_ref[...], preferred_element_type=F32)) + rb_ref[...]
    lane = lax.broadcasted_iota(jnp.int32, logits.shape, 1)
    big = jnp.int32(1 << 20)
    neg_inf = -jnp.inf

    def first_argmax(vals, valid):
        v = jnp.where(valid, vals, neg_inf)
        m = jnp.max(v, axis=-1, keepdims=True)
        idx = jnp.min(jnp.where(jnp.logical_and(valid, v == m), lane, big), axis=-1, keepdims=True)
        return m, idx

    is_g = lane < N_GROUPS
    mg, gi = first_argmax(logits, is_g)
    pg = 1.0 / jnp.sum(jnp.where(is_g, jnp.exp(logits - mg), 0.0), axis=-1, keepdims=True)
    e_lane0 = EXPERT_LANE0 + gi * EXP_PER_GROUP
    is_e = jnp.logical_and(lane >= e_lane0, lane < e_lane0 + EXP_PER_GROUP)
    m1, i1 = first_argmax(logits, is_e)
    m2, i2 = first_argmax(logits, jnp.logical_and(is_e, lane != i1))
    e2 = jnp.exp(m2 - m1)
    den = 1.0 + e2
    w1 = (1.0 / den) * pg
    w2 = (e2 / den) * pg
    comb_ref[...] = jnp.where(lane == i1, w1, jnp.where(lane == i2, w2, 0.0))


def _router(x, mod, layer, norm_g, rw_hi, rw_lo, rb, tm=256):
    n = x.shape[0]
    return pl.pallas_call(
        _router_kernel,
        grid=(n // tm,),
        in_specs=[
            pl.BlockSpec((tm, D), lambda i: (i, 0)),
            pl.BlockSpec((1, D), lambda i: (0, 0)),
            _mod_spec(layer, 4, tm, row_axis=0),
            _mod_spec(layer, 3, tm, row_axis=0),
            pl.BlockSpec((D, ROUTER_LANES), lambda i: (0, 0)),
            pl.BlockSpec((D, ROUTER_LANES), lambda i: (0, 0)),
            pl.BlockSpec((1, ROUTER_LANES), lambda i: (0, 0)),
        ],
        out_specs=[
            pl.BlockSpec((tm, D), lambda i: (i, 0)),
            pl.BlockSpec((tm, ROUTER_LANES), lambda i: (i, 0)),
        ],
        out_shape=[
            jax.ShapeDtypeStruct((n, D), BF16),
            jax.ShapeDtypeStruct((n, ROUTER_LANES), F32),
        ],
        compiler_params=pltpu.CompilerParams(
            dimension_semantics=("arbitrary",), vmem_limit_bytes=VMEM_LIMIT),
        name="router",
    )(x, norm_g.reshape(1, D), mod, mod, rw_hi, rw_lo, rb)


def _moe_kernel(h_ref, comb_ref, w1_ref, w3_ref, w2_ref, o_ref, acc_ref, *, n_inner):
    e = pl.program_id(1)

    @pl.when(e == 0)
    def _():
        acc_ref[...] = jnp.zeros_like(acc_ref)

    h = h_ref[...]
    a = jnp.dot(h, w1_ref[...].astype(BF16), preferred_element_type=F32)
    b = jnp.dot(h, w3_ref[...].astype(BF16), preferred_element_type=F32)
    comb = comb_ref[...]
    lane = lax.broadcasted_iota(jnp.int32, comb.shape, 1)
    cw = jnp.sum(jnp.where(lane == EXPERT_LANE0 + e, comb, 0.0), axis=-1, keepdims=True)
    hid = (_silu(a) * b * cw).astype(BF16)
    acc_ref[...] += jnp.dot(hid, w2_ref[...].astype(BF16), preferred_element_type=F32)

    @pl.when(e == n_inner - 1)
    def _():
        o_ref[...] = acc_ref[...]


def _moe_dense(h_bf16, comb, w1, w3, w2, layer, tm=512):
    n = h_bf16.shape[0]
    return pl.pallas_call(
        functools.partial(_moe_kernel, n_inner=N_EXPERTS),
        grid=(n // tm, N_EXPERTS),
        in_specs=[
            pl.BlockSpec((tm, D), lambda i, e: (i, 0)),
            pl.BlockSpec((tm, ROUTER_LANES), lambda i, e: (i, 0)),
            pl.BlockSpec((None, None, D, D_EXPERT), lambda i, e: (layer, e, 0, 0)),
            pl.BlockSpec((None, None, D, D_EXPERT), lambda i, e: (layer, e, 0, 0)),
            pl.BlockSpec((None, None, D_EXPERT, D), lambda i, e: (layer, e, 0, 0)),
        ],
        out_specs=pl.BlockSpec((tm, D), lambda i, e: (i, 0)),
        out_shape=jax.ShapeDtypeStruct((n, D), F32),
        scratch_shapes=[pltpu.VMEM((tm, D), F32)],
        compiler_params=pltpu.CompilerParams(
            dimension_semantics=("arbitrary", "arbitrary"), vmem_limit_bytes=VMEM_LIMIT),
        name="moe_experts",
    )(h_bf16, comb, w1, w3, w2)


def _res_kernel(x_ref, y_ref, gate_ref, o_ref):
    o_ref[...] = x_ref[...] + gate_ref[...] * y_ref[...]


def _residual(x, y, mod, layer, tm=512):
    n = x.shape[0]
    return pl.pallas_call(
        _res_kernel,
        grid=(n // tm,),
        in_specs=[
            pl.BlockSpec((tm, D), lambda i: (i, 0)),
            pl.BlockSpec((tm, D), lambda i: (i, 0)),
            _mod_spec(layer, 5, tm, row_axis=0),
        ],
        out_specs=pl.BlockSpec((tm, D), lambda i: (i, 0)),
        out_shape=jax.ShapeDtypeStruct((n, D), F32),
        compiler_params=pltpu.CompilerParams(
            dimension_semantics=("arbitrary",), vmem_limit_bytes=VMEM_LIMIT),
        name="moe_residual",
    )(x, y, mod)


def _router_weights(rgw, rgb, rew, reb):
    rw = jnp.concatenate([rgw, rew.reshape(D, N_EXPERTS)], axis=1)
    rw = jnp.pad(rw, ((0, 0), (0, ROUTER_LANES - rw.shape[1])))
    rb = jnp.concatenate([rgb, reb.reshape(N_EXPERTS)])
    rb = jnp.pad(rb, (0, ROUTER_LANES - rb.shape[0])).reshape(1, ROUTER_LANES)
    hi = rw.astype(BF16)
    lo = (rw - hi.astype(F32)).astype(BF16)
    return hi, lo, rb


def kernel(x_prompt, x_sample, cache_k, cache_v, c, c_ctx, ada_w, ada_b, norm_mix_g, norm_ffn_g,
           mix_in_w, pool_w, pool_scale, conv_w, conv_b, conv_ln_g, conv_ln_b, mix_out_w,
           qkv_w, q_norm_g, k_norm_g, rpb, attn_out_w, router_g_w, router_g_b, router_e_w,
           router_e_b, exp_w1, exp_w3, exp_w2):
    x = jnp.concatenate([x_prompt.reshape(N_CTX, D), x_sample.reshape(N_LAT, D)], axis=0)
    cvec8 = jnp.concatenate([c_ctx[None, :], c, jnp.zeros((8 - 1 - N_LAT_SEQ, D), F32)], axis=0)
    mod = _adaln_all(cvec8, ada_w, ada_b)
    n_attn = DEPTH // 2
    cache_k2 = cache_k.reshape(N_LAT_SEQ, n_attn, CTX_SEQ, D)
    cache_v2 = cache_v.reshape(N_LAT_SEQ, n_attn, CTX_SEQ, D)

    new_k, new_v = [], []
    for l in range(DEPTH):
        e = l // 2
        if l % 2 == 0:
            u = _mm_prologue(x, mod, l, norm_mix_g[l], mix_in_w[e].astype(BF16))
            cat = _seq_mixer(u, pool_w[e].astype(BF16), pool_scale[e], conv_w[e], conv_b[e],
                             conv_ln_g[e], conv_ln_b[e])
            x = _mm_residual(cat, mix_out_w[e].astype(BF16), x, mod, l)
        else:
            gains = jnp.concatenate([jnp.tile(q_norm_g[e], N_HEADS), jnp.tile(k_norm_g[e], N_HEADS)])
            qkv = _mm_prologue(x, mod, l, norm_mix_g[l], qkv_w[e].astype(BF16),
                               head_gains=gains.reshape(1, 2 * D))
            new_k.append(qkv[:N_CTX, D:2 * D].reshape(N_CTX_SEQ, CTX_SEQ, N_HEADS, HEAD_DIM))
            new_v.append(qkv[:N_CTX, 2 * D:].reshape(N_CTX_SEQ, CTX_SEQ, N_HEADS, HEAD_DIM))
            o = _ctx_attention(qkv)
            o = _na_attention(qkv, cache_k2[:, e], cache_v2[:, e], _na_bias_table(rpb[e]), o)
            x = _mm_residual(o, attn_out_w[e].astype(BF16), x, mod, l)
        rw_hi, rw_lo, rb = _router_weights(router_g_w[l], router_g_b[l], router_e_w[l], router_e_b[l])
        h2, comb = _router(x, mod, l, norm_ffn_g[l], rw_hi, rw_lo, rb)
        y = _moe_dense(h2, comb, exp_w1, exp_w3, exp_w2, l)
        x = _residual(x, y, mod, l)

    y_prompt = x[:N_CTX].reshape(N_CTX_SEQ, CTX_SEQ, D)
    y_sample = x[N_CTX:].reshape(N_LAT_SEQ, LAT_SEQ, D)
    return (y_prompt, y_sample, jnp.stack(new_k, axis=1), jnp.stack(new_v, axis=1))
```

```python
import functools

import numpy as np
import jax
import jax.numpy as jnp
from jax import lax
from jax.experimental import pallas as pl
from jax.experimental.pallas import tpu as pltpu

D = 2048
N_CTX_SEQ = 16
CTX_SEQ = 256
N_LAT_SEQ = 2
LAT_SEQ = 1024
N_CTX = N_CTX_SEQ * CTX_SEQ
N_LAT = N_LAT_SEQ * LAT_SEQ
N_TOK = N_CTX + N_LAT
DEPTH = 4
GRID_W = 64
GRID_H = LAT_SEQ // GRID_W
N_HEADS = 16
HEAD_DIM = 128
ATTN_SCALE = HEAD_DIM ** -0.5
D_POOL = 1024
D_CONV = 1024
POOL_WINDOWS = (2, 4, 8, 16)
POOL_GC = 256
CONV_W = 31
NA_KH = 8
NA_KW = 16
N_GROUPS = 4
EXP_PER_GROUP = 4
N_EXPERTS = 16
D_EXPERT = 512
EPS = 1e-6
NEG = -1e30

SEQ_TILE = 256
HALO = 16
ROUTER_LANES = 128
EXPERT_LANE0 = N_GROUPS
VMEM_LIMIT = 56 * 1024 * 1024

TOK_TILE = 256
N_TOK_TILES = N_TOK // TOK_TILE
ROW_ALIGN = 16
SUB_ROWS = 64
N_SUB = TOK_TILE // SUB_ROWS
SLACK = SUB_ROWS - ROW_ALIGN
MOE_TILE = 1024
MOE_SUB = 256
FF_CHUNK = 256
N_FF_CHUNKS = EXP_PER_GROUP * D_EXPERT // FF_CHUNK
ZERO_ROWS = 2 * TOK_TILE
MAX_PADDED = N_TOK + N_TOK_TILES * N_GROUPS * (ROW_ALIGN - 1)
MOE_TILES_MAX = (MAX_PADDED + N_GROUPS * (SLACK + MOE_TILE - 1)) // MOE_TILE
ROW_CAP = -(-(MAX_PADDED + N_GROUPS * (ZERO_ROWS + MOE_TILE - 1)) // MOE_TILE) * MOE_TILE

F32 = jnp.float32
BF16 = jnp.bfloat16


def _cond_of_row(row):
    return jnp.where(row < N_CTX, 0, 1 + (row - N_CTX) // LAT_SEQ)


def _mod_spec(layer, which, tm, tn=D, row_axis=1, col_axis=None):
    def index_map(*ids):
        cond = _cond_of_row(ids[row_axis] * tm)
        col = 0 if col_axis is None else ids[col_axis]
        return (layer, cond, which, 0, col)
    return pl.BlockSpec((None, None, None, 1, tn), index_map)


def _silu(x):
    return x / (1.0 + jnp.exp(-x))


def _modnorm(x, g, sc, sh):
    ms = jnp.mean(x * x, axis=-1, keepdims=True)
    y = x * lax.rsqrt(ms + EPS) * g
    return y * (1.0 + sc) + sh


def _params(*sem):
    return pltpu.CompilerParams(dimension_semantics=sem, vmem_limit_bytes=VMEM_LIMIT)


def _ada_kernel(c_ref, w_ref, b_ref, o_ref):
    s = _silu(c_ref[...]).astype(BF16)
    o_ref[...] = jnp.dot(s, w_ref[...].astype(BF16), preferred_element_type=F32) + b_ref[...]


def _adaln_all(cvec8, ada_w, ada_b):
    tn = 1024
    n_out = 6 * D
    out = pl.pallas_call(
        _ada_kernel,
        grid=(DEPTH, n_out // tn),
        in_specs=[
            pl.BlockSpec((8, D), lambda l, j: (0, 0)),
            pl.BlockSpec((None, D, tn), lambda l, j: (l, 0, j)),
            pl.BlockSpec((None, 1, tn), lambda l, j: (l, 0, j)),
        ],
        out_specs=pl.BlockSpec((None, 8, tn), lambda l, j: (l, 0, j)),
        out_shape=jax.ShapeDtypeStruct((DEPTH, 8, n_out), F32),
        compiler_params=_params("arbitrary", "arbitrary"),
        name="adaln",
    )(cvec8, ada_w, ada_b.reshape(DEPTH, 1, n_out))
    return out[:, :3].reshape(DEPTH, 3, 6, 1, D)


def _mm_pro_kernel(x_ref, g_ref, sc_ref, sh_ref, w_ref, o_ref):
    h = _modnorm(x_ref[...], g_ref[...], sc_ref[...], sh_ref[...]).astype(BF16)
    o_ref[...] = jnp.dot(h, w_ref[...], preferred_element_type=F32)


def _mm_prologue(x, mod, layer, norm_g, w_bf16, tm=512, tn=1024):
    n, _ = x.shape
    f = w_bf16.shape[1]
    return pl.pallas_call(
        _mm_pro_kernel,
        grid=(f // tn, n // tm),
        in_specs=[
            pl.BlockSpec((tm, D), lambda j, i: (i, 0)),
            pl.BlockSpec((1, D), lambda j, i: (0, 0)),
            _mod_spec(layer, 1, tm),
            _mod_spec(layer, 0, tm),
            pl.BlockSpec((D, tn), lambda j, i: (0, j)),
        ],
        out_specs=pl.BlockSpec((tm, tn), lambda j, i: (i, j)),
        out_shape=jax.ShapeDtypeStruct((n, f), F32),
        compiler_params=_params("arbitrary", "arbitrary"),
        name="mm_prologue",
    )(x, norm_g.reshape(1, D), mod, mod, w_bf16)


def _qkv_kernel(x_ref, g_ref, sc_ref, sh_ref, wq_ref, wk_ref, wv_ref, qg_ref, kg_ref, *rest,
                tn, n_ctx_tiles, seqs_per_tile):
    q_ref, k_ref, v_ref, ck_ref, cv_ref = rest[-5:]
    i = pl.program_id(1)
    h = _modnorm(x_ref[...], g_ref[...], sc_ref[...], sh_ref[...]).astype(BF16)
    q = jnp.dot(h, wq_ref[...], preferred_element_type=F32)
    k = jnp.dot(h, wk_ref[...], preferred_element_type=F32)
    v = jnp.dot(h, wv_ref[...], preferred_element_type=F32)
    is_ctx = i < n_ctx_tiles
    for hh in range(tn // HEAD_DIM):
        ls = slice(hh * HEAD_DIM, (hh + 1) * HEAD_DIM)
        qh = q[:, ls]
        kh = k[:, ls]
        qn = qh * lax.rsqrt(jnp.mean(qh * qh, axis=-1, keepdims=True) + EPS) * qg_ref[:, ls]
        kn = kh * lax.rsqrt(jnp.mean(kh * kh, axis=-1, keepdims=True) + EPS) * kg_ref[:, ls]
        q_ref[:, ls] = qn.astype(BF16)
        k_ref[:, ls] = kn.astype(BF16)

        @pl.when(is_ctx)
        def _():
            ck_ref[:, :, ls] = kn.reshape(seqs_per_tile, CTX_SEQ, HEAD_DIM)

    v_ref[...] = v.astype(BF16)

    @pl.when(is_ctx)
    def _():
        cv_ref[...] = v.reshape(seqs_per_tile, CTX_SEQ, tn)


def _qkv_proj(x, mod, layer, attn_idx, norm_g, w_bf16, q_gain, k_gain, caches, tm=512, tn=512):
    n = x.shape[0]
    n_attn = DEPTH // 2
    ncol = D // tn
    seqs_per_tile = tm // CTX_SEQ
    n_ctx_tiles = N_CTX // tm
    cache_shape = jax.ShapeDtypeStruct((N_CTX_SEQ, n_attn, CTX_SEQ, D), F32)
    cache_spec = pl.BlockSpec(
        (seqs_per_tile, None, CTX_SEQ, tn),
        lambda j, i: (jnp.minimum(i, n_ctx_tiles - 1), attn_idx, 0, j))
    in_specs = [
        pl.BlockSpec((tm, D), lambda j, i: (i, 0)),
        pl.BlockSpec((1, D), lambda j, i: (0, 0)),
        _mod_spec(layer, 1, tm),
        _mod_spec(layer, 0, tm),
        pl.BlockSpec((D, tn), lambda j, i: (0, j)),
        pl.BlockSpec((D, tn), lambda j, i: (0, ncol + j)),
        pl.BlockSpec((D, tn), lambda j, i: (0, 2 * ncol + j)),
        pl.BlockSpec((1, tn), lambda j, i: (0, j)),
        pl.BlockSpec((1, tn), lambda j, i: (0, j)),
    ]
    args = [x, norm_g.reshape(1, D), mod, mod, w_bf16, w_bf16, w_bf16,
            jnp.tile(q_gain, N_HEADS).reshape(1, D), jnp.tile(k_gain, N_HEADS).reshape(1, D)]
    aliases = {}
    if caches is not None:
        in_specs += [pl.BlockSpec(memory_space=pl.ANY), pl.BlockSpec(memory_space=pl.ANY)]
        aliases = {len(args): 3, len(args) + 1: 4}
        args += list(caches)
    act_spec = pl.BlockSpec((tm, tn), lambda j, i: (i, j))
    act_shape = jax.ShapeDtypeStruct((n, D), BF16)
    return pl.pallas_call(
        functools.partial(_qkv_kernel, tn=tn, n_ctx_tiles=n_ctx_tiles, seqs_per_tile=seqs_per_tile),
        grid=(ncol, n // tm),
        in_specs=in_specs,
        out_specs=[act_spec, act_spec, act_spec, cache_spec, cache_spec],
        out_shape=[act_shape, act_shape, act_shape, cache_shape, cache_shape],
        input_output_aliases=aliases,
        compiler_params=_params("arbitrary", "arbitrary"),
        name="qkv_proj",
    )(*args)


def _mm_res_kernel(a_ref, w_ref, x_ref, gate_ref, o_ref):
    y = jnp.dot(a_ref[...], w_ref[...], preferred_element_type=F32)
    o_ref[...] = x_ref[...] + gate_ref[...] * y


def _mm_residual(a_bf16, w_bf16, x, mod, layer, tm=512, tn=1024):
    n, k = a_bf16.shape
    return pl.pallas_call(
        _mm_res_kernel,
        grid=(D // tn, n // tm),
        in_specs=[
            pl.BlockSpec((tm, k), lambda j, i: (i, 0)),
            pl.BlockSpec((k, tn), lambda j, i: (0, j)),
            pl.BlockSpec((tm, tn), lambda j, i: (i, j)),
            _mod_spec(layer, 2, tm, tn=tn, col_axis=0),
        ],
        out_specs=pl.BlockSpec((tm, tn), lambda j, i: (i, j)),
        out_shape=jax.ShapeDtypeStruct((n, D), F32),
        compiler_params=_params("arbitrary", "arbitrary"),
        name="mm_residual",
    )(a_bf16, w_bf16, x, mod)


def _seq_kernel(u_ref, top_ref, bot_ref, pw_ref, ps_ref, cw_ref, cb_ref, lg_ref, lb_ref,
                o_ref, zpad_ref, conv_ref):
    i = pl.program_id(0)
    n_ctx_tiles = N_CTX // SEQ_TILE
    tiles_per_lat = LAT_SEQ // SEQ_TILE
    is_lat = i >= n_ctx_tiles
    chunk = jnp.where(is_lat, (i - n_ctx_tiles) % tiles_per_lat, 0)
    top_ok = jnp.logical_and(is_lat, chunk > 0)
    bot_ok = jnp.logical_and(is_lat, chunk < tiles_per_lat - 1)
    seq_len = jnp.where(is_lat, LAT_SEQ, CTX_SEQ)
    t = chunk * SEQ_TILE + lax.broadcasted_iota(jnp.int32, (SEQ_TILE, 1), 0)
    rows = SEQ_TILE + 2 * HALO

    for g, win in enumerate(POOL_WINDOWS):
        ls = slice(g * POOL_GC, (g + 1) * POOL_GC)
        mid = u_ref[:, ls]
        top = jnp.where(top_ok, top_ref[:, ls], 0.0)
        bot = jnp.where(bot_ok, bot_ref[:, ls], 0.0)
        up = jnp.concatenate([top, mid, bot], axis=0)
        s = pltpu.roll(up, 1, 0) + up
        if win >= 4:
            s = pltpu.roll(s, 1, 0) + pltpu.roll(s, rows - 1, 0)
        if win >= 8:
            s = pltpu.roll(s, 2, 0) + pltpu.roll(s, rows - 2, 0)
        if win >= 16:
            s = pltpu.roll(s, 4, 0) + pltpu.roll(s, rows - 4, 0)
        s = s[HALO:HALO + SEQ_TILE]
        lo = jnp.maximum(t - win // 2, 0)
        hi = jnp.minimum(t + (win - win // 2) - 1, seq_len - 1)
        cnt = (hi - lo + 1).astype(F32)
        p = (s / cnt - mid).astype(BF16)
        y = jnp.dot(p, pw_ref[g], preferred_element_type=F32) * ps_ref[:, ls]
        o_ref[:, ls] = y.astype(BF16)

    def glu(ref):
        return ref[:, D_POOL:D_POOL + D_CONV] * (1.0 / (1.0 + jnp.exp(-ref[:, D_POOL + D_CONV:])))

    zpad_ref[0:HALO, :] = jnp.where(top_ok, glu(top_ref), 0.0)
    zpad_ref[HALO:HALO + SEQ_TILE, :] = glu(u_ref)
    zpad_ref[HALO + SEQ_TILE:rows, :] = jnp.where(bot_ok, glu(bot_ref), 0.0)
    for lt in range(D_CONV // 128):
        ls = slice(lt * 128, (lt + 1) * 128)
        zp = zpad_ref[:, ls]
        acc = jnp.zeros((SEQ_TILE, 128), F32)
        for b in range(8):
            sb = zp if b == 0 else pltpu.roll(zp, rows - b, 0)
            for a in range(4):
                off = 8 * a + b
                if 1 <= off <= CONV_W:
                    acc = acc + cw_ref[off - 1:off, ls] * sb[8 * a:8 * a + SEQ_TILE]
        conv_ref[:, ls] = acc + cb_ref[:, ls]
    zc = conv_ref[...]
    mu = jnp.mean(zc, axis=-1, keepdims=True)
    d = zc - mu
    var = jnp.mean(d * d, axis=-1, keepdims=True)
    zn = d * lax.rsqrt(var + EPS) * lg_ref[...] + lb_ref[...]
    o_ref[:, D_POOL:] = _silu(zn).astype(BF16)


def _seq_mixer(u, pool_w_bf16, pool_scale, conv_w, conv_b, ln_g, ln_b):
    n = u.shape[0]
    f_in = u.shape[1]
    n_tiles = n // SEQ_TILE
    hb = SEQ_TILE // HALO
    n_hblocks = n // HALO
    return pl.pallas_call(
        _seq_kernel,
        grid=(n_tiles,),
        in_specs=[
            pl.BlockSpec((SEQ_TILE, f_in), lambda i: (i, 0)),
            pl.BlockSpec((HALO, f_in), lambda i: (jnp.maximum(i * hb - 1, 0), 0)),
            pl.BlockSpec((HALO, f_in), lambda i: (jnp.minimum((i + 1) * hb, n_hblocks - 1), 0)),
            pl.BlockSpec((len(POOL_WINDOWS), POOL_GC, POOL_GC), lambda i: (0, 0, 0)),
            pl.BlockSpec((1, D_POOL), lambda i: (0, 0)),
            pl.BlockSpec((CONV_W, D_CONV), lambda i: (0, 0)),
            pl.BlockSpec((1, D_CONV), lambda i: (0, 0)),
            pl.BlockSpec((1, D_CONV), lambda i: (0, 0)),
            pl.BlockSpec((1, D_CONV), lambda i: (0, 0)),
        ],
        out_specs=pl.BlockSpec((SEQ_TILE, D_POOL + D_CONV), lambda i: (i, 0)),
        out_shape=jax.ShapeDtypeStruct((n, D_POOL + D_CONV), BF16),
        scratch_shapes=[
            pltpu.VMEM((SEQ_TILE + 2 * HALO, D_CONV), F32),
            pltpu.VMEM((SEQ_TILE, D_CONV), F32),
        ],
        compiler_params=_params("arbitrary"),
        name="seq_mixer",
    )(u, u, u, pool_w_bf16, pool_scale.reshape(1, D_POOL), conv_w, conv_b.reshape(1, D_CONV),
      ln_g.reshape(1, D_CONV), ln_b.reshape(1, D_CONV))


_NT = (((1,), (1,)), ((), ()))


def _ctx_attn_kernel(q_ref, k_ref, v_ref, o_ref, s_ref, p_ref):
    for h in range(N_HEADS):
        ls = slice(h * HEAD_DIM, (h + 1) * HEAD_DIM)
        s_ref[h] = lax.dot_general(q_ref[:, ls], k_ref[:, ls], _NT, preferred_element_type=F32) * ATTN_SCALE
    for h in range(N_HEADS):
        s = s_ref[h]
        e = jnp.exp(s - jnp.max(s, axis=-1, keepdims=True))
        p_ref[h] = (e * (1.0 / jnp.sum(e, axis=-1, keepdims=True))).astype(BF16)
    for h in range(N_HEADS):
        ls = slice(h * HEAD_DIM, (h + 1) * HEAD_DIM)
        o_ref[:, ls] = jnp.dot(p_ref[h], v_ref[:, ls], preferred_element_type=F32).astype(BF16)


def _ctx_attention(q, k, v):
    n = q.shape[0]
    spec = pl.BlockSpec((CTX_SEQ, D), lambda b: (b, 0))
    return pl.pallas_call(
        _ctx_attn_kernel,
        grid=(N_CTX_SEQ,),
        in_specs=[spec, spec, spec],
        out_specs=spec,
        out_shape=jax.ShapeDtypeStruct((n, D), BF16),
        scratch_shapes=[pltpu.VMEM((N_HEADS, CTX_SEQ, CTX_SEQ), F32),
                        pltpu.VMEM((N_HEADS, CTX_SEQ, CTX_SEQ), BF16)],
        compiler_params=_params("arbitrary"),
        name="ctx_attention",
    )(q, k, v)


def _bias_table_kernel(r_ref, e_ref, o_ref):
    x = r_ref[...]
    hi = x.astype(BF16)
    r1 = x - hi.astype(F32)
    mid = r1.astype(BF16)
    lo = (r1 - mid.astype(F32)).astype(BF16)
    e = e_ref[...]
    o_ref[...] = (jnp.dot(hi, e, preferred_element_type=F32)
                  + jnp.dot(mid, e, preferred_element_type=F32)
                  + jnp.dot(lo, e, preferred_element_type=F32))


def _na_static_tables():
    cq = np.arange(GRID_W)[:, None]
    kc = np.arange(GRID_W)[None, :]
    dcol = kc - cq + NA_KW - 1
    n_dc = 2 * NA_KW - 1
    onehot = np.zeros((128, GRID_W, GRID_W), np.float32)
    for dc in range(n_dc):
        onehot[dc] = (dcol == dc)
    win0 = np.clip(cq - NA_KW // 2, 0, GRID_W - NA_KW)
    mask = ((kc >= win0) & (kc < win0 + NA_KW)).astype(np.float32)
    return onehot.reshape(128, GRID_W * GRID_W), np.tile(mask, (1, NA_KH))


def _na_bias_table(rpb_e):
    n_dr = 2 * NA_KH - 1
    n_dc = 2 * NA_KW - 1
    onehot, _ = _na_static_tables()
    rows = N_HEADS * n_dr
    r2 = jnp.pad(rpb_e.reshape(rows, n_dc), ((0, 0), (0, 128 - n_dc)))
    tz = pl.pallas_call(
        _bias_table_kernel,
        grid=(1,),
        in_specs=[pl.BlockSpec((rows, 128), lambda i: (0, 0)),
                  pl.BlockSpec((128, GRID_W * GRID_W), lambda i: (0, 0))],
        out_specs=pl.BlockSpec((rows, GRID_W * GRID_W), lambda i: (0, 0)),
        out_shape=jax.ShapeDtypeStruct((rows, GRID_W * GRID_W), F32),
        compiler_params=pltpu.CompilerParams(vmem_limit_bytes=VMEM_LIMIT),
        name="na_bias_table",
    )(r2, jnp.asarray(onehot, BF16))
    tz = tz.reshape(N_HEADS, n_dr, GRID_W, GRID_W)
    return jnp.concatenate([tz[:, :-1], tz[:, 1:]], axis=-1)


def _na_key_row0(r):
    return min(max(r - NA_KH // 2, 0), GRID_H - NA_KH)


def _na_segments():
    segs = []
    for r in range(GRID_H):
        kr0 = _na_key_row0(r)
        if segs and segs[-1][2] == kr0:
            segs[-1][1] = r + 1
        else:
            segs.append([r, r + 1, kr0])
    return segs


def _na_kernel(q_ref, k_ref, v_ref, kc_ref, vc_ref, tz_ref, mask_ref, o_in_ref, o_ref,
               s_ref, p_ref, acc_ref):
    del o_in_ref
    band = NA_KH * GRID_W
    n_ctx = CTX_SEQ
    kc = kc_ref[...].astype(BF16)
    vc = vc_ref[...].astype(BF16)
    for r0, r1, kr0 in _na_segments():
        rs = slice(r0 * GRID_W, r1 * GRID_W)
        kb = k_ref[kr0 * GRID_W:kr0 * GRID_W + band, :]
        s_ref[rs, 0:band] = lax.dot_general(q_ref[rs, :], kb, _NT, preferred_element_type=F32) * ATTN_SCALE
    s_ref[:, band:band + n_ctx] = lax.dot_general(q_ref[...], kc, _NT, preferred_element_type=F32) * ATTN_SCALE
    mask = mask_ref[...] > 0.5
    for r in range(GRID_H):
        rs = slice(r * GRID_W, (r + 1) * GRID_W)
        kr0 = _na_key_row0(r)
        bias = jnp.concatenate(
            [tz_ref[kr0 + j - r + NA_KH - 1] for j in range(0, NA_KH, 2)], axis=-1)
        s_loc = jnp.where(mask, s_ref[rs, 0:band] + bias, NEG)
        s_ctx = s_ref[rs, band:band + n_ctx]
        m = jnp.maximum(jnp.max(s_loc, axis=-1, keepdims=True), jnp.max(s_ctx, axis=-1, keepdims=True))
        e_loc = jnp.exp(s_loc - m)
        e_ctx = jnp.exp(s_ctx - m)
        inv = 1.0 / (jnp.sum(e_loc, axis=-1, keepdims=True) + jnp.sum(e_ctx, axis=-1, keepdims=True))
        p_ref[rs, 0:band] = (e_loc * inv).astype(BF16)
        p_ref[rs, band:band + n_ctx] = (e_ctx * inv).astype(BF16)
    for r0, r1, kr0 in _na_segments():
        rs = slice(r0 * GRID_W, r1 * GRID_W)
        vb = v_ref[kr0 * GRID_W:kr0 * GRID_W + band, :]
        acc_ref[rs, :] = jnp.dot(p_ref[rs, 0:band], vb, preferred_element_type=F32)
    o_ctx = jnp.dot(p_ref[:, band:band + n_ctx], vc, preferred_element_type=F32)
    o_ref[...] = (acc_ref[...] + o_ctx).astype(BF16)


def _na_attention(q, k, v, cache_k_e, cache_v_e, bias_table, o_ctx):
    _, mask = _na_static_tables()
    lat0 = N_CTX // LAT_SEQ
    n_dr2 = bias_table.shape[1]
    n_keys = NA_KH * GRID_W + CTX_SEQ
    tok_spec = pl.BlockSpec((LAT_SEQ, HEAD_DIM), lambda b, h: (lat0 + b, h))
    cache_spec = pl.BlockSpec((None, CTX_SEQ, HEAD_DIM), lambda b, h: (b, 0, h))
    return pl.pallas_call(
        _na_kernel,
        grid=(N_LAT_SEQ, N_HEADS),
        in_specs=[
            tok_spec, tok_spec, tok_spec, cache_spec, cache_spec,
            pl.BlockSpec((None, n_dr2, GRID_W, 2 * GRID_W), lambda b, h: (h, 0, 0, 0)),
            pl.BlockSpec((GRID_W, NA_KH * GRID_W), lambda b, h: (0, 0)),
            pl.BlockSpec(memory_space=pl.ANY),
        ],
        out_specs=tok_spec,
        out_shape=jax.ShapeDtypeStruct(o_ctx.shape, o_ctx.dtype),
        input_output_aliases={7: 0},
        scratch_shapes=[pltpu.VMEM((LAT_SEQ, n_keys), F32),
                        pltpu.VMEM((LAT_SEQ, n_keys), BF16),
                        pltpu.VMEM((LAT_SEQ, HEAD_DIM), F32)],
        compiler_params=_params("arbitrary", "arbitrary"),
        name="na_attention",
    )(q, k, v, cache_k_e, cache_v_e, bias_table, jnp.asarray(mask), o_ctx)


def _split_bf16(x, pieces):
    out = []
    for _ in range(pieces):
        p = x.astype(BF16)
        out.append(p)
        x = x - p.astype(F32)
    return out


def _router_kernel(x_ref, g_ref, sc_ref, sh_ref, rw_hi_ref, rw_lo_ref, rb_ref, h_ref, comb_ref, cnt_ref):
    h = _modnorm(x_ref[...], g_ref[...], sc_ref[...], sh_ref[...])
    h_ref[...] = h.astype(BF16)
    h_hi, h_lo = _split_bf16(h, 2)
    logits = (jnp.dot(h_hi, rw_hi_ref[...], preferred_element_type=F32)
              + jnp.dot(h_lo, rw_hi_ref[...], preferred_element_type=F32)
              + jnp.dot(h_hi, rw_lo_ref[...], preferred_element_type=F32)) + rb_ref[...]
    lane = lax.broadcasted_iota(jnp.int32, logits.shape, 1)
    big = jnp.int32(1 << 20)
    neg_inf = -jnp.inf

    def first_argmax(vals, valid):
        v = jnp.where(valid, vals, neg_inf)
        m = jnp.max(v, axis=-1, keepdims=True)
        idx = jnp.min(jnp.where(jnp.logical_and(valid, v == m), lane, big), axis=-1, keepdims=True)
        return m, idx

    is_g = lane < N_GROUPS
    mg, gi = first_argmax(logits, is_g)
    pg = 1.0 / jnp.sum(jnp.where(is_g, jnp.exp(logits - mg), 0.0), axis=-1, keepdims=True)
    e_lane0 = EXPERT_LANE0 + gi * EXP_PER_GROUP
    is_e = jnp.logical_and(lane >= e_lane0, lane < e_lane0 + EXP_PER_GROUP)
    m1, i1 = first_argmax(logits, is_e)
    m2, i2 = first_argmax(logits, jnp.logical_and(is_e, lane != i1))
    e2 = jnp.exp(m2 - m1)
    den = 1.0 + e2
    w1 = (1.0 / den) * pg
    w2 = (e2 / den) * pg
    onehot_g = jnp.where(lane == gi, 1.0, 0.0)
    comb_ref[...] = jnp.where(lane == i1, w1, jnp.where(lane == i2, w2, onehot_g))
    cnt_ref[...] = jnp.sum(onehot_g, axis=0, keepdims=True)


def _router(x, mod, layer, norm_g, rw_hi, rw_lo, rb):
    n = x.shape[0]
    tm = TOK_TILE
    return pl.pallas_call(
        _router_kernel,
        grid=(n // tm,),
        in_specs=[
            pl.BlockSpec((tm, D), lambda i: (i, 0)),
            pl.BlockSpec((1, D), lambda i: (0, 0)),
            _mod_spec(layer, 4, tm, row_axis=0),
            _mod_spec(layer, 3, tm, row_axis=0),
            pl.BlockSpec((D, ROUTER_LANES), lambda i: (0, 0)),
            pl.BlockSpec((D, ROUTER_LANES), lambda i: (0, 0)),
            pl.BlockSpec((1, ROUTER_LANES), lambda i: (0, 0)),
        ],
        out_specs=[
            pl.BlockSpec((tm, D), lambda i: (i, 0)),
            pl.BlockSpec((tm, ROUTER_LANES), lambda i: (i, 0)),
            pl.BlockSpec((None, 1, ROUTER_LANES), lambda i: (i, 0, 0)),
        ],
        out_shape=[
            jax.ShapeDtypeStruct((n, D), BF16),
            jax.ShapeDtypeStruct((n, ROUTER_LANES), F32),
            jax.ShapeDtypeStruct((n // tm, 1, ROUTER_LANES), F32),
        ],
        compiler_params=_params("arbitrary"),
        name="router",
    )(x, norm_g.reshape(1, D), mod, mod, rw_hi, rw_lo, rb)


def _router_weights(rgw, rgb, rew, reb):
    rw = jnp.concatenate([rgw, rew.reshape(D, N_EXPERTS)], axis=1)
    rw = jnp.pad(rw, ((0, 0), (0, ROUTER_LANES - rw.shape[1])))
    rb = jnp.concatenate([rgb, reb.reshape(N_EXPERTS)])
    rb = jnp.pad(rb, (0, ROUTER_LANES - rb.shape[0])).reshape(1, ROUTER_LANES)
    hi = rw.astype(BF16)
    lo = (rw - hi.astype(F32)).astype(BF16)
    return hi, lo, rb


def _moe_plan(cnt):
    i32 = jnp.int32
    padded = (cnt + (ROW_ALIGN - 1)) // ROW_ALIGN * ROW_ALIGN
    length = jnp.sum(padded, axis=0)
    span = (length + ZERO_ROWS + MOE_TILE - 1) // MOE_TILE * MOE_TILE
    start = jnp.cumsum(span) - span
    off = start[None, :] + jnp.cumsum(padded, axis=0) - padded
    end = start + length
    need = jnp.where(length > 0, length + SLACK, 0)
    ntile = (need + MOE_TILE - 1) // MOE_TILE
    cum = jnp.cumsum(ntile)
    total = cum[-1]
    k = jnp.arange(MOE_TILES_MAX, dtype=i32)
    kk = jnp.minimum(k, jnp.maximum(total - 1, 0))
    grp = jnp.minimum(jnp.sum((kk[:, None] >= cum[None, :]).astype(i32), axis=1), N_GROUPS - 1)
    j = kk - (cum - ntile)[grp]
    blk = start[grp] // MOE_TILE + j
    nsub = jnp.clip((need[grp] - j * MOE_TILE + MOE_SUB - 1) // MOE_SUB, 0, MOE_TILE // MOE_SUB)
    nsub = jnp.where(k < total, nsub, 0)
    return (off.reshape(-1).astype(i32), padded.reshape(-1).astype(i32), end.astype(i32),
            blk.astype(i32), grp.astype(i32), nsub.astype(i32))


def _sub_needed(padded_ref, tile, g, s):
    return s * SUB_ROWS < padded_ref[tile * N_GROUPS + g]


def _dispatch_kernel(off_ref, padded_ref, end_ref, h_ref, comb_ref, hs_ref, cs_ref,
                     ch_ref, cc_ref, zh_ref, zc_ref, sem_h, sem_c, sem_z):
    i = pl.program_id(0)
    n_steps = pl.num_programs(0)
    slot = i % 2
    comb = comb_ref[...]
    oh_t = comb.T
    row = lax.broadcasted_iota(jnp.int32, (TOK_TILE, TOK_TILE), 0)
    col = lax.broadcasted_iota(jnp.int32, (TOK_TILE, TOK_TILE), 1)
    before = jnp.where(row < col, 1.0, 0.0).astype(BF16)
    rank_t = jnp.dot(oh_t.astype(BF16), before, preferred_element_type=F32)
    rowf = row.astype(F32)
    h = h_ref[...]
    c_parts = _split_bf16(comb, 3)
    for g in range(N_GROUPS):
        sel = jnp.logical_and(rank_t[g:g + 1, :] == rowf, oh_t[g:g + 1, :] > 0.5)
        perm = jnp.where(sel, 1.0, 0.0).astype(BF16)
        gs = slice(g * TOK_TILE, (g + 1) * TOK_TILE)
        ch_ref[slot, gs, :] = jnp.dot(perm, h, preferred_element_type=F32).astype(BF16)
        cc_ref[slot, gs, :] = (jnp.dot(perm, c_parts[0], preferred_element_type=F32)
                               + jnp.dot(perm, c_parts[1], preferred_element_type=F32)
                               + jnp.dot(perm, c_parts[2], preferred_element_type=F32))

    def copies(tile, sl, g, s):
        r0 = pl.multiple_of(off_ref[tile * N_GROUPS + g], ROW_ALIGN) + s * SUB_ROWS
        src = pl.ds(g * TOK_TILE + s * SUB_ROWS, SUB_ROWS)
        return (pltpu.make_async_copy(ch_ref.at[sl, src, :], hs_ref.at[pl.ds(r0, SUB_ROWS), :], sem_h.at[sl, g, s]),
                pltpu.make_async_copy(cc_ref.at[sl, src, :], cs_ref.at[pl.ds(r0, SUB_ROWS), :], sem_c.at[sl, g, s]))

    def for_each_copy(tile, sl, enabled, fn):
        for g in range(N_GROUPS):
            for s in range(N_SUB):
                @pl.when(jnp.logical_and(enabled, _sub_needed(padded_ref, tile, g, s)))
                def _():
                    for cp in copies(tile, sl, g, s):
                        fn(cp)

    prev = jnp.maximum(i - 1, 0)
    for_each_copy(prev, 1 - slot, i > 0, lambda cp: cp.wait())
    for_each_copy(i, slot, True, lambda cp: cp.start())

    @pl.when(i == n_steps - 1)
    def _():
        for_each_copy(i, slot, True, lambda cp: cp.wait())
        zh_ref[...] = jnp.zeros_like(zh_ref)
        zc_ref[...] = jnp.zeros_like(zc_ref)
        zero_copies = []
        for g in range(N_GROUPS):
            for z in range(ZERO_ROWS // TOK_TILE):
                r0 = pl.multiple_of(end_ref[g], ROW_ALIGN) + z * TOK_TILE
                zero_copies.append(pltpu.make_async_copy(
                    zh_ref, hs_ref.at[pl.ds(r0, TOK_TILE), :], sem_z.at[0, g, z]))
                zero_copies.append(pltpu.make_async_copy(
                    zc_ref, cs_ref.at[pl.ds(r0, TOK_TILE), :], sem_z.at[1, g, z]))
        for cp in zero_copies:
            cp.start()
        for cp in zero_copies:
            cp.wait()


def _dispatch(h_bf16, comb, off, padded, end):
    n_z = ZERO_ROWS // TOK_TILE
    grid_spec = pltpu.PrefetchScalarGridSpec(
        num_scalar_prefetch=3,
        grid=(N_TOK_TILES,),
        in_specs=[
            pl.BlockSpec((TOK_TILE, D), lambda i, *_: (i, 0)),
            pl.BlockSpec((TOK_TILE, ROUTER_LANES), lambda i, *_: (i, 0)),
        ],
        out_specs=[pl.BlockSpec(memory_space=pl.ANY), pl.BlockSpec(memory_space=pl.ANY)],
        scratch_shapes=[
            pltpu.VMEM((2, N_GROUPS * TOK_TILE, D), BF16),
            pltpu.VMEM((2, N_GROUPS * TOK_TILE, ROUTER_LANES), F32),
            pltpu.VMEM((TOK_TILE, D), BF16),
            pltpu.VMEM((TOK_TILE, ROUTER_LANES), F32),
            pltpu.SemaphoreType.DMA((2, N_GROUPS, N_SUB)),
            pltpu.SemaphoreType.DMA((2, N_GROUPS, N_SUB)),
            pltpu.SemaphoreType.DMA((2, N_GROUPS, n_z)),
        ],
    )
    return pl.pallas_call(
        _dispatch_kernel,
        grid_spec=grid_spec,
        out_shape=[jax.ShapeDtypeStruct((ROW_CAP, D), BF16),
                   jax.ShapeDtypeStruct((ROW_CAP, ROUTER_LANES), F32)],
        compiler_params=_params("arbitrary"),
        name="moe_dispatch",
    )(off, padded, end, h_bf16, comb)


def _moe_kernel(blk_ref, grp_ref, nsub_ref, h_ref, c_ref, w1_ref, w3_ref, w2_ref, o_ref, acc_ref):
    k = pl.program_id(0)
    c = pl.program_id(1)
    ns = nsub_ref[k]
    n_sub_max = MOE_TILE // MOE_SUB
    col_chunk = 512

    @pl.when(ns > 0)
    def _():
        w1 = w1_ref[...].astype(BF16)
        w3 = w3_ref[...].astype(BF16)
        w2 = w2_ref[...].astype(BF16)
        e_lane = EXPERT_LANE0 + grp_ref[k] * EXP_PER_GROUP + c * FF_CHUNK // D_EXPERT
        for m in range(1, n_sub_max + 1):
            @pl.when(ns == m)
            def _():
                rows = m * MOE_SUB
                h = h_ref[0:rows, :]
                a = jnp.dot(h, w1, preferred_element_type=F32)
                b = jnp.dot(h, w3, preferred_element_type=F32)
                comb = c_ref[0:rows, :]
                lane = lax.broadcasted_iota(jnp.int32, comb.shape, 1)
                cw = jnp.sum(jnp.where(lane == e_lane, comb, 0.0), axis=-1, keepdims=True)
                hid = (_silu(a) * b * cw).astype(BF16)
                for j in range(D // col_chunk):
                    cs = slice(j * col_chunk, (j + 1) * col_chunk)
                    y = jnp.dot(hid, w2[:, cs], preferred_element_type=F32)

                    @pl.when(c == 0)
                    def _():
                        acc_ref[0:rows, cs] = y

                    @pl.when(c > 0)
                    def _():
                        acc_ref[0:rows, cs] += y

                @pl.when(c == N_FF_CHUNKS - 1)
                def _():
                    o_ref[0:rows, :] = acc_ref[0:rows, :].astype(BF16)
                    if rows < MOE_TILE:
                        o_ref[rows:, :] = jnp.zeros((MOE_TILE - rows, D), BF16)


def _moe_experts(hs, cs, w1, w3, w2, layer, blk, grp, nsub):
    per_e = D_EXPERT // FF_CHUNK

    def w_idx(k, c, blk_ref, grp_ref, nsub_ref):
        cc = jnp.where(nsub_ref[k] > 0, c, N_FF_CHUNKS - 1)
        return grp_ref[k] * EXP_PER_GROUP + cc // per_e, cc % per_e

    def w13_map(k, c, *refs):
        e, part = w_idx(k, c, *refs)
        return (layer, e, 0, part)

    def w2_map(k, c, *refs):
        e, part = w_idx(k, c, *refs)
        return (layer, e, part, 0)

    row_map = lambda k, c, blk_ref, grp_ref, nsub_ref: (blk_ref[k], 0)
    grid_spec = pltpu.PrefetchScalarGridSpec(
        num_scalar_prefetch=3,
        grid=(MOE_TILES_MAX, N_FF_CHUNKS),
        in_specs=[
            pl.BlockSpec((MOE_TILE, D), row_map),
            pl.BlockSpec((MOE_TILE, ROUTER_LANES), row_map),
            pl.BlockSpec((None, None, D, FF_CHUNK), w13_map),
            pl.BlockSpec((None, None, D, FF_CHUNK), w13_map),
            pl.BlockSpec((None, None, FF_CHUNK, D), w2_map),
        ],
        out_specs=pl.BlockSpec((MOE_TILE, D), row_map),
        scratch_shapes=[pltpu.VMEM((MOE_TILE, D), F32)],
    )
    return pl.pallas_call(
        _moe_kernel,
        grid_spec=grid_spec,
        out_shape=jax.ShapeDtypeStruct((ROW_CAP, D), BF16),
        compiler_params=_params("arbitrary", "arbitrary"),
        name="moe_experts",
    )(blk, grp, nsub, hs, cs, w1, w3, w2)


def _combine_kernel(off_ref, padded_ref, x_ref, gate_ref, comb_ref, ys_ref, o_ref, yw_ref, sem):
    i = pl.program_id(0)
    n_steps = pl.num_programs(0)
    slot = i % 2

    def copy(tile, sl, g, s):
        r0 = pl.multiple_of(off_ref[tile * N_GROUPS + g], ROW_ALIGN) + s * SUB_ROWS
        dst = pl.ds(g * TOK_TILE + s * SUB_ROWS, SUB_ROWS)
        return pltpu.make_async_copy(ys_ref.at[pl.ds(r0, SUB_ROWS), :], yw_ref.at[sl, dst, :], sem.at[sl, g, s])

    def fetch(tile, sl, enabled):
        for g in range(N_GROUPS):
            for s in range(N_SUB):
                needed = _sub_needed(padded_ref, tile, g, s)

                @pl.when(jnp.logical_and(enabled, needed))
                def _():
                    copy(tile, sl, g, s).start()

                @pl.when(jnp.logical_and(enabled, jnp.logical_not(needed)))
                def _():
                    yw_ref[sl, pl.ds(g * TOK_TILE + s * SUB_ROWS, SUB_ROWS), :] = jnp.zeros((SUB_ROWS, D), BF16)

    fetch(i, slot, i == 0)
    nxt = jnp.minimum(i + 1, n_steps - 1)
    fetch(nxt, 1 - slot, i + 1 < n_steps)
    for g in range(N_GROUPS):
        for s in range(N_SUB):
            @pl.when(_sub_needed(padded_ref, i, g, s))
            def _():
                copy(i, slot, g, s).wait()

    comb = comb_ref[...]
    lane = lax.broadcasted_iota(jnp.int32, comb.shape, 1)
    onehot = jnp.where(lane < N_GROUPS, comb, 0.0)
    row = lax.broadcasted_iota(jnp.int32, (TOK_TILE, TOK_TILE), 0)
    col = lax.broadcasted_iota(jnp.int32, (TOK_TILE, TOK_TILE), 1)
    earlier = jnp.where(col < row, 1.0, 0.0).astype(BF16)
    rank = jnp.dot(earlier, onehot.astype(BF16), preferred_element_type=F32)
    rank_own = jnp.sum(rank * onehot, axis=-1, keepdims=True)
    group = jnp.sum(lane.astype(F32) * onehot, axis=-1, keepdims=True)
    pos = (group * TOK_TILE + rank_own).astype(jnp.int32)
    wcol = lax.broadcasted_iota(jnp.int32, (TOK_TILE, N_GROUPS * TOK_TILE), 1)
    perm = jnp.where(wcol == pos, 1.0, 0.0).astype(BF16)
    y = jnp.dot(perm, yw_ref[slot], preferred_element_type=F32)
    o_ref[...] = x_ref[...] + gate_ref[...] * y


def _combine(x, mod, layer, comb, ys, off, padded):
    grid_spec = pltpu.PrefetchScalarGridSpec(
        num_scalar_prefetch=2,
        grid=(N_TOK_TILES,),
        in_specs=[
            pl.BlockSpec((TOK_TILE, D), lambda i, *_: (i, 0)),
            _mod_spec(layer, 5, TOK_TILE, row_axis=0),
            pl.BlockSpec((TOK_TILE, ROUTER_LANES), lambda i, *_: (i, 0)),
            pl.BlockSpec(memory_space=pl.ANY),
        ],
        out_specs=pl.BlockSpec((TOK_TILE, D), lambda i, *_: (i, 0)),
        scratch_shapes=[
            pltpu.VMEM((2, N_GROUPS * TOK_TILE, D), BF16),
            pltpu.SemaphoreType.DMA((2, N_GROUPS, N_SUB)),
        ],
    )
    return pl.pallas_call(
        _combine_kernel,
        grid_spec=grid_spec,
        out_shape=jax.ShapeDtypeStruct((N_TOK, D), F32),
        compiler_params=_params("arbitrary"),
        name="moe_combine",
    )(off, padded, x, mod, comb, ys)


def _moe_sublayer(x, mod, layer, norm_g, rgw, rgb, rew, reb, w1, w3, w2):
    rw_hi, rw_lo, rb = _router_weights(rgw, rgb, rew, reb)
    h2, comb, cnt = _router(x, mod, layer, norm_g, rw_hi, rw_lo, rb)
    counts = cnt[:, 0, :N_GROUPS].astype(jnp.int32)
    off, padded, end, blk, grp, nsub = _moe_plan(counts)
    hs, cs = _dispatch(h2, comb, off, padded, end)
    ys = _moe_experts(hs, cs, w1, w3, w2, layer, blk, grp, nsub)
    return _combine(x, mod, layer, comb, ys, off, padded)


def kernel(x_prompt, x_sample, cache_k, cache_v, c, c_ctx, ada_w, ada_b, norm_mix_g, norm_ffn_g,
           mix_in_w, pool_w, pool_scale, conv_w, conv_b, conv_ln_g, conv_ln_b, mix_out_w,
           qkv_w, q_norm_g, k_norm_g, rpb, attn_out_w, router_g_w, router_g_b, router_e_w,
           router_e_b, exp_w1, exp_w3, exp_w2):
    x = jnp.concatenate([x_prompt.reshape(N_CTX, D), x_sample.reshape(N_LAT, D)], axis=0)
    cvec8 = jnp.concatenate([c_ctx[None, :], c, jnp.zeros((8 - 1 - N_LAT_SEQ, D), F32)], axis=0)
    mod = _adaln_all(cvec8, ada_w, ada_b)
    n_attn = DEPTH // 2
    cache_k2 = cache_k.reshape(N_LAT_SEQ, n_attn, CTX_SEQ, D)
    cache_v2 = cache_v.reshape(N_LAT_SEQ, n_attn, CTX_SEQ, D)

    caches = None
    for l in range(DEPTH):
        e = l // 2
        if l % 2 == 0:
            u = _mm_prologue(x, mod, l, norm_mix_g[l], mix_in_w[e].astype(BF16))
            cat = _seq_mixer(u, pool_w[e].astype(BF16), pool_scale[e], conv_w[e], conv_b[e],
                             conv_ln_g[e], conv_ln_b[e])
            x = _mm_residual(cat, mix_out_w[e].astype(BF16), x, mod, l)
        else:
            q, k, v, new_k, new_v = _qkv_proj(x, mod, l, e, norm_mix_g[l], qkv_w[e].astype(BF16),
                                              q_norm_g[e], k_norm_g[e], caches)
            caches = (new_k, new_v)
            o = _ctx_attention(q, k, v)
            o = _na_attention(q, k, v, cache_k2[:, e], cache_v2[:, e], _na_bias_table(rpb[e]), o)
            x = _mm_residual(o, attn_out_w[e].astype(BF16), x, mod, l)
        x = _moe_sublayer(x, mod, l, norm_ffn_g[l], router_g_w[l], router_g_b[l], router_e_w[l],
                          router_e_b[l], exp_w1, exp_w3, exp_w2)

    y_prompt = x[:N_CTX].reshape(N_CTX_SEQ, CTX_SEQ, D)
    y_sample = x[N_CTX:].reshape(N_LAT_SEQ, LAT_SEQ, D)
    cache_dims = (N_CTX_SEQ, n_attn, CTX_SEQ, N_HEADS, HEAD_DIM)
    return (y_prompt, y_sample, caches[0].reshape(cache_dims), caches[1].reshape(cache_dims))
```

```python
import functools

import numpy as np
import jax
import jax.numpy as jnp
from jax import lax
from jax.experimental import pallas as pl
from jax.experimental.pallas import tpu as pltpu

D = 2048
N_CTX_SEQ = 16
CTX_SEQ = 256
N_LAT_SEQ = 2
LAT_SEQ = 1024
N_CTX = N_CTX_SEQ * CTX_SEQ
N_LAT = N_LAT_SEQ * LAT_SEQ
N_TOK = N_CTX + N_LAT
DEPTH = 4
GRID_W = 64
GRID_H = LAT_SEQ // GRID_W
N_HEADS = 16
HEAD_DIM = 128
ATTN_SCALE = HEAD_DIM ** -0.5
D_POOL = 1024
D_CONV = 1024
POOL_WINDOWS = (2, 4, 8, 16)
POOL_GC = 256
CONV_W = 31
NA_KH = 8
NA_KW = 16
N_GROUPS = 4
EXP_PER_GROUP = 4
N_EXPERTS = 16
D_EXPERT = 512
EPS = 1e-6
NEG = -1e30

SEQ_TILE = 256
HALO = 16
ROUTER_LANES = 128
EXPERT_LANE0 = N_GROUPS
POS_LANE = EXPERT_LANE0 + N_EXPERTS
LOW_LANE0 = 32
VMEM_LIMIT = 56 * 1024 * 1024

TOK_TILE = 256
N_TOK_TILES = N_TOK // TOK_TILE
ROW_ALIGN = 16
RUN_BITS = (256, 128, 64, 32, 16)
TILE_ROWS = TOK_TILE + N_GROUPS * ROW_ALIGN
MOE_TILE = 1024
MOE_SUB = 256
FF_CHUNK = 256
OUT_CHUNK = 512
N_FF_CHUNKS = EXP_PER_GROUP * D_EXPERT // FF_CHUNK
N_OUT_CHUNKS = D // OUT_CHUNK
ZERO_ROWS = MOE_SUB
MAX_PADDED = N_TOK + N_TOK_TILES * N_GROUPS * (ROW_ALIGN - 1)
MOE_TILES_MAX = (MAX_PADDED + N_GROUPS * (MOE_TILE - 1)) // MOE_TILE
ROW_CAP = -(-(MAX_PADDED + N_GROUPS * (ZERO_ROWS + MOE_TILE - 1)) // MOE_TILE) * MOE_TILE

F32 = jnp.float32
BF16 = jnp.bfloat16


def _cond_of_row(row):
    return jnp.where(row < N_CTX, 0, 1 + (row - N_CTX) // LAT_SEQ)


def _mod_spec(layer, which, tm, tn=D, row_axis=1, col_axis=None):
    def index_map(*ids):
        cond = _cond_of_row(ids[row_axis] * tm)
        col = 0 if col_axis is None else ids[col_axis]
        return (layer, cond, which, 0, col)
    return pl.BlockSpec((None, None, None, 1, tn), index_map)


def _silu(x):
    return x / (1.0 + jnp.exp(-x))


def _modnorm(x, g, sc, sh):
    ms = jnp.mean(x * x, axis=-1, keepdims=True)
    y = x * lax.rsqrt(ms + EPS) * g
    return y * (1.0 + sc) + sh


def _params(*sem):
    return pltpu.CompilerParams(dimension_semantics=sem, vmem_limit_bytes=VMEM_LIMIT)


def _ada_kernel(c_ref, w_ref, b_ref, o_ref):
    s = _silu(c_ref[...]).astype(BF16)
    o_ref[...] = jnp.dot(s, w_ref[...].astype(BF16), preferred_element_type=F32) + b_ref[...]


def _adaln_all(cvec8, ada_w, ada_b):
    tn = 1024
    n_out = 6 * D
    out = pl.pallas_call(
        _ada_kernel,
        grid=(DEPTH, n_out // tn),
        in_specs=[
            pl.BlockSpec((8, D), lambda l, j: (0, 0)),
            pl.BlockSpec((None, D, tn), lambda l, j: (l, 0, j)),
            pl.BlockSpec((None, 1, tn), lambda l, j: (l, 0, j)),
        ],
        out_specs=pl.BlockSpec((None, 8, tn), lambda l, j: (l, 0, j)),
        out_shape=jax.ShapeDtypeStruct((DEPTH, 8, n_out), F32),
        compiler_params=_params("arbitrary", "arbitrary"),
        name="adaln",
    )(cvec8, ada_w, ada_b.reshape(DEPTH, 1, n_out))
    return out[:, :3].reshape(DEPTH, 3, 6, 1, D)


def _token_specs(x, tm, tn, idx):
    if not isinstance(x, tuple):
        return [pl.BlockSpec((tm, tn), lambda j, i: idx(i, j))], [x]
    n_ctx_tiles = N_CTX // tm
    ctx_map = lambda j, i: idx(jnp.minimum(i, n_ctx_tiles - 1), j)
    lat_map = lambda j, i: idx(jnp.maximum(i - n_ctx_tiles, 0), j)
    return [pl.BlockSpec((tm, tn), ctx_map), pl.BlockSpec((tm, tn), lat_map)], list(x)


def _token_rows(refs, i, tm):
    if len(refs) == 1:
        return refs[0][...]
    return jnp.where(i < N_CTX // tm, refs[0][...], refs[1][...])


def _cast_weights_once(i, pairs):
    @pl.when(i == 0)
    def _():
        for w_ref, wb_ref in pairs:
            wb_ref[...] = w_ref[...].astype(BF16)


def _mm_pro_kernel(*refs, n_x, tm):
    x_refs = refs[:n_x]
    g_ref, sc_ref, sh_ref, w_ref, o_ref, wb_ref = refs[n_x:]
    i = pl.program_id(1)
    _cast_weights_once(i, [(w_ref, wb_ref)])
    h = _modnorm(_token_rows(x_refs, i, tm), g_ref[...], sc_ref[...], sh_ref[...]).astype(BF16)
    o_ref[...] = jnp.dot(h, wb_ref[...], preferred_element_type=F32)


def _mm_prologue(x, mod, layer, norm_g, w, w_idx, tm=512, tn=1024):
    f = w.shape[2]
    x_specs, x_args = _token_specs(x, tm, D, lambda i, j: (i, 0))
    return pl.pallas_call(
        functools.partial(_mm_pro_kernel, n_x=len(x_args), tm=tm),
        grid=(f // tn, N_TOK // tm),
        in_specs=x_specs + [
            pl.BlockSpec((1, D), lambda j, i: (0, 0)),
            _mod_spec(layer, 1, tm),
            _mod_spec(layer, 0, tm),
            pl.BlockSpec((None, D, tn), lambda j, i: (w_idx, 0, j)),
        ],
        out_specs=pl.BlockSpec((tm, tn), lambda j, i: (i, j)),
        out_shape=jax.ShapeDtypeStruct((N_TOK, f), F32),
        scratch_shapes=[pltpu.VMEM((D, tn), BF16)],
        compiler_params=_params("arbitrary", "arbitrary"),
        name="mm_prologue",
    )(*x_args, norm_g.reshape(1, D), mod, mod, w)


def _qkv_kernel(x_ref, g_ref, sc_ref, sh_ref, wq_ref, wk_ref, wv_ref, qg_ref, kg_ref, *rest,
                tn, n_ctx_tiles, seqs_per_tile):
    q_ref, k_ref, v_ref, ck_ref, cv_ref, wqb_ref, wkb_ref, wvb_ref = rest[-8:]
    i = pl.program_id(1)
    _cast_weights_once(i, [(wq_ref, wqb_ref), (wk_ref, wkb_ref), (wv_ref, wvb_ref)])
    h = _modnorm(x_ref[...], g_ref[...], sc_ref[...], sh_ref[...]).astype(BF16)
    q = jnp.dot(h, wqb_ref[...], preferred_element_type=F32)
    k = jnp.dot(h, wkb_ref[...], preferred_element_type=F32)
    v = jnp.dot(h, wvb_ref[...], preferred_element_type=F32)
    is_ctx = i < n_ctx_tiles
    for hh in range(tn // HEAD_DIM):
        ls = slice(hh * HEAD_DIM, (hh + 1) * HEAD_DIM)
        qh = q[:, ls]
        kh = k[:, ls]
        qn = qh * lax.rsqrt(jnp.mean(qh * qh, axis=-1, keepdims=True) + EPS) * qg_ref[:, ls]
        kn = kh * lax.rsqrt(jnp.mean(kh * kh, axis=-1, keepdims=True) + EPS) * kg_ref[:, ls]
        q_ref[:, ls] = qn.astype(BF16)
        k_ref[:, ls] = kn.astype(BF16)

        @pl.when(is_ctx)
        def _():
            ck_ref[:, :, ls] = kn.reshape(seqs_per_tile, CTX_SEQ, HEAD_DIM)

    v_ref[...] = v.astype(BF16)

    @pl.when(is_ctx)
    def _():
        cv_ref[...] = v.reshape(seqs_per_tile, CTX_SEQ, tn)


def _qkv_proj(x, mod, layer, w_idx, norm_g, w, q_gain, k_gain, caches, tm=512, tn=512):
    n = x.shape[0]
    n_attn = DEPTH // 2
    ncol = D // tn
    seqs_per_tile = tm // CTX_SEQ
    n_ctx_tiles = N_CTX // tm
    cache_shape = jax.ShapeDtypeStruct((N_CTX_SEQ, n_attn, CTX_SEQ, D), F32)
    cache_spec = pl.BlockSpec(
        (seqs_per_tile, None, CTX_SEQ, tn),
        lambda j, i: (jnp.minimum(i, n_ctx_tiles - 1), w_idx, 0, j))
    in_specs = [
        pl.BlockSpec((tm, D), lambda j, i: (i, 0)),
        pl.BlockSpec((1, D), lambda j, i: (0, 0)),
        _mod_spec(layer, 1, tm),
        _mod_spec(layer, 0, tm),
        pl.BlockSpec((None, D, tn), lambda j, i: (w_idx, 0, j)),
        pl.BlockSpec((None, D, tn), lambda j, i: (w_idx, 0, ncol + j)),
        pl.BlockSpec((None, D, tn), lambda j, i: (w_idx, 0, 2 * ncol + j)),
        pl.BlockSpec((1, tn), lambda j, i: (0, j)),
        pl.BlockSpec((1, tn), lambda j, i: (0, j)),
    ]
    args = [x, norm_g.reshape(1, D), mod, mod, w, w, w,
            jnp.tile(q_gain, N_HEADS).reshape(1, D), jnp.tile(k_gain, N_HEADS).reshape(1, D)]
    aliases = {}
    if caches is not None:
        in_specs += [pl.BlockSpec(memory_space=pl.ANY), pl.BlockSpec(memory_space=pl.ANY)]
        aliases = {len(args): 3, len(args) + 1: 4}
        args += list(caches)
    act_spec = pl.BlockSpec((tm, tn), lambda j, i: (i, j))
    act_shape = jax.ShapeDtypeStruct((n, D), BF16)
    return pl.pallas_call(
        functools.partial(_qkv_kernel, tn=tn, n_ctx_tiles=n_ctx_tiles, seqs_per_tile=seqs_per_tile),
        grid=(ncol, n // tm),
        in_specs=in_specs,
        out_specs=[act_spec, act_spec, act_spec, cache_spec, cache_spec],
        out_shape=[act_shape, act_shape, act_shape, cache_shape, cache_shape],
        input_output_aliases=aliases,
        scratch_shapes=[pltpu.VMEM((D, tn), BF16)] * 3,
        compiler_params=_params("arbitrary", "arbitrary"),
        name="qkv_proj",
    )(*args)


def _mm_res_kernel(a_ref, w_ref, gate_ref, *refs, tm):
    x_refs, (o_ref, wb_ref) = refs[:-2], refs[-2:]
    i = pl.program_id(1)
    _cast_weights_once(i, [(w_ref, wb_ref)])
    y = jnp.dot(a_ref[...], wb_ref[...], preferred_element_type=F32)
    o_ref[...] = _token_rows(x_refs, i, tm) + gate_ref[...] * y


def _mm_residual(a_bf16, w, w_idx, x, mod, layer, tm=512, tn=1024):
    k = a_bf16.shape[1]
    x_specs, x_args = _token_specs(x, tm, tn, lambda i, j: (i, j))
    return pl.pallas_call(
        functools.partial(_mm_res_kernel, tm=tm),
        grid=(D // tn, N_TOK // tm),
        in_specs=[
            pl.BlockSpec((tm, k), lambda j, i: (i, 0)),
            pl.BlockSpec((None, k, tn), lambda j, i: (w_idx, 0, j)),
            _mod_spec(layer, 2, tm, tn=tn, col_axis=0),
        ] + x_specs,
        out_specs=pl.BlockSpec((tm, tn), lambda j, i: (i, j)),
        out_shape=jax.ShapeDtypeStruct((N_TOK, D), F32),
        scratch_shapes=[pltpu.VMEM((k, tn), BF16)],
        compiler_params=_params("arbitrary", "arbitrary"),
        name="mm_residual",
    )(a_bf16, w, mod, *x_args)


def _seq_kernel(u_ref, top_ref, bot_ref, pw_ref, ps_ref, cw_ref, cb_ref, lg_ref, lb_ref,
                o_ref, zpad_ref, conv_ref):
    i = pl.program_id(0)
    n_ctx_tiles = N_CTX // SEQ_TILE
    tiles_per_lat = LAT_SEQ // SEQ_TILE
    is_lat = i >= n_ctx_tiles
    chunk = jnp.where(is_lat, (i - n_ctx_tiles) % tiles_per_lat, 0)
    top_ok = jnp.logical_and(is_lat, chunk > 0)
    bot_ok = jnp.logical_and(is_lat, chunk < tiles_per_lat - 1)
    seq_len = jnp.where(is_lat, LAT_SEQ, CTX_SEQ)
    t = chunk * SEQ_TILE + lax.broadcasted_iota(jnp.int32, (SEQ_TILE, 1), 0)
    rows = SEQ_TILE + 2 * HALO

    for g, win in enumerate(POOL_WINDOWS):
        ls = slice(g * POOL_GC, (g + 1) * POOL_GC)
        mid = u_ref[:, ls]
        top = jnp.where(top_ok, top_ref[:, ls], 0.0)
        bot = jnp.where(bot_ok, bot_ref[:, ls], 0.0)
        up = jnp.concatenate([top, mid, bot], axis=0)
        s = pltpu.roll(up, 1, 0) + up
        if win >= 4:
            s = pltpu.roll(s, 1, 0) + pltpu.roll(s, rows - 1, 0)
        if win >= 8:
            s = pltpu.roll(s, 2, 0) + pltpu.roll(s, rows - 2, 0)
        if win >= 16:
            s = pltpu.roll(s, 4, 0) + pltpu.roll(s, rows - 4, 0)
        s = s[HALO:HALO + SEQ_TILE]
        lo = jnp.maximum(t - win // 2, 0)
        hi = jnp.minimum(t + (win - win // 2) - 1, seq_len - 1)
        cnt = (hi - lo + 1).astype(F32)
        p = (s / cnt - mid).astype(BF16)
        y = jnp.dot(p, pw_ref[g], preferred_element_type=F32) * ps_ref[:, ls]
        o_ref[:, ls] = y.astype(BF16)

    def glu(ref):
        return ref[:, D_POOL:D_POOL + D_CONV] * (1.0 / (1.0 + jnp.exp(-ref[:, D_POOL + D_CONV:])))

    zpad_ref[0:HALO, :] = jnp.where(top_ok, glu(top_ref), 0.0)
    zpad_ref[HALO:HALO + SEQ_TILE, :] = glu(u_ref)
    zpad_ref[HALO + SEQ_TILE:rows, :] = jnp.where(bot_ok, glu(bot_ref), 0.0)
    for lt in range(D_CONV // 128):
        ls = slice(lt * 128, (lt + 1) * 128)
        zp = zpad_ref[:, ls]
        acc = jnp.zeros((SEQ_TILE, 128), F32)
        for b in range(8):
            sb = zp if b == 0 else pltpu.roll(zp, rows - b, 0)
            for a in range(4):
                off = 8 * a + b
                if 1 <= off <= CONV_W:
                    acc = acc + cw_ref[off - 1:off, ls] * sb[8 * a:8 * a + SEQ_TILE]
        conv_ref[:, ls] = acc + cb_ref[:, ls]
    zc = conv_ref[...]
    mu = jnp.mean(zc, axis=-1, keepdims=True)
    d = zc - mu
    var = jnp.mean(d * d, axis=-1, keepdims=True)
    zn = d * lax.rsqrt(var + EPS) * lg_ref[...] + lb_ref[...]
    o_ref[:, D_POOL:] = _silu(zn).astype(BF16)


def _seq_mixer(u, pool_w_bf16, pool_scale, conv_w, conv_b, ln_g, ln_b):
    n = u.shape[0]
    f_in = u.shape[1]
    n_tiles = n // SEQ_TILE
    hb = SEQ_TILE // HALO
    n_hblocks = n // HALO
    return pl.pallas_call(
        _seq_kernel,
        grid=(n_tiles,),
        in_specs=[
            pl.BlockSpec((SEQ_TILE, f_in), lambda i: (i, 0)),
            pl.BlockSpec((HALO, f_in), lambda i: (jnp.maximum(i * hb - 1, 0), 0)),
            pl.BlockSpec((HALO, f_in), lambda i: (jnp.minimum((i + 1) * hb, n_hblocks - 1), 0)),
            pl.BlockSpec((len(POOL_WINDOWS), POOL_GC, POOL_GC), lambda i: (0, 0, 0)),
            pl.BlockSpec((1, D_POOL), lambda i: (0, 0)),
            pl.BlockSpec((CONV_W, D_CONV), lambda i: (0, 0)),
            pl.BlockSpec((1, D_CONV), lambda i: (0, 0)),
            pl.BlockSpec((1, D_CONV), lambda i: (0, 0)),
            pl.BlockSpec((1, D_CONV), lambda i: (0, 0)),
        ],
        out_specs=pl.BlockSpec((SEQ_TILE, D_POOL + D_CONV), lambda i: (i, 0)),
        out_shape=jax.ShapeDtypeStruct((n, D_POOL + D_CONV), BF16),
        scratch_shapes=[
            pltpu.VMEM((SEQ_TILE + 2 * HALO, D_CONV), F32),
            pltpu.VMEM((SEQ_TILE, D_CONV), F32),
        ],
        compiler_params=_params("arbitrary"),
        name="seq_mixer",
    )(u, u, u, pool_w_bf16, pool_scale.reshape(1, D_POOL), conv_w, conv_b.reshape(1, D_CONV),
      ln_g.reshape(1, D_CONV), ln_b.reshape(1, D_CONV))


_NT = (((1,), (1,)), ((), ()))


def _ctx_attn_kernel(q_ref, k_ref, v_ref, o_ref, s_ref, p_ref):
    for h in range(N_HEADS):
        ls = slice(h * HEAD_DIM, (h + 1) * HEAD_DIM)
        s_ref[h] = lax.dot_general(q_ref[:, ls], k_ref[:, ls], _NT, preferred_element_type=F32) * ATTN_SCALE
    for h in range(N_HEADS):
        s = s_ref[h]
        e = jnp.exp(s - jnp.max(s, axis=-1, keepdims=True))
        p_ref[h] = (e * (1.0 / jnp.sum(e, axis=-1, keepdims=True))).astype(BF16)
    for h in range(N_HEADS):
        ls = slice(h * HEAD_DIM, (h + 1) * HEAD_DIM)
        o_ref[:, ls] = jnp.dot(p_ref[h], v_ref[:, ls], preferred_element_type=F32).astype(BF16)


def _ctx_attention(q, k, v):
    n = q.shape[0]
    spec = pl.BlockSpec((CTX_SEQ, D), lambda b: (b, 0))
    return pl.pallas_call(
        _ctx_attn_kernel,
        grid=(N_CTX_SEQ,),
        in_specs=[spec, spec, spec],
        out_specs=spec,
        out_shape=jax.ShapeDtypeStruct((n, D), BF16),
        scratch_shapes=[pltpu.VMEM((N_HEADS, CTX_SEQ, CTX_SEQ), F32),
                        pltpu.VMEM((N_HEADS, CTX_SEQ, CTX_SEQ), BF16)],
        compiler_params=_params("arbitrary"),
        name="ctx_attention",
    )(q, k, v)


def _bias_table_kernel(r_ref, e_ref, o_ref):
    x = r_ref[...]
    hi = x.astype(BF16)
    r1 = x - hi.astype(F32)
    mid = r1.astype(BF16)
    lo = (r1 - mid.astype(F32)).astype(BF16)
    e = e_ref[...]
    o_ref[...] = (jnp.dot(hi, e, preferred_element_type=F32)
                  + jnp.dot(mid, e, preferred_element_type=F32)
                  + jnp.dot(lo, e, preferred_element_type=F32))


def _na_static_tables():
    cq = np.arange(GRID_W)[:, None]
    kc = np.arange(GRID_W)[None, :]
    dcol = kc - cq + NA_KW - 1
    n_dc = 2 * NA_KW - 1
    onehot = np.zeros((128, GRID_W, GRID_W), np.float32)
    for dc in range(n_dc):
        onehot[dc] = (dcol == dc)
    win0 = np.clip(cq - NA_KW // 2, 0, GRID_W - NA_KW)
    mask = ((kc >= win0) & (kc < win0 + NA_KW)).astype(np.float32)
    return onehot.reshape(128, GRID_W * GRID_W), np.tile(mask, (1, NA_KH))


def _na_bias_table(rpb_e):
    n_dr = 2 * NA_KH - 1
    n_dc = 2 * NA_KW - 1
    onehot, _ = _na_static_tables()
    rows = N_HEADS * n_dr
    r2 = jnp.pad(rpb_e.reshape(rows, n_dc), ((0, 0), (0, 128 - n_dc)))
    tz = pl.pallas_call(
        _bias_table_kernel,
        grid=(1,),
        in_specs=[pl.BlockSpec((rows, 128), lambda i: (0, 0)),
                  pl.BlockSpec((128, GRID_W * GRID_W), lambda i: (0, 0))],
        out_specs=pl.BlockSpec((rows, GRID_W * GRID_W), lambda i: (0, 0)),
        out_shape=jax.ShapeDtypeStruct((rows, GRID_W * GRID_W), F32),
        compiler_params=pltpu.CompilerParams(vmem_limit_bytes=VMEM_LIMIT),
        name="na_bias_table",
    )(r2, jnp.asarray(onehot, BF16))
    tz = tz.reshape(N_HEADS, n_dr, GRID_W, GRID_W)
    return jnp.concatenate([tz[:, :-1], tz[:, 1:]], axis=-1)


def _na_key_row0(r):
    return min(max(r - NA_KH // 2, 0), GRID_H - NA_KH)


def _na_segments():
    segs = []
    for r in range(GRID_H):
        kr0 = _na_key_row0(r)
        if segs and segs[-1][2] == kr0:
            segs[-1][1] = r + 1
        else:
            segs.append([r, r + 1, kr0])
    return segs


def _na_kernel(q_ref, k_ref, v_ref, kc_ref, vc_ref, tz_ref, mask_ref, o_in_ref, o_ref,
               s_ref, p_ref, acc_ref):
    del o_in_ref
    band = NA_KH * GRID_W
    n_ctx = CTX_SEQ
    kc = kc_ref[...].astype(BF16)
    vc = vc_ref[...].astype(BF16)
    for r0, r1, kr0 in _na_segments():
        rs = slice(r0 * GRID_W, r1 * GRID_W)
        kb = k_ref[kr0 * GRID_W:kr0 * GRID_W + band, :]
        s_ref[rs, 0:band] = lax.dot_general(q_ref[rs, :], kb, _NT, preferred_element_type=F32) * ATTN_SCALE
    s_ref[:, band:band + n_ctx] = lax.dot_general(q_ref[...], kc, _NT, preferred_element_type=F32) * ATTN_SCALE
    mask = mask_ref[...] > 0.5
    for r in range(GRID_H):
        rs = slice(r * GRID_W, (r + 1) * GRID_W)
        kr0 = _na_key_row0(r)
        bias = jnp.concatenate(
            [tz_ref[kr0 + j - r + NA_KH - 1] for j in range(0, NA_KH, 2)], axis=-1)
        s_loc = jnp.where(mask, s_ref[rs, 0:band] + bias, NEG)
        s_ctx = s_ref[rs, band:band + n_ctx]
        m = jnp.maximum(jnp.max(s_loc, axis=-1, keepdims=True), jnp.max(s_ctx, axis=-1, keepdims=True))
        e_loc = jnp.exp(s_loc - m)
        e_ctx = jnp.exp(s_ctx - m)
        inv = 1.0 / (jnp.sum(e_loc, axis=-1, keepdims=True) + jnp.sum(e_ctx, axis=-1, keepdims=True))
        p_ref[rs, 0:band] = (e_loc * inv).astype(BF16)
        p_ref[rs, band:band + n_ctx] = (e_ctx * inv).astype(BF16)
    for r0, r1, kr0 in _na_segments():
        rs = slice(r0 * GRID_W, r1 * GRID_W)
        vb = v_ref[kr0 * GRID_W:kr0 * GRID_W + band, :]
        acc_ref[rs, :] = jnp.dot(p_ref[rs, 0:band], vb, preferred_element_type=F32)
    o_ctx = jnp.dot(p_ref[:, band:band + n_ctx], vc, preferred_element_type=F32)
    o_ref[...] = (acc_ref[...] + o_ctx).astype(BF16)


def _na_attention(q, k, v, cache_k_e, cache_v_e, bias_table, o_ctx):
    _, mask = _na_static_tables()
    lat0 = N_CTX // LAT_SEQ
    n_dr2 = bias_table.shape[1]
    n_keys = NA_KH * GRID_W + CTX_SEQ
    tok_spec = pl.BlockSpec((LAT_SEQ, HEAD_DIM), lambda b, h: (lat0 + b, h))
    cache_spec = pl.BlockSpec((None, CTX_SEQ, HEAD_DIM), lambda b, h: (b, 0, h))
    return pl.pallas_call(
        _na_kernel,
        grid=(N_LAT_SEQ, N_HEADS),
        in_specs=[
            tok_spec, tok_spec, tok_spec, cache_spec, cache_spec,
            pl.BlockSpec((None, n_dr2, GRID_W, 2 * GRID_W), lambda b, h: (h, 0, 0, 0)),
            pl.BlockSpec((GRID_W, NA_KH * GRID_W), lambda b, h: (0, 0)),
            pl.BlockSpec(memory_space=pl.ANY),
        ],
        out_specs=tok_spec,
        out_shape=jax.ShapeDtypeStruct(o_ctx.shape, o_ctx.dtype),
        input_output_aliases={7: 0},
        scratch_shapes=[pltpu.VMEM((LAT_SEQ, n_keys), F32),
                        pltpu.VMEM((LAT_SEQ, n_keys), BF16),
                        pltpu.VMEM((LAT_SEQ, HEAD_DIM), F32)],
        compiler_params=_params("arbitrary", "arbitrary"),
        name="na_attention",
    )(q, k, v, cache_k_e, cache_v_e, bias_table, jnp.asarray(mask), o_ctx)


def _split_bf16(x, pieces):
    out = []
    for _ in range(pieces):
        p = x.astype(BF16)
        out.append(p)
        x = x - p.astype(F32)
    return out


def _router_kernel(x_ref, g_ref, sc_ref, sh_ref, rw_ref, rb_ref, hc_ref, cc_ref, comb_ref, cnt_ref):
    h = _modnorm(x_ref[...], g_ref[...], sc_ref[...], sh_ref[...])
    h_hi, h_lo = _split_bf16(h, 2)
    prod = jnp.dot(jnp.concatenate([h_hi, h_lo], axis=1), rw_ref[...], preferred_element_type=F32)
    logits = prod + pltpu.roll(prod, ROUTER_LANES - LOW_LANE0, 1) + rb_ref[...]
    lane = lax.broadcasted_iota(jnp.int32, logits.shape, 1)
    big = jnp.int32(1 << 20)
    neg_inf = -jnp.inf

    def first_argmax(vals, valid):
        v = jnp.where(valid, vals, neg_inf)
        m = jnp.max(v, axis=-1, keepdims=True)
        idx = jnp.min(jnp.where(jnp.logical_and(valid, v == m), lane, big), axis=-1, keepdims=True)
        return m, idx

    is_g = lane < N_GROUPS
    mg, gi = first_argmax(logits, is_g)
    pg = 1.0 / jnp.sum(jnp.where(is_g, jnp.exp(logits - mg), 0.0), axis=-1, keepdims=True)
    e_lane0 = EXPERT_LANE0 + gi * EXP_PER_GROUP
    is_e = jnp.logical_and(lane >= e_lane0, lane < e_lane0 + EXP_PER_GROUP)
    m1, i1 = first_argmax(logits, is_e)
    m2, i2 = first_argmax(logits, jnp.logical_and(is_e, lane != i1))
    e2 = jnp.exp(m2 - m1)
    den = 1.0 + e2
    w1 = (1.0 / den) * pg
    w2 = (e2 / den) * pg
    onehot_g = jnp.where(lane == gi, 1.0, 0.0)

    row = lax.broadcasted_iota(jnp.int32, (TOK_TILE, TOK_TILE), 0)
    col = lax.broadcasted_iota(jnp.int32, (TOK_TILE, TOK_TILE), 1)
    earlier = jnp.where(col < row, 1.0, 0.0).astype(BF16)
    rank = jnp.dot(earlier, onehot_g.astype(BF16), preferred_element_type=F32)
    cnt = jnp.sum(onehot_g, axis=0, keepdims=True)
    padded = jnp.floor((cnt + (ROW_ALIGN - 1)) * (1.0 / ROW_ALIGN)) * ROW_ALIGN
    padded8 = jnp.broadcast_to(padded, (8, ROUTER_LANES))
    run0 = (pltpu.roll(padded8, 1, 1) + pltpu.roll(padded8, 2, 1) + pltpu.roll(padded8, 3, 1))[0:1, :]
    pos = jnp.sum(onehot_g * (rank + run0), axis=-1, keepdims=True)

    comb = jnp.where(lane == i1, w1, jnp.where(lane == i2, w2, jnp.where(lane == POS_LANE, pos, onehot_g)))
    comb_ref[...] = comb
    cnt_ref[...] = cnt

    pos_t = comb.T[POS_LANE:POS_LANE + 1, :]
    dest = lax.broadcasted_iota(jnp.int32, (TILE_ROWS, TOK_TILE), 0).astype(F32)
    perm = jnp.where(pos_t == dest, 1.0, 0.0).astype(BF16)
    hc_ref[...] = jnp.dot(perm, h_hi, preferred_element_type=F32).astype(BF16)
    c_parts = _split_bf16(comb, 3)
    cc_ref[...] = (jnp.dot(perm, c_parts[0], preferred_element_type=F32)
                   + jnp.dot(perm, c_parts[1], preferred_element_type=F32)
                   + jnp.dot(perm, c_parts[2], preferred_element_type=F32))


def _router(x, mod, layer, norm_g, rw, rb):
    tm = TOK_TILE
    return pl.pallas_call(
        _router_kernel,
        grid=(N_TOK_TILES,),
        in_specs=[
            pl.BlockSpec((tm, D), lambda i: (i, 0)),
            pl.BlockSpec((1, D), lambda i: (0, 0)),
            _mod_spec(layer, 4, tm, row_axis=0),
            _mod_spec(layer, 3, tm, row_axis=0),
            pl.BlockSpec((2 * D, ROUTER_LANES), lambda i: (0, 0)),
            pl.BlockSpec((1, ROUTER_LANES), lambda i: (0, 0)),
        ],
        out_specs=[
            pl.BlockSpec((None, TILE_ROWS, D), lambda i: (i, 0, 0)),
            pl.BlockSpec((None, TILE_ROWS, ROUTER_LANES), lambda i: (i, 0, 0)),
            pl.BlockSpec((tm, ROUTER_LANES), lambda i: (i, 0)),
            pl.BlockSpec((None, 1, ROUTER_LANES), lambda i: (i, 0, 0)),
        ],
        out_shape=[
            jax.ShapeDtypeStruct((N_TOK_TILES, TILE_ROWS, D), BF16),
            jax.ShapeDtypeStruct((N_TOK_TILES, TILE_ROWS, ROUTER_LANES), F32),
            jax.ShapeDtypeStruct((N_TOK, ROUTER_LANES), F32),
            jax.ShapeDtypeStruct((N_TOK_TILES, 1, ROUTER_LANES), F32),
        ],
        compiler_params=_params("arbitrary"),
        name="router",
    )(x, norm_g.reshape(1, D), mod, mod, rw, rb)


def _router_weights(rgw, rgb, rew, reb):
    n_out = N_GROUPS + N_EXPERTS
    rw = jnp.concatenate([rgw, rew.reshape(D, N_EXPERTS)], axis=1)
    hi = rw.astype(BF16)
    lo = (rw - hi.astype(F32)).astype(BF16)
    zeros = lambda n: jnp.zeros((D, n), BF16)
    top = jnp.concatenate([hi, zeros(LOW_LANE0 - n_out), lo, zeros(ROUTER_LANES - LOW_LANE0 - n_out)], axis=1)
    bottom = jnp.concatenate([hi, zeros(ROUTER_LANES - n_out)], axis=1)
    rb = jnp.concatenate([rgb, reb.reshape(N_EXPERTS)])
    rb = jnp.pad(rb, (0, ROUTER_LANES - n_out)).reshape(1, ROUTER_LANES)
    return jnp.concatenate([top, bottom], axis=0), rb


def _moe_plan(cnt):
    i32 = jnp.int32
    padded = (cnt + (ROW_ALIGN - 1)) // ROW_ALIGN * ROW_ALIGN
    run0 = jnp.cumsum(padded, axis=1) - padded
    length = jnp.sum(padded, axis=0)
    span = (length + ZERO_ROWS + MOE_TILE - 1) // MOE_TILE * MOE_TILE
    start = jnp.cumsum(span) - span
    off = start[None, :] + jnp.cumsum(padded, axis=0) - padded
    end = start + length
    need = length
    ntile = (need + MOE_TILE - 1) // MOE_TILE
    cum = jnp.cumsum(ntile)
    total = cum[-1]
    k = jnp.arange(MOE_TILES_MAX, dtype=i32)
    kk = jnp.minimum(k, jnp.maximum(total - 1, 0))
    grp = jnp.minimum(jnp.sum((kk[:, None] >= cum[None, :]).astype(i32), axis=1), N_GROUPS - 1)
    j = kk - (cum - ntile)[grp]
    blk = start[grp] // MOE_TILE + j
    nsub = jnp.clip((need[grp] - j * MOE_TILE + MOE_SUB - 1) // MOE_SUB, 0, MOE_TILE // MOE_SUB)
    nsub = jnp.where(k < total, nsub, 0)
    runs = tuple(a.reshape(-1).astype(i32) for a in (off, run0, padded))
    return runs, end.astype(i32), (blk.astype(i32), grp.astype(i32), nsub.astype(i32))


def _run_pieces(run_refs, tile, g):
    off_ref, run0_ref, padded_ref = run_refs
    idx = tile * N_GROUPS + g
    n = padded_ref[idx]
    src0 = run0_ref[idx]
    dst0 = off_ref[idx]
    pieces = []
    for b, bit in enumerate(RUN_BITS):
        done = n & (-2 * bit)
        pieces.append((b, (n & bit) != 0, pl.multiple_of(src0 + done, ROW_ALIGN),
                       pl.multiple_of(dst0 + done, ROW_ALIGN), bit))
    return pieces


def _dispatch_kernel(off_ref, run0_ref, padded_ref, end_ref, hc_ref, cc_ref, hs_ref, cs_ref,
                     zh_ref, zc_ref, sem, sem_z):
    run_refs = (off_ref, run0_ref, padded_ref)

    def for_each_copy(tile, slot, fn):
        for g in range(N_GROUPS):
            for b, pred, src, dst, rows in _run_pieces(run_refs, tile, g):
                @pl.when(pred)
                def _():
                    fn(pltpu.make_async_copy(hc_ref.at[tile, pl.ds(src, rows), :],
                                             hs_ref.at[pl.ds(dst, rows), :], sem.at[slot, 0, g, b]))
                    fn(pltpu.make_async_copy(cc_ref.at[tile, pl.ds(src, rows), :],
                                             cs_ref.at[pl.ds(dst, rows), :], sem.at[slot, 1, g, b]))

    zh_ref[...] = jnp.zeros_like(zh_ref)
    zc_ref[...] = jnp.zeros_like(zc_ref)
    zero_copies = []
    for g in range(N_GROUPS):
        r0 = pl.multiple_of(end_ref[g], ROW_ALIGN)
        zero_copies.append(pltpu.make_async_copy(zh_ref, hs_ref.at[pl.ds(r0, ZERO_ROWS), :], sem_z.at[0, g]))
        zero_copies.append(pltpu.make_async_copy(zc_ref, cs_ref.at[pl.ds(r0, ZERO_ROWS), :], sem_z.at[1, g]))
    for cp in zero_copies:
        cp.start()

    def body(i, carry):
        for_each_copy(i, i % 2, lambda cp: cp.start())

        @pl.when(i > 0)
        def _():
            for_each_copy(i - 1, (i - 1) % 2, lambda cp: cp.wait())

        return carry

    lax.fori_loop(0, N_TOK_TILES, body, 0)
    for_each_copy(N_TOK_TILES - 1, (N_TOK_TILES - 1) % 2, lambda cp: cp.wait())
    for cp in zero_copies:
        cp.wait()


def _dispatch(hc, cc, runs, end):
    any_spec = pl.BlockSpec(memory_space=pl.ANY)
    grid_spec = pltpu.PrefetchScalarGridSpec(
        num_scalar_prefetch=4,
        grid=(1,),
        in_specs=[any_spec, any_spec],
        out_specs=[any_spec, any_spec],
        scratch_shapes=[
            pltpu.VMEM((ZERO_ROWS, D), BF16),
            pltpu.VMEM((ZERO_ROWS, ROUTER_LANES), F32),
            pltpu.SemaphoreType.DMA((2, 2, N_GROUPS, len(RUN_BITS))),
            pltpu.SemaphoreType.DMA((2, N_GROUPS)),
        ],
    )
    return pl.pallas_call(
        _dispatch_kernel,
        grid_spec=grid_spec,
        out_shape=[jax.ShapeDtypeStruct((ROW_CAP, D), BF16),
                   jax.ShapeDtypeStruct((ROW_CAP, ROUTER_LANES), F32)],
        compiler_params=_params("arbitrary"),
        name="moe_dispatch",
    )(*runs, end, hc, cc)


def _moe_kernel(blk_ref, grp_ref, nsub_ref, h_ref, c_ref, w1_ref, w3_ref, w2_ref, o_ref, hid_ref):
    k = pl.program_id(0)
    c = pl.program_id(1)
    ns = nsub_ref[k]
    n_sub_max = MOE_TILE // MOE_SUB

    def for_row_count(fn):
        for m in range(1, n_sub_max + 1):
            @pl.when(ns == m)
            def _():
                fn(m * MOE_SUB)

    @pl.when(jnp.logical_and(ns > 0, c < N_FF_CHUNKS))
    def _():
        w1 = w1_ref[...].astype(BF16)
        w3 = w3_ref[...].astype(BF16)
        e_lane = EXPERT_LANE0 + grp_ref[k] * EXP_PER_GROUP + c * FF_CHUNK // D_EXPERT

        def up(rows):
            h = h_ref[0:rows, :]
            a = jnp.dot(h, w1, preferred_element_type=F32)
            b = jnp.dot(h, w3, preferred_element_type=F32)
            comb = c_ref[0:rows, :]
            lane = lax.broadcasted_iota(jnp.int32, comb.shape, 1)
            cw = jnp.sum(jnp.where(lane == e_lane, comb, 0.0), axis=-1, keepdims=True)
            hid_ref[c, 0:rows, :] = (_silu(a) * b * cw).astype(BF16)

        for_row_count(up)

    @pl.when(jnp.logical_and(ns > 0, c >= N_FF_CHUNKS))
    def _():
        w2 = w2_ref[...].reshape(EXP_PER_GROUP * D_EXPERT, OUT_CHUNK).astype(BF16)

        def down(rows):
            hid = jnp.concatenate([hid_ref[j, 0:rows, :] for j in range(N_FF_CHUNKS)], axis=1)
            o_ref[0:rows, :] = jnp.dot(hid, w2, preferred_element_type=F32).astype(BF16)
            if rows < MOE_TILE:
                o_ref[rows:, :] = jnp.zeros((MOE_TILE - rows, OUT_CHUNK), BF16)

        for_row_count(down)


def _moe_experts(hs, cs, w1, w3, w2, layer, plan):
    per_e = D_EXPERT // FF_CHUNK
    n_steps = N_FF_CHUNKS + N_OUT_CHUNKS

    def step_of(k, c, nsub_ref):
        return jnp.where(nsub_ref[k] > 0, c, n_steps - 1)

    def w13_map(k, c, blk_ref, grp_ref, nsub_ref):
        cc = jnp.minimum(step_of(k, c, nsub_ref), N_FF_CHUNKS - 1)
        return (layer, grp_ref[k] * EXP_PER_GROUP + cc // per_e, 0, cc % per_e)

    def out_chunk(k, c, nsub_ref):
        return jnp.maximum(step_of(k, c, nsub_ref) - N_FF_CHUNKS, 0)

    w2_map = lambda k, c, blk_ref, grp_ref, nsub_ref: (layer, grp_ref[k], 0, out_chunk(k, c, nsub_ref))
    row_map = lambda k, c, blk_ref, grp_ref, nsub_ref: (blk_ref[k], 0)
    out_map = lambda k, c, blk_ref, grp_ref, nsub_ref: (blk_ref[k], out_chunk(k, c, nsub_ref))
    grid_spec = pltpu.PrefetchScalarGridSpec(
        num_scalar_prefetch=3,
        grid=(MOE_TILES_MAX, n_steps),
        in_specs=[
            pl.BlockSpec((MOE_TILE, D), row_map),
            pl.BlockSpec((MOE_TILE, ROUTER_LANES), row_map),
            pl.BlockSpec((None, None, D, FF_CHUNK), w13_map),
            pl.BlockSpec((None, None, D, FF_CHUNK), w13_map),
            pl.BlockSpec((None, EXP_PER_GROUP, D_EXPERT, OUT_CHUNK), w2_map),
        ],
        out_specs=pl.BlockSpec((MOE_TILE, OUT_CHUNK), out_map),
        scratch_shapes=[pltpu.VMEM((N_FF_CHUNKS, MOE_TILE, FF_CHUNK), BF16)],
    )
    return pl.pallas_call(
        _moe_kernel,
        grid_spec=grid_spec,
        out_shape=jax.ShapeDtypeStruct((ROW_CAP, D), BF16),
        compiler_params=_params("arbitrary", "arbitrary"),
        name="moe_experts",
    )(*plan, hs, cs, w1, w3, w2)


def _combine_kernel(off_ref, run0_ref, padded_ref, x_ref, gate_ref, comb_ref, ys_ref, *refs, split_out):
    out_refs, (yw_ref, sem) = refs[:-2], refs[-2:]
    run_refs = (off_ref, run0_ref, padded_ref)
    i = pl.program_id(0)
    n_steps = pl.num_programs(0)
    slot = i % 2

    def for_each_copy(tile, sl, fn):
        for g in range(N_GROUPS):
            for b, pred, src, dst, rows in _run_pieces(run_refs, tile, g):
                @pl.when(pred)
                def _():
                    fn(pltpu.make_async_copy(ys_ref.at[pl.ds(dst, rows), :],
                                             yw_ref.at[sl, pl.ds(src, rows), :], sem.at[sl, g, b]))

    def fetch(tile, sl):
        yw_ref[sl, TOK_TILE:, :] = jnp.zeros((TILE_ROWS - TOK_TILE, D), BF16)
        for_each_copy(tile, sl, lambda cp: cp.start())

    @pl.when(i == 0)
    def _():
        fetch(i, slot)

    @pl.when(i + 1 < n_steps)
    def _():
        fetch(i + 1, 1 - slot)

    for_each_copy(i, slot, lambda cp: cp.wait())

    pos = comb_ref[:, POS_LANE:POS_LANE + 1].astype(jnp.int32)
    wcol = lax.broadcasted_iota(jnp.int32, (TOK_TILE, TILE_ROWS), 1)
    perm = jnp.where(wcol == pos, 1.0, 0.0).astype(BF16)
    y = jnp.dot(perm, yw_ref[slot], preferred_element_type=F32)
    out = x_ref[...] + gate_ref[...] * y
    if not split_out:
        out_refs[0][...] = out
    else:
        @pl.when(i < N_CTX // TOK_TILE)
        def _():
            out_refs[0][...] = out

        @pl.when(i >= N_CTX // TOK_TILE)
        def _():
            out_refs[1][...] = out


def _combine(x, mod, layer, comb, ys, runs, split_out):
    n_ctx_tiles = N_CTX // TOK_TILE
    if split_out:
        out_specs = [pl.BlockSpec((TOK_TILE, D), lambda i, *_: (jnp.minimum(i, n_ctx_tiles - 1), 0)),
                     pl.BlockSpec((TOK_TILE, D), lambda i, *_: (jnp.maximum(i - n_ctx_tiles, 0), 0))]
        out_shape = [jax.ShapeDtypeStruct((N_CTX, D), F32), jax.ShapeDtypeStruct((N_LAT, D), F32)]
    else:
        out_specs = [pl.BlockSpec((TOK_TILE, D), lambda i, *_: (i, 0))]
        out_shape = [jax.ShapeDtypeStruct((N_TOK, D), F32)]
    grid_spec = pltpu.PrefetchScalarGridSpec(
        num_scalar_prefetch=3,
        grid=(N_TOK_TILES,),
        in_specs=[
            pl.BlockSpec((TOK_TILE, D), lambda i, *_: (i, 0)),
            _mod_spec(layer, 5, TOK_TILE, row_axis=0),
            pl.BlockSpec((TOK_TILE, ROUTER_LANES), lambda i, *_: (i, 0)),
            pl.BlockSpec(memory_space=pl.ANY),
        ],
        out_specs=out_specs,
        scratch_shapes=[
            pltpu.VMEM((2, TILE_ROWS, D), BF16),
            pltpu.SemaphoreType.DMA((2, N_GROUPS, len(RUN_BITS))),
        ],
    )
    return pl.pallas_call(
        functools.partial(_combine_kernel, split_out=split_out),
        grid_spec=grid_spec,
        out_shape=out_shape,
        compiler_params=_params("arbitrary"),
        name="moe_combine",
    )(*runs, x, mod, comb, ys)


def _moe_sublayer(x, mod, layer, norm_g, rgw, rgb, rew, reb, w1, w3, w2, split_out=False):
    rw, rb = _router_weights(rgw, rgb, rew, reb)
    hc, cc, comb, cnt = _router(x, mod, layer, norm_g, rw, rb)
    runs, end, plan = _moe_plan(cnt[:, 0, :N_GROUPS].astype(jnp.int32))
    hs, cs = _dispatch(hc, cc, runs, end)
    ys = _moe_experts(hs, cs, w1, w3, w2, layer, plan)
    return _combine(x, mod, layer, comb, ys, runs, split_out)


def kernel(x_prompt, x_sample, cache_k, cache_v, c, c_ctx, ada_w, ada_b, norm_mix_g, norm_ffn_g,
           mix_in_w, pool_w, pool_scale, conv_w, conv_b, conv_ln_g, conv_ln_b, mix_out_w,
           qkv_w, q_norm_g, k_norm_g, rpb, attn_out_w, router_g_w, router_g_b, router_e_w,
           router_e_b, exp_w1, exp_w3, exp_w2):
    x = (x_prompt.reshape(N_CTX, D), x_sample.reshape(N_LAT, D))
    cvec8 = jnp.concatenate([c_ctx[None, :], c, jnp.zeros((8 - 1 - N_LAT_SEQ, D), F32)], axis=0)
    mod = _adaln_all(cvec8, ada_w, ada_b)
    n_attn = DEPTH // 2
    cache_k2 = cache_k.reshape(N_LAT_SEQ, n_attn, CTX_SEQ, D)
    cache_v2 = cache_v.reshape(N_LAT_SEQ, n_attn, CTX_SEQ, D)

    caches = None
    for l in range(DEPTH):
        e = l // 2
        if l % 2 == 0:
            u = _mm_prologue(x, mod, l, norm_mix_g[l], mix_in_w, e)
            cat = _seq_mixer(u, pool_w[e].astype(BF16), pool_scale[e], conv_w[e], conv_b[e],
                             conv_ln_g[e], conv_ln_b[e])
            x = _mm_residual(cat, mix_out_w, e, x, mod, l)
        else:
            q, k, v, new_k, new_v = _qkv_proj(x, mod, l, e, norm_mix_g[l], qkv_w,
                                              q_norm_g[e], k_norm_g[e], caches)
            caches = (new_k, new_v)
            o = _ctx_attention(q, k, v)
            o = _na_attention(q, k, v, cache_k2[:, e], cache_v2[:, e], _na_bias_table(rpb[e]), o)
            x = _mm_residual(o, attn_out_w, e, x, mod, l)
        out = _moe_sublayer(x, mod, l, norm_ffn_g[l], router_g_w[l], router_g_b[l], router_e_w[l],
                            router_e_b[l], exp_w1, exp_w3, exp_w2, split_out=(l == DEPTH - 1))
        x = out[0] if l < DEPTH - 1 else tuple(out)

    y_prompt = x[0].reshape(N_CTX_SEQ, CTX_SEQ, D)
    y_sample = x[1].reshape(N_LAT_SEQ, LAT_SEQ, D)
    cache_dims = (N_CTX_SEQ, n_attn, CTX_SEQ, N_HEADS, HEAD_DIM)
    return (y_prompt, y_sample, caches[0].reshape(cache_dims), caches[1].reshape(cache_dims))
```

```python
import functools

import numpy as np
import jax
import jax.numpy as jnp
from jax import lax
from jax.experimental import pallas as pl
from jax.experimental.pallas import tpu as pltpu

D = 2048
N_CTX_SEQ = 16
CTX_SEQ = 256
N_LAT_SEQ = 2
LAT_SEQ = 1024
N_CTX = N_CTX_SEQ * CTX_SEQ
N_LAT = N_LAT_SEQ * LAT_SEQ
N_TOK = N_CTX + N_LAT
DEPTH = 4
GRID_W = 64
GRID_H = LAT_SEQ // GRID_W
N_HEADS = 16
HEAD_DIM = 128
ATTN_SCALE = HEAD_DIM ** -0.5
D_POOL = 1024
D_CONV = 1024
POOL_WINDOWS = (2, 4, 8, 16)
POOL_GC = 256
CONV_W = 31
NA_KH = 8
NA_KW = 16
N_GROUPS = 4
EXP_PER_GROUP = 4
N_EXPERTS = 16
D_EXPERT = 512
EPS = 1e-6
NEG = -1e30

SEQ_TILE = 256
HALO = 16
ROUTER_LANES = 128
EXPERT_LANE0 = N_GROUPS
POS_LANE = EXPERT_LANE0 + N_EXPERTS
LOW_LANE0 = 32
VMEM_LIMIT = 56 * 1024 * 1024

TOK_TILE = 256
N_TOK_TILES = N_TOK // TOK_TILE
ROW_ALIGN = 16
RUN_BITS = (256, 128, 64, 32, 16)
TILE_ROWS = TOK_TILE + N_GROUPS * ROW_ALIGN
MOE_TILE = 1024
MOE_SUB = 256
FF_CHUNK = 512
OUT_CHUNK = 512
N_FF_CHUNKS = EXP_PER_GROUP * D_EXPERT // FF_CHUNK
N_OUT_CHUNKS = D // OUT_CHUNK
ZERO_ROWS = MOE_SUB
MAX_PADDED = N_TOK + N_TOK_TILES * N_GROUPS * (ROW_ALIGN - 1)
MOE_TILES_MAX = (MAX_PADDED + N_GROUPS * (MOE_TILE - 1)) // MOE_TILE
ROW_CAP = -(-(MAX_PADDED + N_GROUPS * (ZERO_ROWS + MOE_TILE - 1)) // MOE_TILE) * MOE_TILE

F32 = jnp.float32
BF16 = jnp.bfloat16


def _cond_of_row(row):
    return jnp.where(row < N_CTX, 0, 1 + (row - N_CTX) // LAT_SEQ)


def _mod_spec(layer, which, tm, tn=D, row_axis=1, col_axis=None):
    def index_map(*ids):
        cond = _cond_of_row(ids[row_axis] * tm)
        col = 0 if col_axis is None else ids[col_axis]
        return (layer, cond, which, 0, col)
    return pl.BlockSpec((None, None, None, 1, tn), index_map)


def _silu(x):
    return x / (1.0 + jnp.exp(-x))


def _modnorm(x, g, sc, sh):
    ms = jnp.mean(x * x, axis=-1, keepdims=True)
    y = x * lax.rsqrt(ms + EPS) * g
    return y * (1.0 + sc) + sh


def _params(*sem):
    return pltpu.CompilerParams(dimension_semantics=sem, vmem_limit_bytes=VMEM_LIMIT)


def _ada_kernel(c_ref, w_ref, b_ref, o_ref):
    s = _silu(c_ref[...]).astype(BF16)
    o_ref[...] = jnp.dot(s, w_ref[...].astype(BF16), preferred_element_type=F32) + b_ref[...]


def _adaln_all(cvec8, ada_w, ada_b):
    tn = 1024
    n_out = 6 * D
    out = pl.pallas_call(
        _ada_kernel,
        grid=(DEPTH, n_out // tn),
        in_specs=[
            pl.BlockSpec((8, D), lambda l, j: (0, 0)),
            pl.BlockSpec((None, D, tn), lambda l, j: (l, 0, j)),
            pl.BlockSpec((None, 1, tn), lambda l, j: (l, 0, j)),
        ],
        out_specs=pl.BlockSpec((None, 8, tn), lambda l, j: (l, 0, j)),
        out_shape=jax.ShapeDtypeStruct((DEPTH, 8, n_out), F32),
        compiler_params=_params("arbitrary", "arbitrary"),
        name="adaln",
    )(cvec8, ada_w, ada_b.reshape(DEPTH, 1, n_out))
    return out[:, :3].reshape(DEPTH, 3, 6, 1, D)


def _token_specs(x, tm, tn, idx):
    if not isinstance(x, tuple):
        return [pl.BlockSpec((tm, tn), lambda j, i: idx(i, j))], [x]
    n_ctx_tiles = N_CTX // tm
    ctx_map = lambda j, i: idx(jnp.minimum(i, n_ctx_tiles - 1), j)
    lat_map = lambda j, i: idx(jnp.maximum(i - n_ctx_tiles, 0), j)
    return [pl.BlockSpec((tm, tn), ctx_map), pl.BlockSpec((tm, tn), lat_map)], list(x)


def _token_rows(refs, i, tm):
    if len(refs) == 1:
        return refs[0][...]
    return jnp.where(i < N_CTX // tm, refs[0][...], refs[1][...])


def _cast_weights_once(i, pairs):
    @pl.when(i == 0)
    def _():
        for w_ref, wb_ref in pairs:
            wb_ref[...] = w_ref[...].astype(BF16)


def _mm_pro_kernel(*refs, n_x, tm):
    x_refs = refs[:n_x]
    g_ref, sc_ref, sh_ref, w_ref, o_ref, wb_ref = refs[n_x:]
    i = pl.program_id(1)
    _cast_weights_once(i, [(w_ref, wb_ref)])
    h = _modnorm(_token_rows(x_refs, i, tm), g_ref[...], sc_ref[...], sh_ref[...]).astype(BF16)
    o_ref[...] = jnp.dot(h, wb_ref[...], preferred_element_type=F32)


def _mm_plain_kernel(h_ref, w_ref, o_ref, wb_ref):
    _cast_weights_once(pl.program_id(1), [(w_ref, wb_ref)])
    o_ref[...] = jnp.dot(h_ref[...], wb_ref[...], preferred_element_type=F32)


def _mm_prologue(x, mod, layer, norm_g, w, w_idx, h=None, tm=512, tn=1024):
    f = w.shape[2]
    w_spec = pl.BlockSpec((None, D, tn), lambda j, i: (w_idx, 0, j))
    if h is None:
        x_specs, x_args = _token_specs(x, tm, D, lambda i, j: (i, 0))
        body = functools.partial(_mm_pro_kernel, n_x=len(x_args), tm=tm)
        in_specs = x_specs + [pl.BlockSpec((1, D), lambda j, i: (0, 0)),
                              _mod_spec(layer, 1, tm), _mod_spec(layer, 0, tm), w_spec]
        args = x_args + [norm_g.reshape(1, D), mod, mod, w]
    else:
        body = _mm_plain_kernel
        in_specs = [pl.BlockSpec((tm, D), lambda j, i: (i, 0)), w_spec]
        args = [h, w]
    return pl.pallas_call(
        body,
        grid=(f // tn, N_TOK // tm),
        in_specs=in_specs,
        out_specs=pl.BlockSpec((tm, tn), lambda j, i: (i, j)),
        out_shape=jax.ShapeDtypeStruct((N_TOK, f), F32),
        scratch_shapes=[pltpu.VMEM((D, tn), BF16)],
        compiler_params=_params("arbitrary", "arbitrary"),
        name="mm_prologue",
    )(*args)


def _qkv_kernel(h_ref, wq_ref, wk_ref, wv_ref, qg_ref, kg_ref, *rest, tn, n_ctx_tiles, seqs_per_tile):
    q_ref, k_ref, v_ref, ck_ref, cv_ref, wqb_ref, wkb_ref, wvb_ref = rest[-8:]
    i = pl.program_id(1)
    _cast_weights_once(i, [(wq_ref, wqb_ref), (wk_ref, wkb_ref), (wv_ref, wvb_ref)])
    h = h_ref[...]
    q = jnp.dot(h, wqb_ref[...], preferred_element_type=F32)
    k = jnp.dot(h, wkb_ref[...], preferred_element_type=F32)
    v = jnp.dot(h, wvb_ref[...], preferred_element_type=F32)
    is_ctx = i < n_ctx_tiles
    for hh in range(tn // HEAD_DIM):
        ls = slice(hh * HEAD_DIM, (hh + 1) * HEAD_DIM)
        qh = q[:, ls]
        kh = k[:, ls]
        qn = qh * lax.rsqrt(jnp.mean(qh * qh, axis=-1, keepdims=True) + EPS) * qg_ref[:, ls]
        kn = kh * lax.rsqrt(jnp.mean(kh * kh, axis=-1, keepdims=True) + EPS) * kg_ref[:, ls]
        q_ref[:, ls] = qn.astype(BF16)
        k_ref[:, ls] = kn.astype(BF16)

        @pl.when(is_ctx)
        def _():
            ck_ref[:, :, ls] = kn.reshape(seqs_per_tile, CTX_SEQ, HEAD_DIM)

    v_ref[...] = v.astype(BF16)

    @pl.when(is_ctx)
    def _():
        cv_ref[...] = v.reshape(seqs_per_tile, CTX_SEQ, tn)


def _qkv_proj(h, w_idx, w, q_gain, k_gain, caches, tm=512, tn=512):
    n = h.shape[0]
    n_attn = DEPTH // 2
    ncol = D // tn
    seqs_per_tile = tm // CTX_SEQ
    n_ctx_tiles = N_CTX // tm
    cache_shape = jax.ShapeDtypeStruct((N_CTX_SEQ, n_attn, CTX_SEQ, D), F32)
    cache_spec = pl.BlockSpec(
        (seqs_per_tile, None, CTX_SEQ, tn),
        lambda j, i: (jnp.minimum(i, n_ctx_tiles - 1), w_idx, 0, j))
    in_specs = [
        pl.BlockSpec((tm, D), lambda j, i: (i, 0)),
        pl.BlockSpec((None, D, tn), lambda j, i: (w_idx, 0, j)),
        pl.BlockSpec((None, D, tn), lambda j, i: (w_idx, 0, ncol + j)),
        pl.BlockSpec((None, D, tn), lambda j, i: (w_idx, 0, 2 * ncol + j)),
        pl.BlockSpec((1, tn), lambda j, i: (0, j)),
        pl.BlockSpec((1, tn), lambda j, i: (0, j)),
    ]
    args = [h, w, w, w,
            jnp.tile(q_gain, N_HEADS).reshape(1, D), jnp.tile(k_gain, N_HEADS).reshape(1, D)]
    aliases = {}
    if caches is not None:
        in_specs += [pl.BlockSpec(memory_space=pl.ANY), pl.BlockSpec(memory_space=pl.ANY)]
        aliases = {len(args): 3, len(args) + 1: 4}
        args += list(caches)
    act_spec = pl.BlockSpec((tm, tn), lambda j, i: (i, j))
    act_shape = jax.ShapeDtypeStruct((n, D), BF16)
    return pl.pallas_call(
        functools.partial(_qkv_kernel, tn=tn, n_ctx_tiles=n_ctx_tiles, seqs_per_tile=seqs_per_tile),
        grid=(ncol, n // tm),
        in_specs=in_specs,
        out_specs=[act_spec, act_spec, act_spec, cache_spec, cache_spec],
        out_shape=[act_shape, act_shape, act_shape, cache_shape, cache_shape],
        input_output_aliases=aliases,
        scratch_shapes=[pltpu.VMEM((D, tn), BF16)] * 3,
        compiler_params=_params("arbitrary", "arbitrary"),
        name="qkv_proj",
    )(*args)


def _mm_res_kernel(a_ref, w_ref, gate_ref, *refs, tm):
    x_refs, (o_ref, wb_ref) = refs[:-2], refs[-2:]
    i = pl.program_id(1)
    _cast_weights_once(i, [(w_ref, wb_ref)])
    y = jnp.dot(a_ref[...], wb_ref[...], preferred_element_type=F32)
    o_ref[...] = _token_rows(x_refs, i, tm) + gate_ref[...] * y


def _mm_residual(a_bf16, w, w_idx, x, mod, layer, tm=512, tn=1024):
    k = a_bf16.shape[1]
    x_specs, x_args = _token_specs(x, tm, tn, lambda i, j: (i, j))
    return pl.pallas_call(
        functools.partial(_mm_res_kernel, tm=tm),
        grid=(D // tn, N_TOK // tm),
        in_specs=[
            pl.BlockSpec((tm, k), lambda j, i: (i, 0)),
            pl.BlockSpec((None, k, tn), lambda j, i: (w_idx, 0, j)),
            _mod_spec(layer, 2, tm, tn=tn, col_axis=0),
        ] + x_specs,
        out_specs=pl.BlockSpec((tm, tn), lambda j, i: (i, j)),
        out_shape=jax.ShapeDtypeStruct((N_TOK, D), F32),
        scratch_shapes=[pltpu.VMEM((k, tn), BF16)],
        compiler_params=_params("arbitrary", "arbitrary"),
        name="mm_residual",
    )(a_bf16, w, mod, *x_args)


def _seq_kernel(u_ref, top_ref, bot_ref, pw_ref, ps_ref, cw_ref, cb_ref, lg_ref, lb_ref,
                o_ref, zpad_ref, conv_ref):
    i = pl.program_id(0)
    n_ctx_tiles = N_CTX // SEQ_TILE
    tiles_per_lat = LAT_SEQ // SEQ_TILE
    is_lat = i >= n_ctx_tiles
    chunk = jnp.where(is_lat, (i - n_ctx_tiles) % tiles_per_lat, 0)
    top_ok = jnp.logical_and(is_lat, chunk > 0)
    bot_ok = jnp.logical_and(is_lat, chunk < tiles_per_lat - 1)
    seq_len = jnp.where(is_lat, LAT_SEQ, CTX_SEQ)
    t = chunk * SEQ_TILE + lax.broadcasted_iota(jnp.int32, (SEQ_TILE, 1), 0)
    rows = SEQ_TILE + 2 * HALO

    for g, win in enumerate(POOL_WINDOWS):
        ls = slice(g * POOL_GC, (g + 1) * POOL_GC)
        mid = u_ref[:, ls]
        top = jnp.where(top_ok, top_ref[:, ls], 0.0)
        bot = jnp.where(bot_ok, bot_ref[:, ls], 0.0)
        up = jnp.concatenate([top, mid, bot], axis=0)
        s = pltpu.roll(up, 1, 0) + up
        if win >= 4:
            s = pltpu.roll(s, 1, 0) + pltpu.roll(s, rows - 1, 0)
        if win >= 8:
            s = pltpu.roll(s, 2, 0) + pltpu.roll(s, rows - 2, 0)
        if win >= 16:
            s = pltpu.roll(s, 4, 0) + pltpu.roll(s, rows - 4, 0)
        s = s[HALO:HALO + SEQ_TILE]
        lo = jnp.maximum(t - win // 2, 0)
        hi = jnp.minimum(t + (win - win // 2) - 1, seq_len - 1)
        cnt = (hi - lo + 1).astype(F32)
        p = (s / cnt - mid).astype(BF16)
        y = jnp.dot(p, pw_ref[g], preferred_element_type=F32) * ps_ref[:, ls]
        o_ref[:, ls] = y.astype(BF16)

    def glu(ref):
        return ref[:, D_POOL:D_POOL + D_CONV] * (1.0 / (1.0 + jnp.exp(-ref[:, D_POOL + D_CONV:])))

    zpad_ref[0:HALO, :] = jnp.where(top_ok, glu(top_ref), 0.0)
    zpad_ref[HALO:HALO + SEQ_TILE, :] = glu(u_ref)
    zpad_ref[HALO + SEQ_TILE:rows, :] = jnp.where(bot_ok, glu(bot_ref), 0.0)
    for lt in range(D_CONV // 128):
        ls = slice(lt * 128, (lt + 1) * 128)
        zp = zpad_ref[:, ls]
        acc = jnp.zeros((SEQ_TILE, 128), F32)
        for b in range(8):
            sb = zp if b == 0 else pltpu.roll(zp, rows - b, 0)
            for a in range(4):
                off = 8 * a + b
                if 1 <= off <= CONV_W:
                    acc = acc + cw_ref[off - 1:off, ls] * sb[8 * a:8 * a + SEQ_TILE]
        conv_ref[:, ls] = acc + cb_ref[:, ls]
    zc = conv_ref[...]
    mu = jnp.mean(zc, axis=-1, keepdims=True)
    d = zc - mu
    var = jnp.mean(d * d, axis=-1, keepdims=True)
    zn = d * lax.rsqrt(var + EPS) * lg_ref[...] + lb_ref[...]
    o_ref[:, D_POOL:] = _silu(zn).astype(BF16)


def _seq_mixer(u, pool_w_bf16, pool_scale, conv_w, conv_b, ln_g, ln_b):
    n = u.shape[0]
    f_in = u.shape[1]
    n_tiles = n // SEQ_TILE
    hb = SEQ_TILE // HALO
    n_hblocks = n // HALO
    return pl.pallas_call(
        _seq_kernel,
        grid=(n_tiles,),
        in_specs=[
            pl.BlockSpec((SEQ_TILE, f_in), lambda i: (i, 0)),
            pl.BlockSpec((HALO, f_in), lambda i: (jnp.maximum(i * hb - 1, 0), 0)),
            pl.BlockSpec((HALO, f_in), lambda i: (jnp.minimum((i + 1) * hb, n_hblocks - 1), 0)),
            pl.BlockSpec((len(POOL_WINDOWS), POOL_GC, POOL_GC), lambda i: (0, 0, 0)),
            pl.BlockSpec((1, D_POOL), lambda i: (0, 0)),
            pl.BlockSpec((CONV_W, D_CONV), lambda i: (0, 0)),
            pl.BlockSpec((1, D_CONV), lambda i: (0, 0)),
            pl.BlockSpec((1, D_CONV), lambda i: (0, 0)),
            pl.BlockSpec((1, D_CONV), lambda i: (0, 0)),
        ],
        out_specs=pl.BlockSpec((SEQ_TILE, D_POOL + D_CONV), lambda i: (i, 0)),
        out_shape=jax.ShapeDtypeStruct((n, D_POOL + D_CONV), BF16),
        scratch_shapes=[
            pltpu.VMEM((SEQ_TILE + 2 * HALO, D_CONV), F32),
            pltpu.VMEM((SEQ_TILE, D_CONV), F32),
        ],
        compiler_params=_params("arbitrary"),
        name="seq_mixer",
    )(u, u, u, pool_w_bf16, pool_scale.reshape(1, D_POOL), conv_w, conv_b.reshape(1, D_CONV),
      ln_g.reshape(1, D_CONV), ln_b.reshape(1, D_CONV))


_NT = (((1,), (1,)), ((), ()))


def _ctx_attn_kernel(q_ref, k_ref, v_ref, o_ref, s_ref, p_ref):
    for h in range(N_HEADS):
        ls = slice(h * HEAD_DIM, (h + 1) * HEAD_DIM)
        s_ref[h] = lax.dot_general(q_ref[:, ls], k_ref[:, ls], _NT, preferred_element_type=F32) * ATTN_SCALE
    for h in range(N_HEADS):
        s = s_ref[h]
        e = jnp.exp(s - jnp.max(s, axis=-1, keepdims=True))
        p_ref[h] = (e * (1.0 / jnp.sum(e, axis=-1, keepdims=True))).astype(BF16)
    for h in range(N_HEADS):
        ls = slice(h * HEAD_DIM, (h + 1) * HEAD_DIM)
        o_ref[:, ls] = jnp.dot(p_ref[h], v_ref[:, ls], preferred_element_type=F32).astype(BF16)


def _ctx_attention(q, k, v):
    n = q.shape[0]
    spec = pl.BlockSpec((CTX_SEQ, D), lambda b: (b, 0))
    return pl.pallas_call(
        _ctx_attn_kernel,
        grid=(N_CTX_SEQ,),
        in_specs=[spec, spec, spec],
        out_specs=spec,
        out_shape=jax.ShapeDtypeStruct((n, D), BF16),
        scratch_shapes=[pltpu.VMEM((N_HEADS, CTX_SEQ, CTX_SEQ), F32),
                        pltpu.VMEM((N_HEADS, CTX_SEQ, CTX_SEQ), BF16)],
        compiler_params=_params("arbitrary"),
        name="ctx_attention",
    )(q, k, v)


def _bias_table_kernel(r_ref, e_ref, o_ref):
    x = r_ref[...]
    hi = x.astype(BF16)
    r1 = x - hi.astype(F32)
    mid = r1.astype(BF16)
    lo = (r1 - mid.astype(F32)).astype(BF16)
    e = e_ref[...]
    o_ref[...] = (jnp.dot(hi, e, preferred_element_type=F32)
                  + jnp.dot(mid, e, preferred_element_type=F32)
                  + jnp.dot(lo, e, preferred_element_type=F32))


def _na_static_tables():
    cq = np.arange(GRID_W)[:, None]
    kc = np.arange(GRID_W)[None, :]
    dcol = kc - cq + NA_KW - 1
    n_dc = 2 * NA_KW - 1
    onehot = np.zeros((128, GRID_W, GRID_W), np.float32)
    for dc in range(n_dc):
        onehot[dc] = (dcol == dc)
    win0 = np.clip(cq - NA_KW // 2, 0, GRID_W - NA_KW)
    mask = ((kc >= win0) & (kc < win0 + NA_KW)).astype(np.float32)
    return onehot.reshape(128, GRID_W * GRID_W), np.tile(mask, (1, NA_KH))


def _na_bias_table(rpb_e):
    n_dr = 2 * NA_KH - 1
    n_dc = 2 * NA_KW - 1
    onehot, _ = _na_static_tables()
    rows = N_HEADS * n_dr
    r2 = jnp.pad(rpb_e.reshape(rows, n_dc), ((0, 0), (0, 128 - n_dc)))
    tz = pl.pallas_call(
        _bias_table_kernel,
        grid=(1,),
        in_specs=[pl.BlockSpec((rows, 128), lambda i: (0, 0)),
                  pl.BlockSpec((128, GRID_W * GRID_W), lambda i: (0, 0))],
        out_specs=pl.BlockSpec((rows, GRID_W * GRID_W), lambda i: (0, 0)),
        out_shape=jax.ShapeDtypeStruct((rows, GRID_W * GRID_W), F32),
        compiler_params=pltpu.CompilerParams(vmem_limit_bytes=VMEM_LIMIT),
        name="na_bias_table",
    )(r2, jnp.asarray(onehot, BF16))
    tz = tz.reshape(N_HEADS, n_dr, GRID_W, GRID_W)
    return jnp.concatenate([tz[:, :-1], tz[:, 1:]], axis=-1)


def _na_key_row0(r):
    return min(max(r - NA_KH // 2, 0), GRID_H - NA_KH)


def _na_segments():
    segs = []
    for r in range(GRID_H):
        kr0 = _na_key_row0(r)
        if segs and segs[-1][2] == kr0:
            segs[-1][1] = r + 1
        else:
            segs.append([r, r + 1, kr0])
    return segs


def _na_kernel(q_ref, k_ref, v_ref, kc_ref, vc_ref, tz_ref, mask_ref, o_in_ref, o_ref,
               s_ref, p_ref, acc_ref):
    del o_in_ref
    band = NA_KH * GRID_W
    n_ctx = CTX_SEQ
    kc = kc_ref[...].astype(BF16)
    vc = vc_ref[...].astype(BF16)
    for r0, r1, kr0 in _na_segments():
        rs = slice(r0 * GRID_W, r1 * GRID_W)
        kb = k_ref[kr0 * GRID_W:kr0 * GRID_W + band, :]
        s_ref[rs, 0:band] = lax.dot_general(q_ref[rs, :], kb, _NT, preferred_element_type=F32) * ATTN_SCALE
    s_ref[:, band:band + n_ctx] = lax.dot_general(q_ref[...], kc, _NT, preferred_element_type=F32) * ATTN_SCALE
    mask = mask_ref[...] > 0.5
    for r in range(GRID_H):
        rs = slice(r * GRID_W, (r + 1) * GRID_W)
        kr0 = _na_key_row0(r)
        bias = jnp.concatenate(
            [tz_ref[kr0 + j - r + NA_KH - 1] for j in range(0, NA_KH, 2)], axis=-1)
        s_loc = jnp.where(mask, s_ref[rs, 0:band] + bias, NEG)
        s_ctx = s_ref[rs, band:band + n_ctx]
        m = jnp.maximum(jnp.max(s_loc, axis=-1, keepdims=True), jnp.max(s_ctx, axis=-1, keepdims=True))
        e_loc = jnp.exp(s_loc - m)
        e_ctx = jnp.exp(s_ctx - m)
        inv = 1.0 / (jnp.sum(e_loc, axis=-1, keepdims=True) + jnp.sum(e_ctx, axis=-1, keepdims=True))
        p_ref[rs, 0:band] = (e_loc * inv).astype(BF16)
        p_ref[rs, band:band + n_ctx] = (e_ctx * inv).astype(BF16)
    for r0, r1, kr0 in _na_segments():
        rs = slice(r0 * GRID_W, r1 * GRID_W)
        vb = v_ref[kr0 * GRID_W:kr0 * GRID_W + band, :]
        acc_ref[rs, :] = jnp.dot(p_ref[rs, 0:band], vb, preferred_element_type=F32)
    o_ctx = jnp.dot(p_ref[:, band:band + n_ctx], vc, preferred_element_type=F32)
    o_ref[...] = (acc_ref[...] + o_ctx).astype(BF16)


def _na_attention(q, k, v, cache_k_e, cache_v_e, bias_table, o_ctx):
    _, mask = _na_static_tables()
    lat0 = N_CTX // LAT_SEQ
    n_dr2 = bias_table.shape[1]
    n_keys = NA_KH * GRID_W + CTX_SEQ
    tok_spec = pl.BlockSpec((LAT_SEQ, HEAD_DIM), lambda b, h: (lat0 + b, h))
    cache_spec = pl.BlockSpec((None, CTX_SEQ, HEAD_DIM), lambda b, h: (b, 0, h))
    return pl.pallas_call(
        _na_kernel,
        grid=(N_LAT_SEQ, N_HEADS),
        in_specs=[
            tok_spec, tok_spec, tok_spec, cache_spec, cache_spec,
            pl.BlockSpec((None, n_dr2, GRID_W, 2 * GRID_W), lambda b, h: (h, 0, 0, 0)),
            pl.BlockSpec((GRID_W, NA_KH * GRID_W), lambda b, h: (0, 0)),
            pl.BlockSpec(memory_space=pl.ANY),
        ],
        out_specs=tok_spec,
        out_shape=jax.ShapeDtypeStruct(o_ctx.shape, o_ctx.dtype),
        input_output_aliases={7: 0},
        scratch_shapes=[pltpu.VMEM((LAT_SEQ, n_keys), F32),
                        pltpu.VMEM((LAT_SEQ, n_keys), BF16),
                        pltpu.VMEM((LAT_SEQ, HEAD_DIM), F32)],
        compiler_params=_params("arbitrary", "arbitrary"),
        name="na_attention",
    )(q, k, v, cache_k_e, cache_v_e, bias_table, jnp.asarray(mask), o_ctx)


def _split_bf16(x, pieces):
    out = []
    for _ in range(pieces):
        p = x.astype(BF16)
        out.append(p)
        x = x - p.astype(F32)
    return out


def _router_kernel(x_ref, g_ref, sc_ref, sh_ref, rw_ref, rb_ref, hc_ref, cc_ref, comb_ref, cnt_ref):
    h = _modnorm(x_ref[...], g_ref[...], sc_ref[...], sh_ref[...])
    h_hi, h_lo = _split_bf16(h, 2)
    prod = jnp.dot(jnp.concatenate([h_hi, h_lo], axis=1), rw_ref[...], preferred_element_type=F32)
    logits = prod + pltpu.roll(prod, ROUTER_LANES - LOW_LANE0, 1) + rb_ref[...]
    lane = lax.broadcasted_iota(jnp.int32, logits.shape, 1)
    big = jnp.int32(1 << 20)
    neg_inf = -jnp.inf

    def first_argmax(vals, valid):
        v = jnp.where(valid, vals, neg_inf)
        m = jnp.max(v, axis=-1, keepdims=True)
        idx = jnp.min(jnp.where(jnp.logical_and(valid, v == m), lane, big), axis=-1, keepdims=True)
        return m, idx

    is_g = lane < N_GROUPS
    mg, gi = first_argmax(logits, is_g)
    pg = 1.0 / jnp.sum(jnp.where(is_g, jnp.exp(logits - mg), 0.0), axis=-1, keepdims=True)
    e_lane0 = EXPERT_LANE0 + gi * EXP_PER_GROUP
    is_e = jnp.logical_and(lane >= e_lane0, lane < e_lane0 + EXP_PER_GROUP)
    m1, i1 = first_argmax(logits, is_e)
    m2, i2 = first_argmax(logits, jnp.logical_and(is_e, lane != i1))
    e2 = jnp.exp(m2 - m1)
    den = 1.0 + e2
    w1 = (1.0 / den) * pg
    w2 = (e2 / den) * pg
    onehot_g = jnp.where(lane == gi, 1.0, 0.0)

    row = lax.broadcasted_iota(jnp.int32, (TOK_TILE, TOK_TILE), 0)
    col = lax.broadcasted_iota(jnp.int32, (TOK_TILE, TOK_TILE), 1)
    earlier = jnp.where(col < row, 1.0, 0.0).astype(BF16)
    rank = jnp.dot(earlier, onehot_g.astype(BF16), preferred_element_type=F32)
    cnt = jnp.sum(onehot_g, axis=0, keepdims=True)
    padded = jnp.floor((cnt + (ROW_ALIGN - 1)) * (1.0 / ROW_ALIGN)) * ROW_ALIGN
    padded8 = jnp.broadcast_to(padded, (8, ROUTER_LANES))
    run0 = (pltpu.roll(padded8, 1, 1) + pltpu.roll(padded8, 2, 1) + pltpu.roll(padded8, 3, 1))[0:1, :]
    pos = jnp.sum(onehot_g * (rank + run0), axis=-1, keepdims=True)

    comb = jnp.where(lane == i1, w1, jnp.where(lane == i2, w2, jnp.where(lane == POS_LANE, pos, onehot_g)))
    comb_ref[...] = comb
    cnt_ref[...] = cnt

    pos_t = comb.T[POS_LANE:POS_LANE + 1, :]
    dest = lax.broadcasted_iota(jnp.int32, (TILE_ROWS, TOK_TILE), 0).astype(F32)
    perm = jnp.where(pos_t == dest, 1.0, 0.0).astype(BF16)
    hc_ref[...] = jnp.dot(perm, h_hi, preferred_element_type=F32).astype(BF16)
    c_parts = _split_bf16(comb, 3)
    cc_ref[...] = (jnp.dot(perm, c_parts[0], preferred_element_type=F32)
                   + jnp.dot(perm, c_parts[1], preferred_element_type=F32)
                   + jnp.dot(perm, c_parts[2], preferred_element_type=F32))


def _router(x, mod, layer, norm_g, rw, rb):
    tm = TOK_TILE
    return pl.pallas_call(
        _router_kernel,
        grid=(N_TOK_TILES,),
        in_specs=[
            pl.BlockSpec((tm, D), lambda i: (i, 0)),
            pl.BlockSpec((1, D), lambda i: (0, 0)),
            _mod_spec(layer, 4, tm, row_axis=0),
            _mod_spec(layer, 3, tm, row_axis=0),
            pl.BlockSpec((2 * D, ROUTER_LANES), lambda i: (0, 0)),
            pl.BlockSpec((1, ROUTER_LANES), lambda i: (0, 0)),
        ],
        out_specs=[
            pl.BlockSpec((None, TILE_ROWS, D), lambda i: (i, 0, 0)),
            pl.BlockSpec((None, TILE_ROWS, ROUTER_LANES), lambda i: (i, 0, 0)),
            pl.BlockSpec((tm, ROUTER_LANES), lambda i: (i, 0)),
            pl.BlockSpec((None, 1, ROUTER_LANES), lambda i: (i, 0, 0)),
        ],
        out_shape=[
            jax.ShapeDtypeStruct((N_TOK_TILES, TILE_ROWS, D), BF16),
            jax.ShapeDtypeStruct((N_TOK_TILES, TILE_ROWS, ROUTER_LANES), F32),
            jax.ShapeDtypeStruct((N_TOK, ROUTER_LANES), F32),
            jax.ShapeDtypeStruct((N_TOK_TILES, 1, ROUTER_LANES), F32),
        ],
        compiler_params=_params("arbitrary"),
        name="router",
    )(x, norm_g.reshape(1, D), mod, mod, rw, rb)


def _router_weights(rgw, rgb, rew, reb):
    n_out = N_GROUPS + N_EXPERTS
    rw = jnp.concatenate([rgw, rew.reshape(D, N_EXPERTS)], axis=1)
    hi = rw.astype(BF16)
    lo = (rw - hi.astype(F32)).astype(BF16)
    zeros = lambda n: jnp.zeros((D, n), BF16)
    top = jnp.concatenate([hi, zeros(LOW_LANE0 - n_out), lo, zeros(ROUTER_LANES - LOW_LANE0 - n_out)], axis=1)
    bottom = jnp.concatenate([hi, zeros(ROUTER_LANES - n_out)], axis=1)
    rb = jnp.concatenate([rgb, reb.reshape(N_EXPERTS)])
    rb = jnp.pad(rb, (0, ROUTER_LANES - n_out)).reshape(1, ROUTER_LANES)
    return jnp.concatenate([top, bottom], axis=0), rb


def _moe_plan(cnt):
    i32 = jnp.int32
    padded = (cnt + (ROW_ALIGN - 1)) // ROW_ALIGN * ROW_ALIGN
    run0 = jnp.cumsum(padded, axis=1) - padded
    length = jnp.sum(padded, axis=0)
    span = (length + ZERO_ROWS + MOE_TILE - 1) // MOE_TILE * MOE_TILE
    start = jnp.cumsum(span) - span
    off = start[None, :] + jnp.cumsum(padded, axis=0) - padded
    end = start + length
    need = length
    ntile = (need + MOE_TILE - 1) // MOE_TILE
    cum = jnp.cumsum(ntile)
    total = cum[-1]
    k = jnp.arange(MOE_TILES_MAX, dtype=i32)
    kk = jnp.minimum(k, jnp.maximum(total - 1, 0))
    grp = jnp.minimum(jnp.sum((kk[:, None] >= cum[None, :]).astype(i32), axis=1), N_GROUPS - 1)
    j = kk - (cum - ntile)[grp]
    blk = start[grp] // MOE_TILE + j
    nsub = jnp.clip((need[grp] - j * MOE_TILE + MOE_SUB - 1) // MOE_SUB, 0, MOE_TILE // MOE_SUB)
    nsub = jnp.where(k < total, nsub, 0)
    runs = tuple(a.reshape(-1).astype(i32) for a in (off, run0, padded))
    return runs, end.astype(i32), (blk.astype(i32), grp.astype(i32), nsub.astype(i32))


def _run_pieces(run_refs, tile, g):
    off_ref, run0_ref, padded_ref = run_refs
    idx = tile * N_GROUPS + g
    n = padded_ref[idx]
    src0 = run0_ref[idx]
    dst0 = off_ref[idx]
    pieces = []
    for b, bit in enumerate(RUN_BITS):
        done = n & (-2 * bit)
        pieces.append((b, (n & bit) != 0, pl.multiple_of(src0 + done, ROW_ALIGN),
                       pl.multiple_of(dst0 + done, ROW_ALIGN), bit))
    return pieces


def _dispatch_kernel(off_ref, run0_ref, padded_ref, end_ref, hc_ref, cc_ref, hs_ref, cs_ref,
                     hb_ref, cb_ref, zh_ref, zc_ref, sem, sem_z):
    run_refs = (off_ref, run0_ref, padded_ref)
    i = pl.program_id(0)
    n_steps = pl.num_programs(0)
    slot = i % 2

    def for_each_copy(tile, sl, fn):
        for g in range(N_GROUPS):
            for b, pred, src, dst, rows in _run_pieces(run_refs, tile, g):
                @pl.when(pred)
                def _():
                    fn(pltpu.make_async_copy(hb_ref.at[sl, pl.ds(src, rows), :],
                                             hs_ref.at[pl.ds(dst, rows), :], sem.at[sl, 0, g, b]))
                    fn(pltpu.make_async_copy(cb_ref.at[sl, pl.ds(src, rows), :],
                                             cs_ref.at[pl.ds(dst, rows), :], sem.at[sl, 1, g, b]))

    def zero_copies():
        cps = []
        for g in range(N_GROUPS):
            r0 = pl.multiple_of(end_ref[g], ROW_ALIGN)
            cps.append(pltpu.make_async_copy(zh_ref, hs_ref.at[pl.ds(r0, ZERO_ROWS), :], sem_z.at[0, g]))
            cps.append(pltpu.make_async_copy(zc_ref, cs_ref.at[pl.ds(r0, ZERO_ROWS), :], sem_z.at[1, g]))
        return cps

    @pl.when(i == 0)
    def _():
        zh_ref[...] = jnp.zeros_like(zh_ref)
        zc_ref[...] = jnp.zeros_like(zc_ref)
        for cp in zero_copies():
            cp.start()

    hb_ref[slot] = hc_ref[...]
    cb_ref[slot] = cc_ref[...]
    for_each_copy(i, slot, lambda cp: cp.start())

    @pl.when(i > 0)
    def _():
        for_each_copy(i - 1, 1 - slot, lambda cp: cp.wait())

    @pl.when(i == n_steps - 1)
    def _():
        for_each_copy(i, slot, lambda cp: cp.wait())
        for cp in zero_copies():
            cp.wait()


def _dispatch(hc, cc, runs, end):
    any_spec = pl.BlockSpec(memory_space=pl.ANY)
    grid_spec = pltpu.PrefetchScalarGridSpec(
        num_scalar_prefetch=4,
        grid=(N_TOK_TILES,),
        in_specs=[pl.BlockSpec((None, TILE_ROWS, D), lambda i, *_: (i, 0, 0)),
                  pl.BlockSpec((None, TILE_ROWS, ROUTER_LANES), lambda i, *_: (i, 0, 0))],
        out_specs=[any_spec, any_spec],
        scratch_shapes=[
            pltpu.VMEM((2, TILE_ROWS, D), BF16),
            pltpu.VMEM((2, TILE_ROWS, ROUTER_LANES), F32),
            pltpu.VMEM((ZERO_ROWS, D), BF16),
            pltpu.VMEM((ZERO_ROWS, ROUTER_LANES), F32),
            pltpu.SemaphoreType.DMA((2, 2, N_GROUPS, len(RUN_BITS))),
            pltpu.SemaphoreType.DMA((2, N_GROUPS)),
        ],
    )
    return pl.pallas_call(
        _dispatch_kernel,
        grid_spec=grid_spec,
        out_shape=[jax.ShapeDtypeStruct((ROW_CAP, D), BF16),
                   jax.ShapeDtypeStruct((ROW_CAP, ROUTER_LANES), F32)],
        compiler_params=_params("arbitrary"),
        name="moe_dispatch",
    )(*runs, end, hc, cc)


def _moe_kernel(blk_ref, grp_ref, nsub_ref, h_ref, c_ref, w1_ref, w3_ref, w2_ref, o_ref, hid_ref):
    k = pl.program_id(0)
    c = pl.program_id(1)
    ns = nsub_ref[k]
    n_sub_max = MOE_TILE // MOE_SUB

    def for_row_count(fn):
        for m in range(1, n_sub_max + 1):
            @pl.when(ns == m)
            def _():
                fn(m * MOE_SUB)

    @pl.when(jnp.logical_and(ns > 0, c < N_FF_CHUNKS))
    def _():
        w1 = w1_ref[...].astype(BF16)
        w3 = w3_ref[...].astype(BF16)
        e_lane = EXPERT_LANE0 + grp_ref[k] * EXP_PER_GROUP + c * FF_CHUNK // D_EXPERT

        def up(rows):
            h = h_ref[0:rows, :]
            a = jnp.dot(h, w1, preferred_element_type=F32)
            b = jnp.dot(h, w3, preferred_element_type=F32)
            comb = c_ref[0:rows, :]
            lane = lax.broadcasted_iota(jnp.int32, comb.shape, 1)
            cw = jnp.sum(jnp.where(lane == e_lane, comb, 0.0), axis=-1, keepdims=True)
            hid_ref[c, 0:rows, :] = (_silu(a) * b * cw).astype(BF16)

        for_row_count(up)

    @pl.when(jnp.logical_and(ns > 0, c >= N_FF_CHUNKS))
    def _():
        w2 = w2_ref[...].reshape(EXP_PER_GROUP * D_EXPERT, OUT_CHUNK).astype(BF16)

        def down(rows):
            hid = jnp.concatenate([hid_ref[j, 0:rows, :] for j in range(N_FF_CHUNKS)], axis=1)
            o_ref[0:rows, :] = jnp.dot(hid, w2, preferred_element_type=F32).astype(BF16)
            if rows < MOE_TILE:
                o_ref[rows:, :] = jnp.zeros((MOE_TILE - rows, OUT_CHUNK), BF16)

        for_row_count(down)


def _moe_experts(hs, cs, w1, w3, w2, layer, plan):
    per_e = D_EXPERT // FF_CHUNK
    n_steps = N_FF_CHUNKS + N_OUT_CHUNKS

    def step_of(k, c, nsub_ref):
        return jnp.where(nsub_ref[k] > 0, c, n_steps - 1)

    def w13_map(k, c, blk_ref, grp_ref, nsub_ref):
        cc = jnp.minimum(step_of(k, c, nsub_ref), N_FF_CHUNKS - 1)
        return (layer, grp_ref[k] * EXP_PER_GROUP + cc // per_e, 0, cc % per_e)

    def out_chunk(k, c, nsub_ref):
        return jnp.maximum(step_of(k, c, nsub_ref) - N_FF_CHUNKS, 0)

    w2_map = lambda k, c, blk_ref, grp_ref, nsub_ref: (layer, grp_ref[k], 0, out_chunk(k, c, nsub_ref))
    row_map = lambda k, c, blk_ref, grp_ref, nsub_ref: (blk_ref[k], 0)
    out_map = lambda k, c, blk_ref, grp_ref, nsub_ref: (blk_ref[k], out_chunk(k, c, nsub_ref))
    grid_spec = pltpu.PrefetchScalarGridSpec(
        num_scalar_prefetch=3,
        grid=(MOE_TILES_MAX, n_steps),
        in_specs=[
            pl.BlockSpec((MOE_TILE, D), row_map),
            pl.BlockSpec((MOE_TILE, ROUTER_LANES), row_map),
            pl.BlockSpec((None, None, D, FF_CHUNK), w13_map),
            pl.BlockSpec((None, None, D, FF_CHUNK), w13_map),
            pl.BlockSpec((None, EXP_PER_GROUP, D_EXPERT, OUT_CHUNK), w2_map),
        ],
        out_specs=pl.BlockSpec((MOE_TILE, OUT_CHUNK), out_map),
        scratch_shapes=[pltpu.VMEM((N_FF_CHUNKS, MOE_TILE, FF_CHUNK), BF16)],
    )
    return pl.pallas_call(
        _moe_kernel,
        grid_spec=grid_spec,
        out_shape=jax.ShapeDtypeStruct((ROW_CAP, D), BF16),
        compiler_params=_params("arbitrary", "arbitrary"),
        name="moe_experts",
    )(*plan, hs, cs, w1, w3, w2)


def _combine_kernel(off_ref, run0_ref, padded_ref, x_ref, gate_ref, comb_ref, ys_ref, *refs,
                    split_out, emit_next):
    if emit_next:
        (ng_ref, nsc_ref, nsh_ref), refs = refs[:3], refs[3:]
    out_refs, (yw_ref, sem) = refs[:-2], refs[-2:]
    run_refs = (off_ref, run0_ref, padded_ref)
    i = pl.program_id(0)
    n_steps = pl.num_programs(0)
    slot = i % 2

    def for_each_copy(tile, sl, fn):
        for g in range(N_GROUPS):
            for b, pred, src, dst, rows in _run_pieces(run_refs, tile, g):
                @pl.when(pred)
                def _():
                    fn(pltpu.make_async_copy(ys_ref.at[pl.ds(dst, rows), :],
                                             yw_ref.at[sl, pl.ds(src, rows), :], sem.at[sl, g, b]))

    def fetch(tile, sl):
        yw_ref[sl, TOK_TILE:, :] = jnp.zeros((TILE_ROWS - TOK_TILE, D), BF16)
        for_each_copy(tile, sl, lambda cp: cp.start())

    @pl.when(i == 0)
    def _():
        fetch(i, slot)

    @pl.when(i + 1 < n_steps)
    def _():
        fetch(i + 1, 1 - slot)

    for_each_copy(i, slot, lambda cp: cp.wait())

    pos = comb_ref[:, POS_LANE:POS_LANE + 1].astype(jnp.int32)
    wcol = lax.broadcasted_iota(jnp.int32, (TOK_TILE, TILE_ROWS), 1)
    perm = jnp.where(wcol == pos, 1.0, 0.0).astype(BF16)
    y = jnp.dot(perm, yw_ref[slot], preferred_element_type=F32)
    out = x_ref[...] + gate_ref[...] * y
    if emit_next:
        out_refs[1][...] = _modnorm(out, ng_ref[...], nsc_ref[...], nsh_ref[...]).astype(BF16)
    if not split_out:
        out_refs[0][...] = out
    else:
        @pl.when(i < N_CTX // TOK_TILE)
        def _():
            out_refs[0][...] = out

        @pl.when(i >= N_CTX // TOK_TILE)
        def _():
            out_refs[1][...] = out


def _combine(x, mod, layer, comb, ys, runs, next_norm_g):
    n_ctx_tiles = N_CTX // TOK_TILE
    split_out = next_norm_g is None
    tile_spec = pl.BlockSpec((TOK_TILE, D), lambda i, *_: (i, 0))
    in_specs = [
        tile_spec,
        _mod_spec(layer, 5, TOK_TILE, row_axis=0),
        pl.BlockSpec((TOK_TILE, ROUTER_LANES), lambda i, *_: (i, 0)),
        pl.BlockSpec(memory_space=pl.ANY),
    ]
    args = [x, mod, comb, ys]
    if split_out:
        out_specs = [pl.BlockSpec((TOK_TILE, D), lambda i, *_: (jnp.minimum(i, n_ctx_tiles - 1), 0)),
                     pl.BlockSpec((TOK_TILE, D), lambda i, *_: (jnp.maximum(i - n_ctx_tiles, 0), 0))]
        out_shape = [jax.ShapeDtypeStruct((N_CTX, D), F32), jax.ShapeDtypeStruct((N_LAT, D), F32)]
    else:
        in_specs += [pl.BlockSpec((1, D), lambda i, *_: (0, 0)),
                     _mod_spec(layer + 1, 1, TOK_TILE, row_axis=0),
                     _mod_spec(layer + 1, 0, TOK_TILE, row_axis=0)]
        args += [next_norm_g.reshape(1, D), mod, mod]
        out_specs = [tile_spec, tile_spec]
        out_shape = [jax.ShapeDtypeStruct((N_TOK, D), F32), jax.ShapeDtypeStruct((N_TOK, D), BF16)]
    grid_spec = pltpu.PrefetchScalarGridSpec(
        num_scalar_prefetch=3,
        grid=(N_TOK_TILES,),
        in_specs=in_specs,
        out_specs=out_specs,
        scratch_shapes=[
            pltpu.VMEM((2, TILE_ROWS, D), BF16),
            pltpu.SemaphoreType.DMA((2, N_GROUPS, len(RUN_BITS))),
        ],
    )
    return pl.pallas_call(
        functools.partial(_combine_kernel, split_out=split_out, emit_next=not split_out),
        grid_spec=grid_spec,
        out_shape=out_shape,
        compiler_params=_params("arbitrary"),
        name="moe_combine",
    )(*runs, *args)


def _moe_sublayer(x, mod, layer, norm_g, rgw, rgb, rew, reb, w1, w3, w2, next_norm_g):
    rw, rb = _router_weights(rgw, rgb, rew, reb)
    hc, cc, comb, cnt = _router(x, mod, layer, norm_g, rw, rb)
    runs, end, plan = _moe_plan(cnt[:, 0, :N_GROUPS].astype(jnp.int32))
    hs, cs = _dispatch(hc, cc, runs, end)
    ys = _moe_experts(hs, cs, w1, w3, w2, layer, plan)
    return _combine(x, mod, layer, comb, ys, runs, next_norm_g)


def kernel(x_prompt, x_sample, cache_k, cache_v, c, c_ctx, ada_w, ada_b, norm_mix_g, norm_ffn_g,
           mix_in_w, pool_w, pool_scale, conv_w, conv_b, conv_ln_g, conv_ln_b, mix_out_w,
           qkv_w, q_norm_g, k_norm_g, rpb, attn_out_w, router_g_w, router_g_b, router_e_w,
           router_e_b, exp_w1, exp_w3, exp_w2):
    x = (x_prompt.reshape(N_CTX, D), x_sample.reshape(N_LAT, D))
    cvec8 = jnp.concatenate([c_ctx[None, :], c, jnp.zeros((8 - 1 - N_LAT_SEQ, D), F32)], axis=0)
    mod = _adaln_all(cvec8, ada_w, ada_b)
    n_attn = DEPTH // 2
    cache_k2 = cache_k.reshape(N_LAT_SEQ, n_attn, CTX_SEQ, D)
    cache_v2 = cache_v.reshape(N_LAT_SEQ, n_attn, CTX_SEQ, D)

    caches = None
    h = None
    for l in range(DEPTH):
        e = l // 2
        if l % 2 == 0:
            u = _mm_prologue(x, mod, l, norm_mix_g[l], mix_in_w, e, h=h)
            cat = _seq_mixer(u, pool_w[e].astype(BF16), pool_scale[e], conv_w[e], conv_b[e],
                             conv_ln_g[e], conv_ln_b[e])
            x = _mm_residual(cat, mix_out_w, e, x, mod, l)
        else:
            q, k, v, new_k, new_v = _qkv_proj(h, e, qkv_w, q_norm_g[e], k_norm_g[e], caches)
            caches = (new_k, new_v)
            o = _ctx_attention(q, k, v)
            o = _na_attention(q, k, v, cache_k2[:, e], cache_v2[:, e], _na_bias_table(rpb[e]), o)
            x = _mm_residual(o, attn_out_w, e, x, mod, l)
        last = l == DEPTH - 1
        x, h = _moe_sublayer(x, mod, l, norm_ffn_g[l], router_g_w[l], router_g_b[l], router_e_w[l],
                             router_e_b[l], exp_w1, exp_w3, exp_w2,
                             next_norm_g=None if last else norm_mix_g[l + 1])

    y_prompt = x.reshape(N_CTX_SEQ, CTX_SEQ, D)
    y_sample = h.reshape(N_LAT_SEQ, LAT_SEQ, D)
    cache_dims = (N_CTX_SEQ, n_attn, CTX_SEQ, N_HEADS, HEAD_DIM)
    return (y_prompt, y_sample, caches[0].reshape(cache_dims), caches[1].reshape(cache_dims))
```

```python
import functools

import numpy as np
import jax
import jax.numpy as jnp
from jax import lax
from jax.experimental import pallas as pl
from jax.experimental.pallas import tpu as pltpu

D = 2048
N_CTX_SEQ = 16
CTX_SEQ = 256
N_LAT_SEQ = 2
LAT_SEQ = 1024
N_CTX = N_CTX_SEQ * CTX_SEQ
N_LAT = N_LAT_SEQ * LAT_SEQ
N_TOK = N_CTX + N_LAT
DEPTH = 4
GRID_W = 64
GRID_H = LAT_SEQ // GRID_W
N_HEADS = 16
HEAD_DIM = 128
ATTN_SCALE = HEAD_DIM ** -0.5
D_POOL = 1024
D_CONV = 1024
POOL_WINDOWS = (2, 4, 8, 16)
POOL_GC = 256
CONV_W = 31
NA_KH = 8
NA_KW = 16
N_GROUPS = 4
EXP_PER_GROUP = 4
N_EXPERTS = 16
D_EXPERT = 512
EPS = 1e-6
NEG = -1e30

SEQ_TILE = 256
HALO = 16
ROUTER_LANES = 128
EXPERT_LANE0 = N_GROUPS
POS_LANE = EXPERT_LANE0 + N_EXPERTS
LOW_LANE0 = 32
VMEM_LIMIT = 56 * 1024 * 1024

TOK_TILE = 256
N_TOK_TILES = N_TOK // TOK_TILE
ROW_ALIGN = 16
RUN_BITS = (256, 128, 64, 32, 16)
TILE_ROWS = TOK_TILE + N_GROUPS * ROW_ALIGN
MOE_TILE = 1024
MOE_SUB = 256
FF_CHUNK = 512
OUT_CHUNK = 512
N_FF_CHUNKS = EXP_PER_GROUP * D_EXPERT // FF_CHUNK
N_OUT_CHUNKS = D // OUT_CHUNK
ZERO_ROWS = MOE_SUB
MAX_PADDED = N_TOK + N_TOK_TILES * N_GROUPS * (ROW_ALIGN - 1)
MOE_TILES_MAX = (MAX_PADDED + N_GROUPS * (MOE_TILE - 1)) // MOE_TILE
ROW_CAP = -(-(MAX_PADDED + N_GROUPS * (ZERO_ROWS + MOE_TILE - 1)) // MOE_TILE) * MOE_TILE

F32 = jnp.float32
BF16 = jnp.bfloat16


def _cond_of_row(row):
    row = jnp.minimum(row, N_TOK - 1)
    return jnp.where(row < N_CTX, 0, 1 + (row - N_CTX) // LAT_SEQ)


def _mod_spec(layer, which, tm, tn=D, row_axis=1, col_axis=None):
    def index_map(*ids):
        cond = _cond_of_row(ids[row_axis] * tm)
        col = 0 if col_axis is None else ids[col_axis]
        return (layer, cond, which, 0, col)
    return pl.BlockSpec((None, None, None, 1, tn), index_map)


def _silu(x):
    return x / (1.0 + jnp.exp(-x))


def _modnorm(x, g, sc, sh):
    ms = jnp.mean(x * x, axis=-1, keepdims=True)
    y = x * lax.rsqrt(ms + EPS) * g
    return y * (1.0 + sc) + sh


def _params(*sem):
    return pltpu.CompilerParams(dimension_semantics=sem, vmem_limit_bytes=VMEM_LIMIT)


def _ada_kernel(c_ref, w_ref, b_ref, o_ref):
    s = _silu(c_ref[...]).astype(BF16)
    o_ref[...] = jnp.dot(s, w_ref[...].astype(BF16), preferred_element_type=F32) + b_ref[...]


def _adaln_all(cvec8, ada_w, ada_b):
    tn = 1024
    n_out = 6 * D
    out = pl.pallas_call(
        _ada_kernel,
        grid=(DEPTH, n_out // tn),
        in_specs=[
            pl.BlockSpec((8, D), lambda l, j: (0, 0)),
            pl.BlockSpec((None, D, tn), lambda l, j: (l, 0, j)),
            pl.BlockSpec((None, 1, tn), lambda l, j: (l, 0, j)),
        ],
        out_specs=pl.BlockSpec((None, 8, tn), lambda l, j: (l, 0, j)),
        out_shape=jax.ShapeDtypeStruct((DEPTH, 8, n_out), F32),
        compiler_params=_params("arbitrary", "arbitrary"),
        name="adaln",
    )(cvec8, ada_w, ada_b.reshape(DEPTH, 1, n_out))
    return out[:, :3].reshape(DEPTH, 3, 6, 1, D)


def _token_specs(x, tm, tn, idx):
    if not isinstance(x, tuple):
        return [pl.BlockSpec((tm, tn), lambda j, i: idx(i, j))], [x]
    n_ctx_tiles = N_CTX // tm
    ctx_map = lambda j, i: idx(jnp.minimum(i, n_ctx_tiles - 1), j)
    lat_map = lambda j, i: idx(jnp.maximum(i - n_ctx_tiles, 0), j)
    return [pl.BlockSpec((tm, tn), ctx_map), pl.BlockSpec((tm, tn), lat_map)], list(x)


def _token_rows(refs, i, tm):
    if len(refs) == 1:
        return refs[0][...]
    return jnp.where(i < N_CTX // tm, refs[0][...], refs[1][...])


def _cast_weights_once(i, pairs):
    @pl.when(i == 0)
    def _():
        for w_ref, wb_ref in pairs:
            wb_ref[...] = w_ref[...].astype(BF16)


def _mm_pro_kernel(*refs, n_x, tm):
    x_refs = refs[:n_x]
    g_ref, sc_ref, sh_ref, w_ref, o_ref, wb_ref = refs[n_x:]
    i = pl.program_id(1)
    _cast_weights_once(i, [(w_ref, wb_ref)])
    h = _modnorm(_token_rows(x_refs, i, tm), g_ref[...], sc_ref[...], sh_ref[...]).astype(BF16)
    o_ref[...] = jnp.dot(h, wb_ref[...], preferred_element_type=F32)


def _mm_plain_kernel(h_ref, w_ref, o_ref, wb_ref):
    _cast_weights_once(pl.program_id(1), [(w_ref, wb_ref)])
    o_ref[...] = jnp.dot(h_ref[...], wb_ref[...], preferred_element_type=F32)


def _mm_prologue(x, mod, layer, norm_g, w, w_idx, h=None, tm=512, tn=1024):
    f = w.shape[2]
    w_spec = pl.BlockSpec((None, D, tn), lambda j, i: (w_idx, 0, j))
    if h is None:
        x_specs, x_args = _token_specs(x, tm, D, lambda i, j: (i, 0))
        body = functools.partial(_mm_pro_kernel, n_x=len(x_args), tm=tm)
        in_specs = x_specs + [pl.BlockSpec((1, D), lambda j, i: (0, 0)),
                              _mod_spec(layer, 1, tm), _mod_spec(layer, 0, tm), w_spec]
        args = x_args + [norm_g.reshape(1, D), mod, mod, w]
    else:
        body = _mm_plain_kernel
        in_specs = [pl.BlockSpec((tm, D), lambda j, i: (i, 0)), w_spec]
        args = [h, w]
    return pl.pallas_call(
        body,
        grid=(f // tn, N_TOK // tm),
        in_specs=in_specs,
        out_specs=pl.BlockSpec((tm, tn), lambda j, i: (i, j)),
        out_shape=jax.ShapeDtypeStruct((N_TOK, f), F32),
        scratch_shapes=[pltpu.VMEM((D, tn), BF16)],
        compiler_params=_params("arbitrary", "arbitrary"),
        name="mm_prologue",
    )(*args)


def _qkv_kernel(h_ref, wq_ref, wk_ref, wv_ref, qg_ref, kg_ref, *rest, tn, n_ctx_tiles, seqs_per_tile):
    q_ref, k_ref, v_ref, ck_ref, cv_ref, wqb_ref, wkb_ref, wvb_ref = rest[-8:]
    i = pl.program_id(1)
    _cast_weights_once(i, [(wq_ref, wqb_ref), (wk_ref, wkb_ref), (wv_ref, wvb_ref)])
    h = h_ref[...]
    q = jnp.dot(h, wqb_ref[...], preferred_element_type=F32)
    k = jnp.dot(h, wkb_ref[...], preferred_element_type=F32)
    v = jnp.dot(h, wvb_ref[...], preferred_element_type=F32)
    is_ctx = i < n_ctx_tiles
    for hh in range(tn // HEAD_DIM):
        ls = slice(hh * HEAD_DIM, (hh + 1) * HEAD_DIM)
        qh = q[:, ls]
        kh = k[:, ls]
        qn = qh * lax.rsqrt(jnp.mean(qh * qh, axis=-1, keepdims=True) + EPS) * qg_ref[:, ls]
        kn = kh * lax.rsqrt(jnp.mean(kh * kh, axis=-1, keepdims=True) + EPS) * kg_ref[:, ls]
        q_ref[:, ls] = qn.astype(BF16)
        k_ref[:, ls] = kn.astype(BF16)

        @pl.when(is_ctx)
        def _():
            ck_ref[:, :, ls] = kn.reshape(seqs_per_tile, CTX_SEQ, HEAD_DIM)

    v_ref[...] = v.astype(BF16)

    @pl.when(is_ctx)
    def _():
        cv_ref[...] = v.reshape(seqs_per_tile, CTX_SEQ, tn)


def _qkv_proj(h, w_idx, w, q_gain, k_gain, caches, tm=512, tn=512):
    n = h.shape[0]
    n_attn = DEPTH // 2
    ncol = D // tn
    seqs_per_tile = tm // CTX_SEQ
    n_ctx_tiles = N_CTX // tm
    cache_shape = jax.ShapeDtypeStruct((N_CTX_SEQ, n_attn, CTX_SEQ, D), F32)
    cache_spec = pl.BlockSpec(
        (seqs_per_tile, None, CTX_SEQ, tn),
        lambda j, i: (jnp.minimum(i, n_ctx_tiles - 1), w_idx, 0, j))
    in_specs = [
        pl.BlockSpec((tm, D), lambda j, i: (i, 0)),
        pl.BlockSpec((None, D, tn), lambda j, i: (w_idx, 0, j)),
        pl.BlockSpec((None, D, tn), lambda j, i: (w_idx, 0, ncol + j)),
        pl.BlockSpec((None, D, tn), lambda j, i: (w_idx, 0, 2 * ncol + j)),
        pl.BlockSpec((1, tn), lambda j, i: (0, j)),
        pl.BlockSpec((1, tn), lambda j, i: (0, j)),
    ]
    args = [h, w, w, w,
            jnp.tile(q_gain, N_HEADS).reshape(1, D), jnp.tile(k_gain, N_HEADS).reshape(1, D)]
    aliases = {}
    if caches is not None:
        in_specs += [pl.BlockSpec(memory_space=pl.ANY), pl.BlockSpec(memory_space=pl.ANY)]
        aliases = {len(args): 3, len(args) + 1: 4}
        args += list(caches)
    act_spec = pl.BlockSpec((tm, tn), lambda j, i: (i, j))
    act_shape = jax.ShapeDtypeStruct((n, D), BF16)
    return pl.pallas_call(
        functools.partial(_qkv_kernel, tn=tn, n_ctx_tiles=n_ctx_tiles, seqs_per_tile=seqs_per_tile),
        grid=(ncol, n // tm),
        in_specs=in_specs,
        out_specs=[act_spec, act_spec, act_spec, cache_spec, cache_spec],
        out_shape=[act_shape, act_shape, act_shape, cache_shape, cache_shape],
        input_output_aliases=aliases,
        scratch_shapes=[pltpu.VMEM((D, tn), BF16)] * 3,
        compiler_params=_params("arbitrary", "arbitrary"),
        name="qkv_proj",
    )(*args)


def _seq_kernel(u_ref, top_ref, bot_ref, pw_ref, ps_ref, cw_ref, cb_ref, lg_ref, lb_ref,
                o_ref, zpad_ref, conv_ref):
    i = pl.program_id(0)
    n_ctx_tiles = N_CTX // SEQ_TILE
    tiles_per_lat = LAT_SEQ // SEQ_TILE
    is_lat = i >= n_ctx_tiles
    chunk = jnp.where(is_lat, (i - n_ctx_tiles) % tiles_per_lat, 0)
    top_ok = jnp.logical_and(is_lat, chunk > 0)
    bot_ok = jnp.logical_and(is_lat, chunk < tiles_per_lat - 1)
    seq_len = jnp.where(is_lat, LAT_SEQ, CTX_SEQ)
    t = chunk * SEQ_TILE + lax.broadcasted_iota(jnp.int32, (SEQ_TILE, 1), 0)
    rows = SEQ_TILE + 2 * HALO

    for g, win in enumerate(POOL_WINDOWS):
        ls = slice(g * POOL_GC, (g + 1) * POOL_GC)
        mid = u_ref[:, ls]
        top = jnp.where(top_ok, top_ref[:, ls], 0.0)
        bot = jnp.where(bot_ok, bot_ref[:, ls], 0.0)
        up = jnp.concatenate([top, mid, bot], axis=0)
        s = pltpu.roll(up, 1, 0) + up
        if win >= 4:
            s = pltpu.roll(s, 1, 0) + pltpu.roll(s, rows - 1, 0)
        if win >= 8:
            s = pltpu.roll(s, 2, 0) + pltpu.roll(s, rows - 2, 0)
        if win >= 16:
            s = pltpu.roll(s, 4, 0) + pltpu.roll(s, rows - 4, 0)
        s = s[HALO:HALO + SEQ_TILE]
        lo = jnp.maximum(t - win // 2, 0)
        hi = jnp.minimum(t + (win - win // 2) - 1, seq_len - 1)
        cnt = (hi - lo + 1).astype(F32)
        p = (s / cnt - mid).astype(BF16)
        y = jnp.dot(p, pw_ref[g], preferred_element_type=F32) * ps_ref[:, ls]
        o_ref[:, ls] = y.astype(BF16)

    def glu(ref):
        return ref[:, D_POOL:D_POOL + D_CONV] * (1.0 / (1.0 + jnp.exp(-ref[:, D_POOL + D_CONV:])))

    zpad_ref[0:HALO, :] = jnp.where(top_ok, glu(top_ref), 0.0)
    zpad_ref[HALO:HALO + SEQ_TILE, :] = glu(u_ref)
    zpad_ref[HALO + SEQ_TILE:rows, :] = jnp.where(bot_ok, glu(bot_ref), 0.0)
    for lt in range(D_CONV // 128):
        ls = slice(lt * 128, (lt + 1) * 128)
        zp = zpad_ref[:, ls]
        acc = jnp.zeros((SEQ_TILE, 128), F32)
        for b in range(8):
            sb = zp if b == 0 else pltpu.roll(zp, rows - b, 0)
            for a in range(4):
                off = 8 * a + b
                if 1 <= off <= CONV_W:
                    acc = acc + cw_ref[off - 1:off, ls] * sb[8 * a:8 * a + SEQ_TILE]
        conv_ref[:, ls] = acc + cb_ref[:, ls]
    zc = conv_ref[...]
    mu = jnp.mean(zc, axis=-1, keepdims=True)
    d = zc - mu
    var = jnp.mean(d * d, axis=-1, keepdims=True)
    zn = d * lax.rsqrt(var + EPS) * lg_ref[...] + lb_ref[...]
    o_ref[:, D_POOL:] = _silu(zn).astype(BF16)


def _seq_mixer(u, pool_w_bf16, pool_scale, conv_w, conv_b, ln_g, ln_b):
    n = u.shape[0]
    f_in = u.shape[1]
    n_tiles = n // SEQ_TILE
    hb = SEQ_TILE // HALO
    n_hblocks = n // HALO
    return pl.pallas_call(
        _seq_kernel,
        grid=(n_tiles,),
        in_specs=[
            pl.BlockSpec((SEQ_TILE, f_in), lambda i: (i, 0)),
            pl.BlockSpec((HALO, f_in), lambda i: (jnp.maximum(i * hb - 1, 0), 0)),
            pl.BlockSpec((HALO, f_in), lambda i: (jnp.minimum((i + 1) * hb, n_hblocks - 1), 0)),
            pl.BlockSpec((len(POOL_WINDOWS), POOL_GC, POOL_GC), lambda i: (0, 0, 0)),
            pl.BlockSpec((1, D_POOL), lambda i: (0, 0)),
            pl.BlockSpec((CONV_W, D_CONV), lambda i: (0, 0)),
            pl.BlockSpec((1, D_CONV), lambda i: (0, 0)),
            pl.BlockSpec((1, D_CONV), lambda i: (0, 0)),
            pl.BlockSpec((1, D_CONV), lambda i: (0, 0)),
        ],
        out_specs=pl.BlockSpec((SEQ_TILE, D_POOL + D_CONV), lambda i: (i, 0)),
        out_shape=jax.ShapeDtypeStruct((n, D_POOL + D_CONV), BF16),
        scratch_shapes=[
            pltpu.VMEM((SEQ_TILE + 2 * HALO, D_CONV), F32),
            pltpu.VMEM((SEQ_TILE, D_CONV), F32),
        ],
        compiler_params=_params("arbitrary"),
        name="seq_mixer",
    )(u, u, u, pool_w_bf16, pool_scale.reshape(1, D_POOL), conv_w, conv_b.reshape(1, D_CONV),
      ln_g.reshape(1, D_CONV), ln_b.reshape(1, D_CONV))


_NT = (((1,), (1,)), ((), ()))


def _ctx_attn_kernel(q_ref, k_ref, v_ref, o_ref, s_ref, p_ref):
    for h in range(N_HEADS):
        ls = slice(h * HEAD_DIM, (h + 1) * HEAD_DIM)
        s_ref[h] = lax.dot_general(q_ref[:, ls], k_ref[:, ls], _NT, preferred_element_type=F32) * ATTN_SCALE
    for h in range(N_HEADS):
        s = s_ref[h]
        e = jnp.exp(s - jnp.max(s, axis=-1, keepdims=True))
        p_ref[h] = (e * (1.0 / jnp.sum(e, axis=-1, keepdims=True))).astype(BF16)
    for h in range(N_HEADS):
        ls = slice(h * HEAD_DIM, (h + 1) * HEAD_DIM)
        o_ref[:, ls] = jnp.dot(p_ref[h], v_ref[:, ls], preferred_element_type=F32).astype(BF16)


def _ctx_attention(q, k, v):
    n = q.shape[0]
    spec = pl.BlockSpec((CTX_SEQ, D), lambda b: (b, 0))
    return pl.pallas_call(
        _ctx_attn_kernel,
        grid=(N_CTX_SEQ,),
        in_specs=[spec, spec, spec],
        out_specs=spec,
        out_shape=jax.ShapeDtypeStruct((n, D), BF16),
        scratch_shapes=[pltpu.VMEM((N_HEADS, CTX_SEQ, CTX_SEQ), F32),
                        pltpu.VMEM((N_HEADS, CTX_SEQ, CTX_SEQ), BF16)],
        compiler_params=_params("arbitrary"),
        name="ctx_attention",
    )(q, k, v)


def _na_window_mask():
    cq = np.arange(GRID_W)[:, None]
    kc = np.arange(GRID_W)[None, :]
    win0 = np.clip(cq - NA_KW // 2, 0, GRID_W - NA_KW)
    mask = ((kc >= win0) & (kc < win0 + NA_KW)).astype(np.float32)
    return np.tile(mask, (1, NA_KH))


def _na_bias_rows(rpb_e):
    centre = NA_KW - 1
    pad = jnp.zeros(rpb_e.shape[:2] + (128 - (2 * NA_KW - 1),), F32)
    return jnp.concatenate([rpb_e[..., centre:], pad, rpb_e[..., :centre]], axis=-1)


def _na_key_row0(r):
    return min(max(r - NA_KH // 2, 0), GRID_H - NA_KH)


def _na_segments():
    segs = []
    for r in range(GRID_H):
        kr0 = _na_key_row0(r)
        if segs and segs[-1][2] == kr0:
            segs[-1][1] = r + 1
        else:
            segs.append([r, r + 1, kr0])
    return segs


def _na_kernel(q_ref, k_ref, v_ref, kc_ref, vc_ref, rows_ref, mask_ref, o_in_ref, o_ref,
               s_ref, p_ref, acc_ref, tz_ref):
    del o_in_ref
    band = NA_KH * GRID_W
    n_ctx = CTX_SEQ
    kc = kc_ref[...].astype(BF16)
    vc = vc_ref[...].astype(BF16)
    @pl.when(pl.program_id(1) == 0)
    def _():
        n_dr = 2 * NA_KH - 1
        toeplitz = [pltpu.roll(jnp.broadcast_to(rows_ref[d:d + 1, :], (GRID_W, 2 * GRID_W)), 0, 1,
                               stride=1, stride_axis=0) for d in range(n_dr)]
        left = lax.broadcasted_iota(jnp.int32, (GRID_W, 2 * GRID_W), 1) < GRID_W
        for d in range(n_dr - 1):
            tz_ref[d] = jnp.where(left, toeplitz[d], pltpu.roll(toeplitz[d + 1], GRID_W, 1))
    for r0, r1, kr0 in _na_segments():
        rs = slice(r0 * GRID_W, r1 * GRID_W)
        kb = k_ref[kr0 * GRID_W:kr0 * GRID_W + band, :]
        s_ref[rs, 0:band] = lax.dot_general(q_ref[rs, :], kb, _NT, preferred_element_type=F32) * ATTN_SCALE
    s_ref[:, band:band + n_ctx] = lax.dot_general(q_ref[...], kc, _NT, preferred_element_type=F32) * ATTN_SCALE
    mask = mask_ref[...] > 0.5
    for r in range(GRID_H):
        rs = slice(r * GRID_W, (r + 1) * GRID_W)
        kr0 = _na_key_row0(r)
        bias = jnp.concatenate(
            [tz_ref[kr0 + j - r + NA_KH - 1] for j in range(0, NA_KH, 2)], axis=-1)
        s_loc = jnp.where(mask, s_ref[rs, 0:band] + bias, NEG)
        s_ctx = s_ref[rs, band:band + n_ctx]
        m = jnp.maximum(jnp.max(s_loc, axis=-1, keepdims=True), jnp.max(s_ctx, axis=-1, keepdims=True))
        e_loc = jnp.exp(s_loc - m)
        e_ctx = jnp.exp(s_ctx - m)
        inv = 1.0 / (jnp.sum(e_loc, axis=-1, keepdims=True) + jnp.sum(e_ctx, axis=-1, keepdims=True))
        p_ref[rs, 0:band] = (e_loc * inv).astype(BF16)
        p_ref[rs, band:band + n_ctx] = (e_ctx * inv).astype(BF16)
    for r0, r1, kr0 in _na_segments():
        rs = slice(r0 * GRID_W, r1 * GRID_W)
        vb = v_ref[kr0 * GRID_W:kr0 * GRID_W + band, :]
        acc_ref[rs, :] = jnp.dot(p_ref[rs, 0:band], vb, preferred_element_type=F32)
    o_ctx = jnp.dot(p_ref[:, band:band + n_ctx], vc, preferred_element_type=F32)
    o_ref[...] = (acc_ref[...] + o_ctx).astype(BF16)


def _na_attention(q, k, v, cache_k_e, cache_v_e, rpb_e, o_ctx):
    mask = _na_window_mask()
    lat0 = N_CTX // LAT_SEQ
    n_dr = 2 * NA_KH - 1
    n_keys = NA_KH * GRID_W + CTX_SEQ
    tok_spec = pl.BlockSpec((LAT_SEQ, HEAD_DIM), lambda h, b: (lat0 + b, h))
    cache_spec = pl.BlockSpec((None, CTX_SEQ, HEAD_DIM), lambda h, b: (b, 0, h))
    return pl.pallas_call(
        _na_kernel,
        grid=(N_HEADS, N_LAT_SEQ),
        in_specs=[
            tok_spec, tok_spec, tok_spec, cache_spec, cache_spec,
            pl.BlockSpec((None, n_dr, 2 * GRID_W), lambda h, b: (h, 0, 0)),
            pl.BlockSpec((GRID_W, NA_KH * GRID_W), lambda h, b: (0, 0)),
            pl.BlockSpec(memory_space=pl.ANY),
        ],
        out_specs=tok_spec,
        out_shape=jax.ShapeDtypeStruct(o_ctx.shape, o_ctx.dtype),
        input_output_aliases={7: 0},
        scratch_shapes=[pltpu.VMEM((LAT_SEQ, n_keys), F32),
                        pltpu.VMEM((LAT_SEQ, n_keys), BF16),
                        pltpu.VMEM((LAT_SEQ, HEAD_DIM), F32),
                        pltpu.VMEM((n_dr - 1, GRID_W, 2 * GRID_W), F32)],
        compiler_params=_params("arbitrary", "arbitrary"),
        name="na_attention",
    )(q, k, v, cache_k_e, cache_v_e, _na_bias_rows(rpb_e), jnp.asarray(mask), o_ctx)


def _split_bf16(x, pieces):
    out = []
    for _ in range(pieces):
        p = x.astype(BF16)
        out.append(p)
        x = x - p.astype(F32)
    return out


def _moe_prenorm(x, g_ref, sc_ref, sh_ref):
    h = _modnorm(x, g_ref[...], sc_ref[...], sh_ref[...])
    return jnp.concatenate(_split_bf16(h, 2), axis=1)


def _router_logits(h_pieces, rw_ref, rb_ref):
    prod = jnp.dot(h_pieces, rw_ref[...], preferred_element_type=F32)
    return prod + pltpu.roll(prod, ROUTER_LANES - LOW_LANE0, 1) + rb_ref[...]


def _route_sort(h_hi, logits, hc_ref, cc_ref, comb_ref, cnt_ref):
    lane = lax.broadcasted_iota(jnp.int32, logits.shape, 1)
    big = jnp.int32(1 << 20)
    neg_inf = -jnp.inf

    def first_argmax(vals, valid):
        v = jnp.where(valid, vals, neg_inf)
        m = jnp.max(v, axis=-1, keepdims=True)
        idx = jnp.min(jnp.where(jnp.logical_and(valid, v == m), lane, big), axis=-1, keepdims=True)
        return m, idx

    is_g = lane < N_GROUPS
    mg, gi = first_argmax(logits, is_g)
    pg = 1.0 / jnp.sum(jnp.where(is_g, jnp.exp(logits - mg), 0.0), axis=-1, keepdims=True)
    e_lane0 = EXPERT_LANE0 + gi * EXP_PER_GROUP
    is_e = jnp.logical_and(lane >= e_lane0, lane < e_lane0 + EXP_PER_GROUP)
    m1, i1 = first_argmax(logits, is_e)
    m2, i2 = first_argmax(logits, jnp.logical_and(is_e, lane != i1))
    e2 = jnp.exp(m2 - m1)
    den = 1.0 + e2
    w1 = (1.0 / den) * pg
    w2 = (e2 / den) * pg
    onehot_g = jnp.where(lane == gi, 1.0, 0.0)

    row = lax.broadcasted_iota(jnp.int32, (TOK_TILE, TOK_TILE), 0)
    col = lax.broadcasted_iota(jnp.int32, (TOK_TILE, TOK_TILE), 1)
    earlier = jnp.where(col < row, 1.0, 0.0).astype(BF16)
    rank = jnp.dot(earlier, onehot_g.astype(BF16), preferred_element_type=F32)
    cnt = jnp.sum(onehot_g, axis=0, keepdims=True)
    padded = jnp.floor((cnt + (ROW_ALIGN - 1)) * (1.0 / ROW_ALIGN)) * ROW_ALIGN
    padded8 = jnp.broadcast_to(padded, (8, ROUTER_LANES))
    run0 = (pltpu.roll(padded8, 1, 1) + pltpu.roll(padded8, 2, 1) + pltpu.roll(padded8, 3, 1))[0:1, :]
    pos = jnp.sum(onehot_g * (rank + run0), axis=-1, keepdims=True)

    comb = jnp.where(lane == i1, w1, jnp.where(lane == i2, w2, jnp.where(lane == POS_LANE, pos, onehot_g)))
    comb_ref[...] = comb
    cnt_ref[...] = cnt

    pos_t = comb.T[POS_LANE:POS_LANE + 1, :]
    dest = lax.broadcasted_iota(jnp.int32, (TILE_ROWS, TOK_TILE), 0).astype(F32)
    perm = jnp.where(pos_t == dest, 1.0, 0.0).astype(BF16)
    hc_ref[...] = jnp.dot(perm, h_hi, preferred_element_type=F32).astype(BF16)
    c_parts = _split_bf16(comb, 3)
    cc_ref[...] = (jnp.dot(perm, c_parts[0], preferred_element_type=F32)
                   + jnp.dot(perm, c_parts[1], preferred_element_type=F32)
                   + jnp.dot(perm, c_parts[2], preferred_element_type=F32))


W_STAGE_COLS = 512


def _out_route_kernel(a_ref, w_hbm, gate_ref, g_ref, sc_ref, sh_ref, rw_ref, rb_ref, *refs, n_x, w_idx):
    x_refs = refs[:n_x]
    xo_ref, hc_ref, cc_ref, comb_ref, cnt_ref, wb_ref, stage_ref, prev_ref, sem = refs[n_x:]
    i = pl.program_id(0)

    @pl.when(i == 0)
    def _():
        prev_ref[...] = jnp.zeros_like(prev_ref)
        n_chunks = D // W_STAGE_COLS

        def chunk_copy(c):
            cols = pl.ds(c * W_STAGE_COLS, W_STAGE_COLS)
            return pltpu.make_async_copy(w_hbm.at[w_idx, :, cols], stage_ref.at[c % 2], sem.at[c % 2])

        chunk_copy(0).start()
        chunk_copy(1).start()
        for c in range(n_chunks):
            chunk_copy(c).wait()
            wb_ref[:, c * W_STAGE_COLS:(c + 1) * W_STAGE_COLS] = stage_ref[c % 2].astype(BF16)
            if c + 2 < n_chunks:
                chunk_copy(c + 2).start()

    h_prev = prev_ref[...]
    logits = _router_logits(h_prev, rw_ref, rb_ref)
    y = jnp.dot(a_ref[...], wb_ref[...], preferred_element_type=F32)
    x_mid = _token_rows(x_refs, jnp.minimum(i, N_TOK_TILES - 1), TOK_TILE) + gate_ref[...] * y
    xo_ref[...] = x_mid
    _route_sort(h_prev[:, :D], logits, hc_ref, cc_ref, comb_ref, cnt_ref)
    prev_ref[...] = _moe_prenorm(x_mid, g_ref, sc_ref, sh_ref)


def _out_proj_and_route(a_bf16, w, w_idx, x, mod, layer, norm_g, rw, rb):
    tm = TOK_TILE
    k = a_bf16.shape[1]
    last = N_TOK_TILES - 1
    cur = lambda i: jnp.minimum(i, last)
    prev = lambda i: jnp.maximum(i - 1, 0)
    if isinstance(x, tuple):
        n_ctx_tiles = N_CTX // tm
        x_specs = [pl.BlockSpec((tm, D), lambda i: (jnp.minimum(i, n_ctx_tiles - 1), 0)),
                   pl.BlockSpec((tm, D), lambda i: (jnp.maximum(cur(i) - n_ctx_tiles, 0), 0))]
        x_args = list(x)
    else:
        x_specs = [pl.BlockSpec((tm, D), lambda i: (cur(i), 0))]
        x_args = [x]
    return pl.pallas_call(
        functools.partial(_out_route_kernel, n_x=len(x_args), w_idx=w_idx),
        grid=(N_TOK_TILES + 1,),
        in_specs=[
            pl.BlockSpec((tm, k), lambda i: (cur(i), 0)),
            pl.BlockSpec(memory_space=pl.ANY),
            _mod_spec(layer, 2, tm, row_axis=0),
            pl.BlockSpec((1, D), lambda i: (0, 0)),
            _mod_spec(layer, 4, tm, row_axis=0),
            _mod_spec(layer, 3, tm, row_axis=0),
            pl.BlockSpec((2 * D, ROUTER_LANES), lambda i: (0, 0)),
            pl.BlockSpec((1, ROUTER_LANES), lambda i: (0, 0)),
        ] + x_specs,
        out_specs=[
            pl.BlockSpec((tm, D), lambda i: (cur(i), 0)),
            pl.BlockSpec((None, TILE_ROWS, D), lambda i: (prev(i), 0, 0)),
            pl.BlockSpec((None, TILE_ROWS, ROUTER_LANES), lambda i: (prev(i), 0, 0)),
            pl.BlockSpec((tm, ROUTER_LANES), lambda i: (prev(i), 0)),
            pl.BlockSpec((None, 1, ROUTER_LANES), lambda i: (prev(i), 0, 0)),
        ],
        out_shape=[
            jax.ShapeDtypeStruct((N_TOK, D), F32),
            jax.ShapeDtypeStruct((N_TOK_TILES, TILE_ROWS, D), BF16),
            jax.ShapeDtypeStruct((N_TOK_TILES, TILE_ROWS, ROUTER_LANES), F32),
            jax.ShapeDtypeStruct((N_TOK, ROUTER_LANES), F32),
            jax.ShapeDtypeStruct((N_TOK_TILES, 1, ROUTER_LANES), F32),
        ],
        scratch_shapes=[
            pltpu.VMEM((k, D), BF16),
            pltpu.VMEM((2, k, W_STAGE_COLS), F32),
            pltpu.VMEM((tm, 2 * D), BF16),
            pltpu.SemaphoreType.DMA((2,)),
        ],
        compiler_params=_params("arbitrary"),
        name="out_proj_route",
    )(a_bf16, w, mod, norm_g.reshape(1, D), mod, mod, rw, rb, *x_args)


def _router_weights(rgw, rgb, rew, reb):
    n_out = N_GROUPS + N_EXPERTS
    rw = jnp.concatenate([rgw, rew.reshape(D, N_EXPERTS)], axis=1)
    hi = rw.astype(BF16)
    lo = (rw - hi.astype(F32)).astype(BF16)
    zeros = lambda n: jnp.zeros((D, n), BF16)
    top = jnp.concatenate([hi, zeros(LOW_LANE0 - n_out), lo, zeros(ROUTER_LANES - LOW_LANE0 - n_out)], axis=1)
    bottom = jnp.concatenate([hi, zeros(ROUTER_LANES - n_out)], axis=1)
    rb = jnp.concatenate([rgb, reb.reshape(N_EXPERTS)])
    rb = jnp.pad(rb, (0, ROUTER_LANES - n_out)).reshape(1, ROUTER_LANES)
    return jnp.concatenate([top, bottom], axis=0), rb


def _moe_plan(cnt):
    i32 = jnp.int32
    padded = (cnt + (ROW_ALIGN - 1)) // ROW_ALIGN * ROW_ALIGN
    run0 = jnp.cumsum(padded, axis=1) - padded
    length = jnp.sum(padded, axis=0)
    span = (length + ZERO_ROWS + MOE_TILE - 1) // MOE_TILE * MOE_TILE
    start = jnp.cumsum(span) - span
    off = start[None, :] + jnp.cumsum(padded, axis=0) - padded
    end = start + length
    need = length
    ntile = (need + MOE_TILE - 1) // MOE_TILE
    cum = jnp.cumsum(ntile)
    total = cum[-1]
    k = jnp.arange(MOE_TILES_MAX, dtype=i32)
    kk = jnp.minimum(k, jnp.maximum(total - 1, 0))
    grp = jnp.minimum(jnp.sum((kk[:, None] >= cum[None, :]).astype(i32), axis=1), N_GROUPS - 1)
    j = kk - (cum - ntile)[grp]
    blk = start[grp] // MOE_TILE + j
    nsub = jnp.clip((need[grp] - j * MOE_TILE + MOE_SUB - 1) // MOE_SUB, 0, MOE_TILE // MOE_SUB)
    nsub = jnp.where(k < total, nsub, 0)
    runs = tuple(a.reshape(-1).astype(i32) for a in (off, run0, padded))
    return runs, end.astype(i32), (blk.astype(i32), grp.astype(i32), nsub.astype(i32))


def _run_pieces(run_refs, tile, g):
    off_ref, run0_ref, padded_ref = run_refs
    idx = tile * N_GROUPS + g
    n = padded_ref[idx]
    src0 = run0_ref[idx]
    dst0 = off_ref[idx]
    pieces = []
    for b, bit in enumerate(RUN_BITS):
        done = n & (-2 * bit)
        pieces.append((b, (n & bit) != 0, pl.multiple_of(src0 + done, ROW_ALIGN),
                       pl.multiple_of(dst0 + done, ROW_ALIGN), bit))
    return pieces


def _dispatch_kernel(off_ref, run0_ref, padded_ref, end_ref, hc_ref, cc_ref, hs_ref, cs_ref,
                     hb_ref, cb_ref, zh_ref, zc_ref, sem, sem_z):
    run_refs = (off_ref, run0_ref, padded_ref)
    i = pl.program_id(0)
    n_steps = pl.num_programs(0)
    slot = i % 2

    def for_each_copy(tile, sl, fn):
        for g in range(N_GROUPS):
            for b, pred, src, dst, rows in _run_pieces(run_refs, tile, g):
                @pl.when(pred)
                def _():
                    fn(pltpu.make_async_copy(hb_ref.at[sl, pl.ds(src, rows), :],
                                             hs_ref.at[pl.ds(dst, rows), :], sem.at[sl, 0, g, b]))
                    fn(pltpu.make_async_copy(cb_ref.at[sl, pl.ds(src, rows), :],
                                             cs_ref.at[pl.ds(dst, rows), :], sem.at[sl, 1, g, b]))

    def zero_copies():
        cps = []
        for g in range(N_GROUPS):
            r0 = pl.multiple_of(end_ref[g], ROW_ALIGN)
            cps.append(pltpu.make_async_copy(zh_ref, hs_ref.at[pl.ds(r0, ZERO_ROWS), :], sem_z.at[0, g]))
            cps.append(pltpu.make_async_copy(zc_ref, cs_ref.at[pl.ds(r0, ZERO_ROWS), :], sem_z.at[1, g]))
        return cps

    @pl.when(i == 0)
    def _():
        zh_ref[...] = jnp.zeros_like(zh_ref)
        zc_ref[...] = jnp.zeros_like(zc_ref)
        for cp in zero_copies():
            cp.start()

    hb_ref[slot] = hc_ref[...]
    cb_ref[slot] = cc_ref[...]
    for_each_copy(i, slot, lambda cp: cp.start())

    @pl.when(i > 0)
    def _():
        for_each_copy(i - 1, 1 - slot, lambda cp: cp.wait())

    @pl.when(i == n_steps - 1)
    def _():
        for_each_copy(i, slot, lambda cp: cp.wait())
        for cp in zero_copies():
            cp.wait()


def _dispatch(hc, cc, runs, end):
    any_spec = pl.BlockSpec(memory_space=pl.ANY)
    grid_spec = pltpu.PrefetchScalarGridSpec(
        num_scalar_prefetch=4,
        grid=(N_TOK_TILES,),
        in_specs=[pl.BlockSpec((None, TILE_ROWS, D), lambda i, *_: (i, 0, 0)),
                  pl.BlockSpec((None, TILE_ROWS, ROUTER_LANES), lambda i, *_: (i, 0, 0))],
        out_specs=[any_spec, any_spec],
        scratch_shapes=[
            pltpu.VMEM((2, TILE_ROWS, D), BF16),
            pltpu.VMEM((2, TILE_ROWS, ROUTER_LANES), F32),
            pltpu.VMEM((ZERO_ROWS, D), BF16),
            pltpu.VMEM((ZERO_ROWS, ROUTER_LANES), F32),
            pltpu.SemaphoreType.DMA((2, 2, N_GROUPS, len(RUN_BITS))),
            pltpu.SemaphoreType.DMA((2, N_GROUPS)),
        ],
    )
    return pl.pallas_call(
        _dispatch_kernel,
        grid_spec=grid_spec,
        out_shape=[jax.ShapeDtypeStruct((ROW_CAP, D), BF16),
                   jax.ShapeDtypeStruct((ROW_CAP, ROUTER_LANES), F32)],
        compiler_params=_params("arbitrary"),
        name="moe_dispatch",
    )(*runs, end, hc, cc)


def _moe_kernel(blk_ref, grp_ref, nsub_ref, h_ref, c_ref, w1_ref, w3_ref, w2_ref, o_ref, hid_ref):
    k = pl.program_id(0)
    c = pl.program_id(1)
    ns = nsub_ref[k]
    n_sub_max = MOE_TILE // MOE_SUB

    def for_row_count(fn):
        for m in range(1, n_sub_max + 1):
            @pl.when(ns == m)
            def _():
                fn(m * MOE_SUB)

    @pl.when(jnp.logical_and(ns > 0, c < N_FF_CHUNKS))
    def _():
        w1 = w1_ref[...].astype(BF16)
        w3 = w3_ref[...].astype(BF16)
        e_lane = EXPERT_LANE0 + grp_ref[k] * EXP_PER_GROUP + c * FF_CHUNK // D_EXPERT

        def up(rows):
            h = h_ref[0:rows, :]
            a = jnp.dot(h, w1, preferred_element_type=F32)
            b = jnp.dot(h, w3, preferred_element_type=F32)
            comb = c_ref[0:rows, :]
            lane = lax.broadcasted_iota(jnp.int32, comb.shape, 1)
            cw = jnp.sum(jnp.where(lane == e_lane, comb, 0.0), axis=-1, keepdims=True)
            hid_ref[c, 0:rows, :] = (_silu(a) * b * cw).astype(BF16)

        for_row_count(up)

    @pl.when(jnp.logical_and(ns > 0, c >= N_FF_CHUNKS))
    def _():
        w2 = w2_ref[...].reshape(EXP_PER_GROUP * D_EXPERT, OUT_CHUNK).astype(BF16)

        def down(rows):
            hid = jnp.concatenate([hid_ref[j, 0:rows, :] for j in range(N_FF_CHUNKS)], axis=1)
            o_ref[0:rows, :] = jnp.dot(hid, w2, preferred_element_type=F32).astype(BF16)
            if rows < MOE_TILE:
                o_ref[rows:, :] = jnp.zeros((MOE_TILE - rows, OUT_CHUNK), BF16)

        for_row_count(down)


def _moe_experts(hs, cs, w1, w3, w2, layer, plan):
    per_e = D_EXPERT // FF_CHUNK
    n_steps = N_FF_CHUNKS + N_OUT_CHUNKS

    def step_of(k, c, nsub_ref):
        return jnp.where(nsub_ref[k] > 0, c, n_steps - 1)

    def w13_map(k, c, blk_ref, grp_ref, nsub_ref):
        cc = jnp.minimum(step_of(k, c, nsub_ref), N_FF_CHUNKS - 1)
        return (layer, grp_ref[k] * EXP_PER_GROUP + cc // per_e, 0, cc % per_e)

    def out_chunk(k, c, nsub_ref):
        return jnp.maximum(step_of(k, c, nsub_ref) - N_FF_CHUNKS, 0)

    w2_map = lambda k, c, blk_ref, grp_ref, nsub_ref: (layer, grp_ref[k], 0, out_chunk(k, c, nsub_ref))
    row_map = lambda k, c, blk_ref, grp_ref, nsub_ref: (blk_ref[k], 0)
    out_map = lambda k, c, blk_ref, grp_ref, nsub_ref: (blk_ref[k], out_chunk(k, c, nsub_ref))
    grid_spec = pltpu.PrefetchScalarGridSpec(
        num_scalar_prefetch=3,
        grid=(MOE_TILES_MAX, n_steps),
        in_specs=[
            pl.BlockSpec((MOE_TILE, D), row_map),
            pl.BlockSpec((MOE_TILE, ROUTER_LANES), row_map),
            pl.BlockSpec((None, None, D, FF_CHUNK), w13_map),
            pl.BlockSpec((None, None, D, FF_CHUNK), w13_map),
            pl.BlockSpec((None, EXP_PER_GROUP, D_EXPERT, OUT_CHUNK), w2_map),
        ],
        out_specs=pl.BlockSpec((MOE_TILE, OUT_CHUNK), out_map),
        scratch_shapes=[pltpu.VMEM((N_FF_CHUNKS, MOE_TILE, FF_CHUNK), BF16)],
    )
    return pl.pallas_call(
        _moe_kernel,
        grid_spec=grid_spec,
        out_shape=jax.ShapeDtypeStruct((ROW_CAP, D), BF16),
        compiler_params=_params("arbitrary", "arbitrary"),
        name="moe_experts",
    )(*plan, hs, cs, w1, w3, w2)


def _combine_kernel(off_ref, run0_ref, padded_ref, x_ref, gate_ref, comb_ref, ys_ref, *refs,
                    split_out, emit_next):
    if emit_next:
        (ng_ref, nsc_ref, nsh_ref), refs = refs[:3], refs[3:]
    out_refs, (yw_ref, sem) = refs[:-2], refs[-2:]
    run_refs = (off_ref, run0_ref, padded_ref)
    i = pl.program_id(0)
    n_steps = pl.num_programs(0)
    slot = i % 2

    def for_each_copy(tile, sl, fn):
        for g in range(N_GROUPS):
            for b, pred, src, dst, rows in _run_pieces(run_refs, tile, g):
                @pl.when(pred)
                def _():
                    fn(pltpu.make_async_copy(ys_ref.at[pl.ds(dst, rows), :],
                                             yw_ref.at[sl, pl.ds(src, rows), :], sem.at[sl, g, b]))

    def fetch(tile, sl):
        yw_ref[sl, TOK_TILE:, :] = jnp.zeros((TILE_ROWS - TOK_TILE, D), BF16)
        for_each_copy(tile, sl, lambda cp: cp.start())

    @pl.when(i == 0)
    def _():
        fetch(i, slot)

    @pl.when(i + 1 < n_steps)
    def _():
        fetch(i + 1, 1 - slot)

    for_each_copy(i, slot, lambda cp: cp.wait())

    pos = comb_ref[:, POS_LANE:POS_LANE + 1].astype(jnp.int32)
    wcol = lax.broadcasted_iota(jnp.int32, (TOK_TILE, TILE_ROWS), 1)
    perm = jnp.where(wcol == pos, 1.0, 0.0).astype(BF16)
    y = jnp.dot(perm, yw_ref[slot], preferred_element_type=F32)
    out = x_ref[...] + gate_ref[...] * y
    if emit_next:
        out_refs[1][...] = _modnorm(out, ng_ref[...], nsc_ref[...], nsh_ref[...]).astype(BF16)
    if not split_out:
        out_refs[0][...] = out
    else:
        @pl.when(i < N_CTX // TOK_TILE)
        def _():
            out_refs[0][...] = out

        @pl.when(i >= N_CTX // TOK_TILE)
        def _():
            out_refs[1][...] = out


def _combine(x, mod, layer, comb, ys, runs, next_norm_g):
    n_ctx_tiles = N_CTX // TOK_TILE
    split_out = next_norm_g is None
    tile_spec = pl.BlockSpec((TOK_TILE, D), lambda i, *_: (i, 0))
    in_specs = [
        tile_spec,
        _mod_spec(layer, 5, TOK_TILE, row_axis=0),
        pl.BlockSpec((TOK_TILE, ROUTER_LANES), lambda i, *_: (i, 0)),
        pl.BlockSpec(memory_space=pl.ANY),
    ]
    args = [x, mod, comb, ys]
    if split_out:
        out_specs = [pl.BlockSpec((TOK_TILE, D), lambda i, *_: (jnp.minimum(i, n_ctx_tiles - 1), 0)),
                     pl.BlockSpec((TOK_TILE, D), lambda i, *_: (jnp.maximum(i - n_ctx_tiles, 0), 0))]
        out_shape = [jax.ShapeDtypeStruct((N_CTX, D), F32), jax.ShapeDtypeStruct((N_LAT, D), F32)]
    else:
        in_specs += [pl.BlockSpec((1, D), lambda i, *_: (0, 0)),
                     _mod_spec(layer + 1, 1, TOK_TILE, row_axis=0),
                     _mod_spec(layer + 1, 0, TOK_TILE, row_axis=0)]
        args += [next_norm_g.reshape(1, D), mod, mod]
        out_specs = [tile_spec, tile_spec]
        out_shape = [jax.ShapeDtypeStruct((N_TOK, D), F32), jax.ShapeDtypeStruct((N_TOK, D), BF16)]
    grid_spec = pltpu.PrefetchScalarGridSpec(
        num_scalar_prefetch=3,
        grid=(N_TOK_TILES,),
        in_specs=in_specs,
        out_specs=out_specs,
        scratch_shapes=[
            pltpu.VMEM((2, TILE_ROWS, D), BF16),
            pltpu.SemaphoreType.DMA((2, N_GROUPS, len(RUN_BITS))),
        ],
    )
    return pl.pallas_call(
        functools.partial(_combine_kernel, split_out=split_out, emit_next=not split_out),
        grid_spec=grid_spec,
        out_shape=out_shape,
        compiler_params=_params("arbitrary"),
        name="moe_combine",
    )(*runs, *args)


def _moe_experts_and_combine(x, routed, mod, layer, w1, w3, w2, next_norm_g):
    hc, cc, comb, cnt = routed
    runs, end, plan = _moe_plan(cnt[:, 0, :N_GROUPS].astype(jnp.int32))
    hs, cs = _dispatch(hc, cc, runs, end)
    ys = _moe_experts(hs, cs, w1, w3, w2, layer, plan)
    return _combine(x, mod, layer, comb, ys, runs, next_norm_g)


def kernel(x_prompt, x_sample, cache_k, cache_v, c, c_ctx, ada_w, ada_b, norm_mix_g, norm_ffn_g,
           mix_in_w, pool_w, pool_scale, conv_w, conv_b, conv_ln_g, conv_ln_b, mix_out_w,
           qkv_w, q_norm_g, k_norm_g, rpb, attn_out_w, router_g_w, router_g_b, router_e_w,
           router_e_b, exp_w1, exp_w3, exp_w2):
    x = (x_prompt.reshape(N_CTX, D), x_sample.reshape(N_LAT, D))
    cvec8 = jnp.concatenate([c_ctx[None, :], c, jnp.zeros((8 - 1 - N_LAT_SEQ, D), F32)], axis=0)
    mod = _adaln_all(cvec8, ada_w, ada_b)
    n_attn = DEPTH // 2
    cache_k2 = cache_k.reshape(N_LAT_SEQ, n_attn, CTX_SEQ, D)
    cache_v2 = cache_v.reshape(N_LAT_SEQ, n_attn, CTX_SEQ, D)

    caches = None
    h = None
    for l in range(DEPTH):
        e = l // 2
        if l % 2 == 0:
            u = _mm_prologue(x, mod, l, norm_mix_g[l], mix_in_w, e, h=h)
            mixed = _seq_mixer(u, pool_w[e].astype(BF16), pool_scale[e], conv_w[e], conv_b[e],
                               conv_ln_g[e], conv_ln_b[e])
            out_w = mix_out_w
        else:
            q, k, v, new_k, new_v = _qkv_proj(h, e, qkv_w, q_norm_g[e], k_norm_g[e], caches)
            caches = (new_k, new_v)
            mixed = _ctx_attention(q, k, v)
            mixed = _na_attention(q, k, v, cache_k2[:, e], cache_v2[:, e], rpb[e], mixed)
            out_w = attn_out_w
        rw, rb = _router_weights(router_g_w[l], router_g_b[l], router_e_w[l], router_e_b[l])
        x, *routed = _out_proj_and_route(mixed, out_w, e, x, mod, l, norm_ffn_g[l], rw, rb)
        last = l == DEPTH - 1
        x, h = _moe_experts_and_combine(x, routed, mod, l, exp_w1, exp_w3, exp_w2,
                                        next_norm_g=None if last else norm_mix_g[l + 1])

    y_prompt = x.reshape(N_CTX_SEQ, CTX_SEQ, D)
    y_sample = h.reshape(N_LAT_SEQ, LAT_SEQ, D)
    cache_dims = (N_CTX_SEQ, n_attn, CTX_SEQ, N_HEADS, HEAD_DIM)
    return (y_prompt, y_sample, caches[0].reshape(cache_dims), caches[1].reshape(cache_dims))
```

```python
import functools

import numpy as np
import jax
import jax.numpy as jnp
from jax import lax
from jax.experimental import pallas as pl
from jax.experimental.pallas import tpu as pltpu

D = 2048
N_CTX_SEQ = 16
CTX_SEQ = 256
N_LAT_SEQ = 2
LAT_SEQ = 1024
N_CTX = N_CTX_SEQ * CTX_SEQ
N_LAT = N_LAT_SEQ * LAT_SEQ
N_TOK = N_CTX + N_LAT
DEPTH = 4
GRID_W = 64
GRID_H = LAT_SEQ // GRID_W
N_HEADS = 16
HEAD_DIM = 128
ATTN_SCALE = HEAD_DIM ** -0.5
D_POOL = 1024
D_CONV = 1024
POOL_WINDOWS = (2, 4, 8, 16)
POOL_GC = 256
CONV_W = 31
NA_KH = 8
NA_KW = 16
N_GROUPS = 4
EXP_PER_GROUP = 4
N_EXPERTS = 16
D_EXPERT = 512
EPS = 1e-6
NEG = -1e30

SEQ_TILE = 256
HALO = 16
ROUTER_LANES = 128
EXPERT_LANE0 = N_GROUPS
POS_LANE = EXPERT_LANE0 + N_EXPERTS
LOW_LANE0 = 32
VMEM_LIMIT = 56 * 1024 * 1024

TOK_TILE = 256
N_TOK_TILES = N_TOK // TOK_TILE
ROW_ALIGN = 16
RUN_BITS = (256, 128, 64, 32, 16)
TILE_ROWS = TOK_TILE + N_GROUPS * ROW_ALIGN
MOE_TILE = 1024
MOE_SUB = 128
FF_CHUNK = 512
OUT_CHUNK = 512
N_FF_CHUNKS = EXP_PER_GROUP * D_EXPERT // FF_CHUNK
N_OUT_CHUNKS = D // OUT_CHUNK
ZERO_ROWS = MOE_SUB
MAX_PADDED = N_TOK + N_TOK_TILES * N_GROUPS * (ROW_ALIGN - 1)
MOE_TILES_MAX = (MAX_PADDED + N_GROUPS * (MOE_TILE - 1)) // MOE_TILE
ROW_CAP = -(-(MAX_PADDED + N_GROUPS * (ZERO_ROWS + MOE_TILE - 1)) // MOE_TILE) * MOE_TILE

F32 = jnp.float32
BF16 = jnp.bfloat16


def _cond_of_row(row):
    row = jnp.minimum(row, N_TOK - 1)
    return jnp.where(row < N_CTX, 0, 1 + (row - N_CTX) // LAT_SEQ)


def _mod_spec(layer, which, tm, tn=D, row_axis=1, col_axis=None):
    def index_map(*ids):
        cond = _cond_of_row(ids[row_axis] * tm)
        col = 0 if col_axis is None else ids[col_axis]
        return (layer, cond, which, 0, col)
    return pl.BlockSpec((None, None, None, 1, tn), index_map)


def _silu(x):
    return x / (1.0 + jnp.exp(-x))


def _modnorm(x, g, sc, sh):
    ms = jnp.mean(x * x, axis=-1, keepdims=True)
    y = x * lax.rsqrt(ms + EPS) * g
    return y * (1.0 + sc) + sh


def _params(*sem):
    return pltpu.CompilerParams(dimension_semantics=sem, vmem_limit_bytes=VMEM_LIMIT)


def _ada_kernel(c_ref, w_ref, b_ref, o_ref):
    s = _silu(c_ref[...]).astype(BF16)
    o_ref[...] = jnp.dot(s, w_ref[...].astype(BF16), preferred_element_type=F32) + b_ref[...]


def _adaln_all(cvec8, ada_w, ada_b):
    tn = 1024
    n_out = 6 * D
    out = pl.pallas_call(
        _ada_kernel,
        grid=(DEPTH, n_out // tn),
        in_specs=[
            pl.BlockSpec((8, D), lambda l, j: (0, 0)),
            pl.BlockSpec((None, D, tn), lambda l, j: (l, 0, j)),
            pl.BlockSpec((None, 1, tn), lambda l, j: (l, 0, j)),
        ],
        out_specs=pl.BlockSpec((None, 8, tn), lambda l, j: (l, 0, j)),
        out_shape=jax.ShapeDtypeStruct((DEPTH, 8, n_out), F32),
        compiler_params=_params("arbitrary", "arbitrary"),
        name="adaln",
    )(cvec8, ada_w, ada_b.reshape(DEPTH, 1, n_out))
    return out[:, :3].reshape(DEPTH, 3, 6, 1, D)


def _token_specs(x, tm, tn, idx):
    if not isinstance(x, tuple):
        return [pl.BlockSpec((tm, tn), lambda j, i: idx(i, j))], [x]
    n_ctx_tiles = N_CTX // tm
    ctx_map = lambda j, i: idx(jnp.minimum(i, n_ctx_tiles - 1), j)
    lat_map = lambda j, i: idx(jnp.maximum(i - n_ctx_tiles, 0), j)
    return [pl.BlockSpec((tm, tn), ctx_map), pl.BlockSpec((tm, tn), lat_map)], list(x)


def _token_rows(refs, i, tm):
    if len(refs) == 1:
        return refs[0][...]
    return jnp.where(i < N_CTX // tm, refs[0][...], refs[1][...])


def _cast_weights_once(i, pairs):
    @pl.when(i == 0)
    def _():
        for w_ref, wb_ref in pairs:
            wb_ref[...] = w_ref[...].astype(BF16)


def _mm_pro_kernel(*refs, n_x, tm):
    x_refs = refs[:n_x]
    g_ref, sc_ref, sh_ref, w_ref, o_ref, wb_ref = refs[n_x:]
    i = pl.program_id(1)
    _cast_weights_once(i, [(w_ref, wb_ref)])
    h = _modnorm(_token_rows(x_refs, i, tm), g_ref[...], sc_ref[...], sh_ref[...]).astype(BF16)
    o_ref[...] = jnp.dot(h, wb_ref[...], preferred_element_type=F32)


def _mm_plain_kernel(h_ref, w_ref, o_ref, wb_ref):
    _cast_weights_once(pl.program_id(1), [(w_ref, wb_ref)])
    o_ref[...] = jnp.dot(h_ref[...], wb_ref[...], preferred_element_type=F32)


def _mm_prologue(x, mod, layer, norm_g, w, w_idx, h=None, tm=512, tn=1024):
    f = w.shape[2]
    if h is not None:
        tm = 2 * tm
    w_spec = pl.BlockSpec((None, D, tn), lambda j, i: (w_idx, 0, j))
    if h is None:
        x_specs, x_args = _token_specs(x, tm, D, lambda i, j: (i, 0))
        body = functools.partial(_mm_pro_kernel, n_x=len(x_args), tm=tm)
        in_specs = x_specs + [pl.BlockSpec((1, D), lambda j, i: (0, 0)),
                              _mod_spec(layer, 1, tm), _mod_spec(layer, 0, tm), w_spec]
        args = x_args + [norm_g.reshape(1, D), mod, mod, w]
    else:
        body = _mm_plain_kernel
        in_specs = [pl.BlockSpec((tm, D), lambda j, i: (i, 0)), w_spec]
        args = [h, w]
    return pl.pallas_call(
        body,
        grid=(f // tn, N_TOK // tm),
        in_specs=in_specs,
        out_specs=pl.BlockSpec((tm, tn), lambda j, i: (i, j)),
        out_shape=jax.ShapeDtypeStruct((N_TOK, f), F32),
        scratch_shapes=[pltpu.VMEM((D, tn), BF16)],
        compiler_params=_params("arbitrary", "arbitrary"),
        name="mm_prologue",
    )(*args)


def _qkv_kernel(h_ref, wq_ref, wk_ref, wv_ref, qg_ref, kg_ref, *rest, tn, n_ctx_tiles, seqs_per_tile):
    q_ref, k_ref, v_ref, ck_ref, cv_ref, wqb_ref, wkb_ref, wvb_ref, kn_ref, vf_ref = rest[-10:]
    i = pl.program_id(1)
    _cast_weights_once(i, [(wq_ref, wqb_ref), (wk_ref, wkb_ref), (wv_ref, wvb_ref)])
    h = h_ref[...]
    q = jnp.dot(h, wqb_ref[...], preferred_element_type=F32)
    k = jnp.dot(h, wkb_ref[...], preferred_element_type=F32)
    v = jnp.dot(h, wvb_ref[...], preferred_element_type=F32)
    for hh in range(tn // HEAD_DIM):
        ls = slice(hh * HEAD_DIM, (hh + 1) * HEAD_DIM)
        qh = q[:, ls]
        kh = k[:, ls]
        qn = qh * lax.rsqrt(jnp.mean(qh * qh, axis=-1, keepdims=True) + EPS) * qg_ref[:, ls]
        kn = kh * lax.rsqrt(jnp.mean(kh * kh, axis=-1, keepdims=True) + EPS) * kg_ref[:, ls]
        q_ref[:, ls] = qn.astype(BF16)
        k_ref[:, ls] = kn.astype(BF16)
        kn_ref[:, ls] = kn
    v_ref[...] = v.astype(BF16)
    vf_ref[...] = v

    @pl.when(i < n_ctx_tiles)
    def _():
        ck_ref[...] = kn_ref[...].reshape(seqs_per_tile, CTX_SEQ, tn)
        cv_ref[...] = vf_ref[...].reshape(seqs_per_tile, CTX_SEQ, tn)


def _qkv_proj(h, w_idx, w, q_gain, k_gain, caches, tm=512, tn=512):
    n = h.shape[0]
    n_attn = DEPTH // 2
    ncol = D // tn
    seqs_per_tile = tm // CTX_SEQ
    n_ctx_tiles = N_CTX // tm
    cache_shape = jax.ShapeDtypeStruct((N_CTX_SEQ, n_attn, CTX_SEQ, D), F32)
    cache_spec = pl.BlockSpec(
        (seqs_per_tile, None, CTX_SEQ, tn),
        lambda j, i: (jnp.minimum(i, n_ctx_tiles - 1), w_idx, 0, j))
    in_specs = [
        pl.BlockSpec((tm, D), lambda j, i: (i, 0)),
        pl.BlockSpec((None, D, tn), lambda j, i: (w_idx, 0, j)),
        pl.BlockSpec((None, D, tn), lambda j, i: (w_idx, 0, ncol + j)),
        pl.BlockSpec((None, D, tn), lambda j, i: (w_idx, 0, 2 * ncol + j)),
        pl.BlockSpec((1, tn), lambda j, i: (0, j)),
        pl.BlockSpec((1, tn), lambda j, i: (0, j)),
    ]
    args = [h, w, w, w,
            jnp.tile(q_gain, N_HEADS).reshape(1, D), jnp.tile(k_gain, N_HEADS).reshape(1, D)]
    aliases = {}
    if caches is not None:
        in_specs += [pl.BlockSpec(memory_space=pl.ANY), pl.BlockSpec(memory_space=pl.ANY)]
        aliases = {len(args): 3, len(args) + 1: 4}
        args += list(caches)
    act_spec = pl.BlockSpec((tm, tn), lambda j, i: (i, j))
    act_shape = jax.ShapeDtypeStruct((n, D), BF16)
    return pl.pallas_call(
        functools.partial(_qkv_kernel, tn=tn, n_ctx_tiles=n_ctx_tiles, seqs_per_tile=seqs_per_tile),
        grid=(ncol, n // tm),
        in_specs=in_specs,
        out_specs=[act_spec, act_spec, act_spec, cache_spec, cache_spec],
        out_shape=[act_shape, act_shape, act_shape, cache_shape, cache_shape],
        input_output_aliases=aliases,
        scratch_shapes=[pltpu.VMEM((D, tn), BF16)] * 3 + [pltpu.VMEM((tm, tn), F32)] * 2,
        compiler_params=_params("arbitrary", "arbitrary"),
        name="qkv_proj",
    )(*args)


def _seq_mix_tile(i, u_ref, top_ref, bot_ref, pw_ref, ps_ref, cw_ref, cb_ref, lg_ref, lb_ref,
                  o_ref, zpad_ref, conv_ref):
    n_ctx_tiles = N_CTX // SEQ_TILE
    tiles_per_lat = LAT_SEQ // SEQ_TILE
    is_lat = i >= n_ctx_tiles
    chunk = jnp.where(is_lat, (i - n_ctx_tiles) % tiles_per_lat, 0)
    top_ok = jnp.logical_and(is_lat, chunk > 0)
    bot_ok = jnp.logical_and(is_lat, chunk < tiles_per_lat - 1)
    seq_len = jnp.where(is_lat, LAT_SEQ, CTX_SEQ)
    t = chunk * SEQ_TILE + lax.broadcasted_iota(jnp.int32, (SEQ_TILE, 1), 0)
    rows = SEQ_TILE + 2 * HALO

    for g, win in enumerate(POOL_WINDOWS):
        ls = slice(g * POOL_GC, (g + 1) * POOL_GC)
        mid = u_ref[:, ls]
        top = jnp.where(top_ok, top_ref[:, ls], 0.0)
        bot = jnp.where(bot_ok, bot_ref[:, ls], 0.0)
        up = jnp.concatenate([top, mid, bot], axis=0)
        s = pltpu.roll(up, 1, 0) + up
        if win >= 4:
            s = pltpu.roll(s, 1, 0) + pltpu.roll(s, rows - 1, 0)
        if win >= 8:
            s = pltpu.roll(s, 2, 0) + pltpu.roll(s, rows - 2, 0)
        if win >= 16:
            s = pltpu.roll(s, 4, 0) + pltpu.roll(s, rows - 4, 0)
        s = s[HALO:HALO + SEQ_TILE]
        lo = jnp.maximum(t - win // 2, 0)
        hi = jnp.minimum(t + (win - win // 2) - 1, seq_len - 1)
        cnt = (hi - lo + 1).astype(F32)
        p = (s / cnt - mid).astype(BF16)
        y = jnp.dot(p, pw_ref[g], preferred_element_type=F32) * ps_ref[:, ls]
        o_ref[:, ls] = y.astype(BF16)

    def glu(ref):
        return ref[:, D_POOL:D_POOL + D_CONV] * (1.0 / (1.0 + jnp.exp(-ref[:, D_POOL + D_CONV:])))

    zpad_ref[0:HALO, :] = jnp.where(top_ok, glu(top_ref), 0.0)
    zpad_ref[HALO:HALO + SEQ_TILE, :] = glu(u_ref)
    zpad_ref[HALO + SEQ_TILE:rows, :] = jnp.where(bot_ok, glu(bot_ref), 0.0)
    for lt in range(D_CONV // 128):
        ls = slice(lt * 128, (lt + 1) * 128)
        zp = zpad_ref[:, ls]
        acc = jnp.zeros((SEQ_TILE, 128), F32)
        for b in range(8):
            sb = zp if b == 0 else pltpu.roll(zp, rows - b, 0)
            for a in range(4):
                off = 8 * a + b
                if 1 <= off <= CONV_W:
                    acc = acc + cw_ref[off - 1:off, ls] * sb[8 * a:8 * a + SEQ_TILE]
        conv_ref[:, ls] = acc + cb_ref[:, ls]
    zc = conv_ref[...]
    mu = jnp.mean(zc, axis=-1, keepdims=True)
    d = zc - mu
    var = jnp.mean(d * d, axis=-1, keepdims=True)
    zn = d * lax.rsqrt(var + EPS) * lg_ref[...] + lb_ref[...]
    o_ref[:, D_POOL:] = _silu(zn).astype(BF16)


def _seq_kernel(*refs):
    _seq_mix_tile(pl.program_id(0), *refs)


def _seq_mixer(u, pool_w_bf16, pool_scale, conv_w, conv_b, ln_g, ln_b):
    n = u.shape[0]
    f_in = u.shape[1]
    hb = SEQ_TILE // HALO
    n_hblocks = n // HALO
    const = lambda *shape: pl.BlockSpec(shape, lambda i: (0,) * len(shape))
    return pl.pallas_call(
        _seq_kernel,
        grid=(n // SEQ_TILE,),
        in_specs=[
            pl.BlockSpec((SEQ_TILE, f_in), lambda i: (i, 0)),
            pl.BlockSpec((HALO, f_in), lambda i: (jnp.maximum(i * hb - 1, 0), 0)),
            pl.BlockSpec((HALO, f_in), lambda i: (jnp.minimum((i + 1) * hb, n_hblocks - 1), 0)),
            const(len(POOL_WINDOWS), POOL_GC, POOL_GC),
            const(1, D_POOL), const(CONV_W, D_CONV), const(1, D_CONV), const(1, D_CONV), const(1, D_CONV),
        ],
        out_specs=pl.BlockSpec((SEQ_TILE, D_POOL + D_CONV), lambda i: (i, 0)),
        out_shape=jax.ShapeDtypeStruct((n, D_POOL + D_CONV), BF16),
        scratch_shapes=[
            pltpu.VMEM((SEQ_TILE + 2 * HALO, D_CONV), F32),
            pltpu.VMEM((SEQ_TILE, D_CONV), F32),
        ],
        compiler_params=_params("arbitrary"),
        name="seq_mixer",
    )(u, u, u, pool_w_bf16, pool_scale.reshape(1, D_POOL), conv_w, conv_b.reshape(1, D_CONV),
      ln_g.reshape(1, D_CONV), ln_b.reshape(1, D_CONV))


_NT = (((1,), (1,)), ((), ()))


def _ctx_attn_kernel(q_ref, k_ref, v_ref, o_ref, s_ref, p_ref):
    for h in range(N_HEADS):
        ls = slice(h * HEAD_DIM, (h + 1) * HEAD_DIM)
        s_ref[h] = lax.dot_general(q_ref[:, ls], k_ref[:, ls], _NT, preferred_element_type=F32) * ATTN_SCALE
    for h in range(N_HEADS):
        s = s_ref[h]
        e = jnp.exp(s - jnp.max(s, axis=-1, keepdims=True))
        p_ref[h] = (e * (1.0 / jnp.sum(e, axis=-1, keepdims=True))).astype(BF16)
    for h in range(N_HEADS):
        ls = slice(h * HEAD_DIM, (h + 1) * HEAD_DIM)
        o_ref[:, ls] = jnp.dot(p_ref[h], v_ref[:, ls], preferred_element_type=F32).astype(BF16)


def _ctx_attention(q, k, v):
    n = q.shape[0]
    spec = pl.BlockSpec((CTX_SEQ, D), lambda b: (b, 0))
    return pl.pallas_call(
        _ctx_attn_kernel,
        grid=(N_CTX_SEQ,),
        in_specs=[spec, spec, spec],
        out_specs=spec,
        out_shape=jax.ShapeDtypeStruct((n, D), BF16),
        scratch_shapes=[pltpu.VMEM((N_HEADS, CTX_SEQ, CTX_SEQ), F32),
                        pltpu.VMEM((N_HEADS, CTX_SEQ, CTX_SEQ), BF16)],
        compiler_params=_params("arbitrary"),
        name="ctx_attention",
    )(q, k, v)


def _na_window_mask():
    cq = np.arange(GRID_W)[:, None]
    kc = np.arange(GRID_W)[None, :]
    win0 = np.clip(cq - NA_KW // 2, 0, GRID_W - NA_KW)
    mask = ((kc >= win0) & (kc < win0 + NA_KW)).astype(np.float32)
    return np.tile(mask, (1, NA_KH))


def _na_bias_rows(rpb_e):
    centre = NA_KW - 1
    pad = jnp.zeros(rpb_e.shape[:2] + (128 - (2 * NA_KW - 1),), F32)
    return jnp.concatenate([rpb_e[..., centre:], pad, rpb_e[..., :centre]], axis=-1)


def _na_key_row0(r):
    return min(max(r - NA_KH // 2, 0), GRID_H - NA_KH)


def _na_segments():
    segs = []
    for r in range(GRID_H):
        kr0 = _na_key_row0(r)
        if segs and segs[-1][2] == kr0:
            segs[-1][1] = r + 1
        else:
            segs.append([r, r + 1, kr0])
    return segs


def _na_kernel(q_ref, k_ref, v_ref, kc_ref, vc_ref, rows_ref, mask_ref, o_in_ref, o_ref,
               s_ref, p_ref, acc_ref, tz_ref):
    del o_in_ref
    band = NA_KH * GRID_W
    n_ctx = CTX_SEQ
    kc = kc_ref[...].astype(BF16)
    vc = vc_ref[...].astype(BF16)
    @pl.when(pl.program_id(1) == 0)
    def _():
        n_dr = 2 * NA_KH - 1
        toeplitz = [pltpu.roll(jnp.broadcast_to(rows_ref[d:d + 1, :], (GRID_W, 2 * GRID_W)), 0, 1,
                               stride=1, stride_axis=0) for d in range(n_dr)]
        left = lax.broadcasted_iota(jnp.int32, (GRID_W, 2 * GRID_W), 1) < GRID_W
        for d in range(n_dr - 1):
            tz_ref[d] = jnp.where(left, toeplitz[d], pltpu.roll(toeplitz[d + 1], GRID_W, 1))
    for r0, r1, kr0 in _na_segments():
        rs = slice(r0 * GRID_W, r1 * GRID_W)
        kb = k_ref[kr0 * GRID_W:kr0 * GRID_W + band, :]
        s_ref[rs, 0:band] = lax.dot_general(q_ref[rs, :], kb, _NT, preferred_element_type=F32) * ATTN_SCALE
    s_ref[:, band:band + n_ctx] = lax.dot_general(q_ref[...], kc, _NT, preferred_element_type=F32) * ATTN_SCALE
    mask = mask_ref[...] > 0.5
    for r in range(GRID_H):
        rs = slice(r * GRID_W, (r + 1) * GRID_W)
        kr0 = _na_key_row0(r)
        bias = jnp.concatenate(
            [tz_ref[kr0 + j - r + NA_KH - 1] for j in range(0, NA_KH, 2)], axis=-1)
        s_loc = jnp.where(mask, s_ref[rs, 0:band] + bias, NEG)
        s_ctx = s_ref[rs, band:band + n_ctx]
        m = jnp.maximum(jnp.max(s_loc, axis=-1, keepdims=True), jnp.max(s_ctx, axis=-1, keepdims=True))
        e_loc = jnp.exp(s_loc - m)
        e_ctx = jnp.exp(s_ctx - m)
        inv = 1.0 / (jnp.sum(e_loc, axis=-1, keepdims=True) + jnp.sum(e_ctx, axis=-1, keepdims=True))
        p_ref[rs, 0:band] = (e_loc * inv).astype(BF16)
        p_ref[rs, band:band + n_ctx] = (e_ctx * inv).astype(BF16)
    for r0, r1, kr0 in _na_segments():
        rs = slice(r0 * GRID_W, r1 * GRID_W)
        vb = v_ref[kr0 * GRID_W:kr0 * GRID_W + band, :]
        acc_ref[rs, :] = jnp.dot(p_ref[rs, 0:band], vb, preferred_element_type=F32)
    o_ctx = jnp.dot(p_ref[:, band:band + n_ctx], vc, preferred_element_type=F32)
    o_ref[...] = (acc_ref[...] + o_ctx).astype(BF16)


def _na_attention(q, k, v, cache_k_e, cache_v_e, rpb_e, o_ctx):
    mask = _na_window_mask()
    lat0 = N_CTX // LAT_SEQ
    n_dr = 2 * NA_KH - 1
    n_keys = NA_KH * GRID_W + CTX_SEQ
    tok_spec = pl.BlockSpec((LAT_SEQ, HEAD_DIM), lambda h, b: (lat0 + b, h))
    cache_spec = pl.BlockSpec((None, CTX_SEQ, HEAD_DIM), lambda h, b: (b, 0, h))
    return pl.pallas_call(
        _na_kernel,
        grid=(N_HEADS, N_LAT_SEQ),
        in_specs=[
            tok_spec, tok_spec, tok_spec, cache_spec, cache_spec,
            pl.BlockSpec((None, n_dr, 2 * GRID_W), lambda h, b: (h, 0, 0)),
            pl.BlockSpec((GRID_W, NA_KH * GRID_W), lambda h, b: (0, 0)),
            pl.BlockSpec(memory_space=pl.ANY),
        ],
        out_specs=tok_spec,
        out_shape=jax.ShapeDtypeStruct(o_ctx.shape, o_ctx.dtype),
        input_output_aliases={7: 0},
        scratch_shapes=[pltpu.VMEM((LAT_SEQ, n_keys), F32),
                        pltpu.VMEM((LAT_SEQ, n_keys), BF16),
                        pltpu.VMEM((LAT_SEQ, HEAD_DIM), F32),
                        pltpu.VMEM((n_dr - 1, GRID_W, 2 * GRID_W), F32)],
        compiler_params=_params("arbitrary", "arbitrary"),
        name="na_attention",
    )(q, k, v, cache_k_e, cache_v_e, _na_bias_rows(rpb_e), jnp.asarray(mask), o_ctx)


def _split_bf16(x, pieces):
    out = []
    for _ in range(pieces):
        p = x.astype(BF16)
        out.append(p)
        x = x - p.astype(F32)
    return out


def _moe_prenorm(x, g_ref, sc_ref, sh_ref):
    h = _modnorm(x, g_ref[...], sc_ref[...], sh_ref[...])
    return jnp.concatenate(_split_bf16(h, 2), axis=1)


def _router_logits(h_pieces, rw_ref, rb_ref):
    prod = jnp.dot(h_pieces, rw_ref[...], preferred_element_type=F32)
    return prod + pltpu.roll(prod, ROUTER_LANES - LOW_LANE0, 1) + rb_ref[...]


def _route_sort(h_hi, logits, hc_ref, cc_ref, comb_ref, cnt_ref):
    lane = lax.broadcasted_iota(jnp.int32, logits.shape, 1)
    big = jnp.int32(1 << 20)
    neg_inf = -jnp.inf

    def first_argmax(vals, valid):
        v = jnp.where(valid, vals, neg_inf)
        m = jnp.max(v, axis=-1, keepdims=True)
        idx = jnp.min(jnp.where(jnp.logical_and(valid, v == m), lane, big), axis=-1, keepdims=True)
        return m, idx

    is_g = lane < N_GROUPS
    mg, gi = first_argmax(logits, is_g)
    pg = 1.0 / jnp.sum(jnp.where(is_g, jnp.exp(logits - mg), 0.0), axis=-1, keepdims=True)
    e_lane0 = EXPERT_LANE0 + gi * EXP_PER_GROUP
    is_e = jnp.logical_and(lane >= e_lane0, lane < e_lane0 + EXP_PER_GROUP)
    m1, i1 = first_argmax(logits, is_e)
    m2, i2 = first_argmax(logits, jnp.logical_and(is_e, lane != i1))
    e2 = jnp.exp(m2 - m1)
    den = 1.0 + e2
    w1 = (1.0 / den) * pg
    w2 = (e2 / den) * pg
    onehot_g = jnp.where(lane == gi, 1.0, 0.0)

    row = lax.broadcasted_iota(jnp.int32, (TOK_TILE, TOK_TILE), 0)
    col = lax.broadcasted_iota(jnp.int32, (TOK_TILE, TOK_TILE), 1)
    earlier = jnp.where(col < row, 1.0, 0.0).astype(BF16)
    rank = jnp.dot(earlier, onehot_g.astype(BF16), preferred_element_type=F32)
    cnt = jnp.sum(onehot_g, axis=0, keepdims=True)
    padded = jnp.floor((cnt + (ROW_ALIGN - 1)) * (1.0 / ROW_ALIGN)) * ROW_ALIGN
    padded8 = jnp.broadcast_to(padded, (8, ROUTER_LANES))
    run0 = (pltpu.roll(padded8, 1, 1) + pltpu.roll(padded8, 2, 1) + pltpu.roll(padded8, 3, 1))[0:1, :]
    pos = jnp.sum(onehot_g * (rank + run0), axis=-1, keepdims=True)

    comb = jnp.where(lane == i1, w1, jnp.where(lane == i2, w2, jnp.where(lane == POS_LANE, pos, onehot_g)))
    comb_ref[...] = comb
    cnt_ref[...] = cnt

    pos_t = comb.T[POS_LANE:POS_LANE + 1, :]
    dest = lax.broadcasted_iota(jnp.int32, (TILE_ROWS, TOK_TILE), 0).astype(F32)
    perm = jnp.where(pos_t == dest, 1.0, 0.0).astype(BF16)
    hc_ref[...] = jnp.dot(perm, h_hi, preferred_element_type=F32).astype(BF16)
    c_parts = _split_bf16(comb, 3)
    cc_ref[...] = (jnp.dot(perm, c_parts[0], preferred_element_type=F32)
                   + jnp.dot(perm, c_parts[1], preferred_element_type=F32)
                   + jnp.dot(perm, c_parts[2], preferred_element_type=F32))


W_STAGE_COLS = 512


def _out_route_kernel(a_ref, w_hbm, gate_ref, g_ref, sc_ref, sh_ref, rw_ref, rb_ref, *refs, n_x, w_idx):
    x_refs = refs[:n_x]
    xo_ref, hc_ref, cc_ref, comb_ref, cnt_ref, wb_ref, stage_ref, prev_ref, sem = refs[n_x:]
    i = pl.program_id(0)

    @pl.when(i == 0)
    def _():
        prev_ref[...] = jnp.zeros_like(prev_ref)
        n_chunks = D // W_STAGE_COLS

        def chunk_copy(c):
            cols = pl.ds(c * W_STAGE_COLS, W_STAGE_COLS)
            return pltpu.make_async_copy(w_hbm.at[w_idx, :, cols], stage_ref.at[c % 2], sem.at[c % 2])

        chunk_copy(0).start()
        chunk_copy(1).start()
        for c in range(n_chunks):
            chunk_copy(c).wait()
            wb_ref[:, c * W_STAGE_COLS:(c + 1) * W_STAGE_COLS] = stage_ref[c % 2].astype(BF16)
            if c + 2 < n_chunks:
                chunk_copy(c + 2).start()

    h_prev = prev_ref[...]
    logits = _router_logits(h_prev, rw_ref, rb_ref)
    y = jnp.dot(a_ref[...], wb_ref[...], preferred_element_type=F32)
    x_mid = _token_rows(x_refs, jnp.minimum(i, N_TOK_TILES - 1), TOK_TILE) + gate_ref[...] * y
    xo_ref[...] = x_mid
    _route_sort(h_prev[:, :D], logits, hc_ref, cc_ref, comb_ref, cnt_ref)
    prev_ref[...] = _moe_prenorm(x_mid, g_ref, sc_ref, sh_ref)


def _out_proj_and_route(a_bf16, w, w_idx, x, mod, layer, norm_g, rw, rb):
    tm = TOK_TILE
    k = a_bf16.shape[1]
    last = N_TOK_TILES - 1
    proj = lambda i: jnp.minimum(i, last)
    routed = lambda i: jnp.maximum(i - 1, 0)
    if isinstance(x, tuple):
        n_ctx_tiles = N_CTX // tm
        x_specs = [pl.BlockSpec((tm, D), lambda i: (jnp.minimum(proj(i), n_ctx_tiles - 1), 0)),
                   pl.BlockSpec((tm, D), lambda i: (jnp.maximum(proj(i) - n_ctx_tiles, 0), 0))]
        x_args = list(x)
    else:
        x_specs = [pl.BlockSpec((tm, D), lambda i: (proj(i), 0))]
        x_args = [x]

    def mod_of_proj(which):
        return pl.BlockSpec((None, None, None, 1, D),
                            lambda i: (layer, _cond_of_row(proj(i) * tm), which, 0, 0))

    return pl.pallas_call(
        functools.partial(_out_route_kernel, n_x=len(x_args), w_idx=w_idx),
        grid=(N_TOK_TILES + 1,),
        in_specs=[
            pl.BlockSpec((tm, k), lambda i: (proj(i), 0)),
            pl.BlockSpec(memory_space=pl.ANY),
            mod_of_proj(2),
            pl.BlockSpec((1, D), lambda i: (0, 0)),
            mod_of_proj(4),
            mod_of_proj(3),
            pl.BlockSpec((2 * D, ROUTER_LANES), lambda i: (0, 0)),
            pl.BlockSpec((1, ROUTER_LANES), lambda i: (0, 0)),
        ] + x_specs,
        out_specs=[
            pl.BlockSpec((tm, D), lambda i: (proj(i), 0)),
            pl.BlockSpec((None, TILE_ROWS, D), lambda i: (routed(i), 0, 0)),
            pl.BlockSpec((None, TILE_ROWS, ROUTER_LANES), lambda i: (routed(i), 0, 0)),
            pl.BlockSpec((tm, ROUTER_LANES), lambda i: (routed(i), 0)),
            pl.BlockSpec((None, 1, ROUTER_LANES), lambda i: (routed(i), 0, 0)),
        ],
        out_shape=[
            jax.ShapeDtypeStruct((N_TOK, D), F32),
            jax.ShapeDtypeStruct((N_TOK_TILES, TILE_ROWS, D), BF16),
            jax.ShapeDtypeStruct((N_TOK_TILES, TILE_ROWS, ROUTER_LANES), F32),
            jax.ShapeDtypeStruct((N_TOK, ROUTER_LANES), F32),
            jax.ShapeDtypeStruct((N_TOK_TILES, 1, ROUTER_LANES), F32),
        ],
        scratch_shapes=[
            pltpu.VMEM((k, D), BF16),
            pltpu.VMEM((2, k, W_STAGE_COLS), F32),
            pltpu.VMEM((tm, 2 * D), BF16),
            pltpu.SemaphoreType.DMA((2,)),
        ],
        compiler_params=_params("arbitrary"),
        name="out_proj_route",
    )(a_bf16, w, mod, norm_g.reshape(1, D), mod, mod, rw, rb, *x_args)


def _router_weights(rgw, rgb, rew, reb):
    n_out = N_GROUPS + N_EXPERTS
    rw = jnp.concatenate([rgw, rew.reshape(D, N_EXPERTS)], axis=1)
    hi = rw.astype(BF16)
    lo = (rw - hi.astype(F32)).astype(BF16)
    zeros = lambda n: jnp.zeros((D, n), BF16)
    top = jnp.concatenate([hi, zeros(LOW_LANE0 - n_out), lo, zeros(ROUTER_LANES - LOW_LANE0 - n_out)], axis=1)
    bottom = jnp.concatenate([hi, zeros(ROUTER_LANES - n_out)], axis=1)
    rb = jnp.concatenate([rgb, reb.reshape(N_EXPERTS)])
    rb = jnp.pad(rb, (0, ROUTER_LANES - n_out)).reshape(1, ROUTER_LANES)
    return jnp.concatenate([top, bottom], axis=0), rb


def _moe_plan(cnt):
    i32 = jnp.int32
    padded = (cnt + (ROW_ALIGN - 1)) // ROW_ALIGN * ROW_ALIGN
    run0 = jnp.cumsum(padded, axis=1) - padded
    length = jnp.sum(padded, axis=0)
    span = (length + ZERO_ROWS + MOE_TILE - 1) // MOE_TILE * MOE_TILE
    start = jnp.cumsum(span) - span
    off = start[None, :] + jnp.cumsum(padded, axis=0) - padded
    end = start + length
    need = length
    ntile = (need + MOE_TILE - 1) // MOE_TILE
    cum = jnp.cumsum(ntile)
    total = cum[-1]
    k = jnp.arange(MOE_TILES_MAX, dtype=i32)
    kk = jnp.minimum(k, jnp.maximum(total - 1, 0))
    grp = jnp.minimum(jnp.sum((kk[:, None] >= cum[None, :]).astype(i32), axis=1), N_GROUPS - 1)
    j = kk - (cum - ntile)[grp]
    blk = start[grp] // MOE_TILE + j
    nsub = jnp.clip((need[grp] - j * MOE_TILE + MOE_SUB - 1) // MOE_SUB, 0, MOE_TILE // MOE_SUB)
    nsub = jnp.where(k < total, nsub, 0)
    runs = tuple(a.reshape(-1).astype(i32) for a in (off, run0, padded))
    return runs, end.astype(i32), (blk.astype(i32), grp.astype(i32), nsub.astype(i32))


def _run_pieces(run_refs, tile, g):
    off_ref, run0_ref, padded_ref = run_refs
    idx = tile * N_GROUPS + g
    n = padded_ref[idx]
    src0 = run0_ref[idx]
    dst0 = off_ref[idx]
    pieces = []
    for b, bit in enumerate(RUN_BITS):
        done = n & (-2 * bit)
        pieces.append((b, (n & bit) != 0, pl.multiple_of(src0 + done, ROW_ALIGN),
                       pl.multiple_of(dst0 + done, ROW_ALIGN), bit))
    return pieces


def _dispatch_kernel(off_ref, run0_ref, padded_ref, end_ref, hc_ref, cc_ref, hs_ref, cs_ref,
                     hb_ref, cb_ref, zh_ref, zc_ref, sem, sem_z):
    run_refs = (off_ref, run0_ref, padded_ref)
    i = pl.program_id(0)
    n_steps = pl.num_programs(0)
    slot = i % 2

    def for_each_copy(tile, sl, fn):
        for g in range(N_GROUPS):
            for b, pred, src, dst, rows in _run_pieces(run_refs, tile, g):
                @pl.when(pred)
                def _():
                    fn(pltpu.make_async_copy(hb_ref.at[sl, pl.ds(src, rows), :],
                                             hs_ref.at[pl.ds(dst, rows), :], sem.at[sl, 0, g, b]))
                    fn(pltpu.make_async_copy(cb_ref.at[sl, pl.ds(src, rows), :],
                                             cs_ref.at[pl.ds(dst, rows), :], sem.at[sl, 1, g, b]))

    def zero_copies():
        cps = []
        for g in range(N_GROUPS):
            r0 = pl.multiple_of(end_ref[g], ROW_ALIGN)
            cps.append(pltpu.make_async_copy(zh_ref, hs_ref.at[pl.ds(r0, ZERO_ROWS), :], sem_z.at[0, g]))
            cps.append(pltpu.make_async_copy(zc_ref, cs_ref.at[pl.ds(r0, ZERO_ROWS), :], sem_z.at[1, g]))
        return cps

    @pl.when(i == 0)
    def _():
        zh_ref[...] = jnp.zeros_like(zh_ref)
        zc_ref[...] = jnp.zeros_like(zc_ref)
        for cp in zero_copies():
            cp.start()

    hb_ref[slot] = hc_ref[...]
    cb_ref[slot] = cc_ref[...]
    for_each_copy(i, slot, lambda cp: cp.start())

    @pl.when(i > 0)
    def _():
        for_each_copy(i - 1, 1 - slot, lambda cp: cp.wait())

    @pl.when(i == n_steps - 1)
    def _():
        for_each_copy(i, slot, lambda cp: cp.wait())
        for cp in zero_copies():
            cp.wait()


def _dispatch(hc, cc, runs, end):
    any_spec = pl.BlockSpec(memory_space=pl.ANY)
    grid_spec = pltpu.PrefetchScalarGridSpec(
        num_scalar_prefetch=4,
        grid=(N_TOK_TILES,),
        in_specs=[pl.BlockSpec((None, TILE_ROWS, D), lambda i, *_: (i, 0, 0)),
                  pl.BlockSpec((None, TILE_ROWS, ROUTER_LANES), lambda i, *_: (i, 0, 0))],
        out_specs=[any_spec, any_spec],
        scratch_shapes=[
            pltpu.VMEM((2, TILE_ROWS, D), BF16),
            pltpu.VMEM((2, TILE_ROWS, ROUTER_LANES), F32),
            pltpu.VMEM((ZERO_ROWS, D), BF16),
            pltpu.VMEM((ZERO_ROWS, ROUTER_LANES), F32),
            pltpu.SemaphoreType.DMA((2, 2, N_GROUPS, len(RUN_BITS))),
            pltpu.SemaphoreType.DMA((2, N_GROUPS)),
        ],
    )
    return pl.pallas_call(
        _dispatch_kernel,
        grid_spec=grid_spec,
        out_shape=[jax.ShapeDtypeStruct((ROW_CAP, D), BF16),
                   jax.ShapeDtypeStruct((ROW_CAP, ROUTER_LANES), F32)],
        compiler_params=_params("arbitrary"),
        name="moe_dispatch",
    )(*runs, end, hc, cc)


def _moe_kernel(blk_ref, grp_ref, nsub_ref, h_ref, c_ref, w1_ref, w3_ref, w2_ref, o_ref, hid_ref):
    k = pl.program_id(0)
    c = pl.program_id(1)
    ns = nsub_ref[k]
    n_sub_max = MOE_TILE // MOE_SUB

    def for_row_count(fn):
        for m in range(1, n_sub_max + 1):
            @pl.when(ns == m)
            def _():
                fn(m * MOE_SUB)

    @pl.when(jnp.logical_and(ns > 0, c < N_FF_CHUNKS))
    def _():
        w1 = w1_ref[...].astype(BF16)
        w3 = w3_ref[...].astype(BF16)
        e_lane = EXPERT_LANE0 + grp_ref[k] * EXP_PER_GROUP + c * FF_CHUNK // D_EXPERT

        def up(rows):
            h = h_ref[0:rows, :]
            a = jnp.dot(h, w1, preferred_element_type=F32)
            b = jnp.dot(h, w3, preferred_element_type=F32)
            comb = c_ref[0:rows, :]
            lane = lax.broadcasted_iota(jnp.int32, comb.shape, 1)
            cw = jnp.sum(jnp.where(lane == e_lane, comb, 0.0), axis=-1, keepdims=True)
            hid_ref[c, 0:rows, :] = (_silu(a) * b * cw).astype(BF16)

        for_row_count(up)

    @pl.when(jnp.logical_and(ns > 0, c >= N_FF_CHUNKS))
    def _():
        w2 = w2_ref[...].reshape(EXP_PER_GROUP * D_EXPERT, OUT_CHUNK).astype(BF16)

        def down(rows):
            hid = jnp.concatenate([hid_ref[j, 0:rows, :] for j in range(N_FF_CHUNKS)], axis=1)
            o_ref[0:rows, :] = jnp.dot(hid, w2, preferred_element_type=F32).astype(BF16)
            if rows < MOE_TILE:
                o_ref[rows:, :] = jnp.zeros((MOE_TILE - rows, OUT_CHUNK), BF16)

        for_row_count(down)


def _moe_experts(hs, cs, w1, w3, w2, layer, plan):
    per_e = D_EXPERT // FF_CHUNK
    n_steps = N_FF_CHUNKS + N_OUT_CHUNKS

    def step_of(k, c, nsub_ref):
        return jnp.where(nsub_ref[k] > 0, c, n_steps - 1)

    def w13_map(k, c, blk_ref, grp_ref, nsub_ref):
        cc = jnp.minimum(step_of(k, c, nsub_ref), N_FF_CHUNKS - 1)
        return (layer, grp_ref[k] * EXP_PER_GROUP + cc // per_e, 0, cc % per_e)

    def out_chunk(k, c, nsub_ref):
        return jnp.maximum(step_of(k, c, nsub_ref) - N_FF_CHUNKS, 0)

    w2_map = lambda k, c, blk_ref, grp_ref, nsub_ref: (layer, grp_ref[k], 0, out_chunk(k, c, nsub_ref))
    row_map = lambda k, c, blk_ref, grp_ref, nsub_ref: (blk_ref[k], 0)
    out_map = lambda k, c, blk_ref, grp_ref, nsub_ref: (blk_ref[k], out_chunk(k, c, nsub_ref))
    grid_spec = pltpu.PrefetchScalarGridSpec(
        num_scalar_prefetch=3,
        grid=(MOE_TILES_MAX, n_steps),
        in_specs=[
            pl.BlockSpec((MOE_TILE, D), row_map),
            pl.BlockSpec((MOE_TILE, ROUTER_LANES), row_map),
            pl.BlockSpec((None, None, D, FF_CHUNK), w13_map),
            pl.BlockSpec((None, None, D, FF_CHUNK), w13_map),
            pl.BlockSpec((None, EXP_PER_GROUP, D_EXPERT, OUT_CHUNK), w2_map),
        ],
        out_specs=pl.BlockSpec((MOE_TILE, OUT_CHUNK), out_map),
        scratch_shapes=[pltpu.VMEM((N_FF_CHUNKS, MOE_TILE, FF_CHUNK), BF16)],
    )
    return pl.pallas_call(
        _moe_kernel,
        grid_spec=grid_spec,
        out_shape=jax.ShapeDtypeStruct((ROW_CAP, D), BF16),
        compiler_params=_params("arbitrary", "arbitrary"),
        name="moe_experts",
    )(*plan, hs, cs, w1, w3, w2)


def _combine_kernel(off_ref, run0_ref, padded_ref, x_ref, gate_ref, comb_ref, ys_ref, *refs,
                    split_out, emit_next):
    if emit_next:
        (ng_ref, nsc_ref, nsh_ref), refs = refs[:3], refs[3:]
    out_refs, (yw_ref, sem) = refs[:-2], refs[-2:]
    run_refs = (off_ref, run0_ref, padded_ref)
    i = pl.program_id(0)
    n_steps = pl.num_programs(0)
    slot = i % 2

    def for_each_copy(tile, sl, fn):
        for g in range(N_GROUPS):
            for b, pred, src, dst, rows in _run_pieces(run_refs, tile, g):
                @pl.when(pred)
                def _():
                    fn(pltpu.make_async_copy(ys_ref.at[pl.ds(dst, rows), :],
                                             yw_ref.at[sl, pl.ds(src, rows), :], sem.at[sl, g, b]))

    def fetch(tile, sl):
        yw_ref[sl, TOK_TILE:, :] = jnp.zeros((TILE_ROWS - TOK_TILE, D), BF16)
        for_each_copy(tile, sl, lambda cp: cp.start())

    @pl.when(i == 0)
    def _():
        fetch(i, slot)

    @pl.when(i + 1 < n_steps)
    def _():
        fetch(i + 1, 1 - slot)

    for_each_copy(i, slot, lambda cp: cp.wait())

    pos = comb_ref[:, POS_LANE:POS_LANE + 1].astype(jnp.int32)
    wcol = lax.broadcasted_iota(jnp.int32, (TOK_TILE, TILE_ROWS), 1)
    perm = jnp.where(wcol == pos, 1.0, 0.0).astype(BF16)
    y = jnp.dot(perm, yw_ref[slot], preferred_element_type=F32)
    out = x_ref[...] + gate_ref[...] * y
    if emit_next:
        out_refs[1][...] = _modnorm(out, ng_ref[...], nsc_ref[...], nsh_ref[...]).astype(BF16)
    if not split_out:
        out_refs[0][...] = out
    else:
        @pl.when(i < N_CTX // TOK_TILE)
        def _():
            out_refs[0][...] = out

        @pl.when(i >= N_CTX // TOK_TILE)
        def _():
            out_refs[1][...] = out


def _combine(x, mod, layer, comb, ys, runs, next_norm_g):
    n_ctx_tiles = N_CTX // TOK_TILE
    split_out = next_norm_g is None
    tile_spec = pl.BlockSpec((TOK_TILE, D), lambda i, *_: (i, 0))
    in_specs = [
        tile_spec,
        _mod_spec(layer, 5, TOK_TILE, row_axis=0),
        pl.BlockSpec((TOK_TILE, ROUTER_LANES), lambda i, *_: (i, 0)),
        pl.BlockSpec(memory_space=pl.ANY),
    ]
    args = [x, mod, comb, ys]
    if split_out:
        out_specs = [pl.BlockSpec((TOK_TILE, D), lambda i, *_: (jnp.minimum(i, n_ctx_tiles - 1), 0)),
                     pl.BlockSpec((TOK_TILE, D), lambda i, *_: (jnp.maximum(i - n_ctx_tiles, 0), 0))]
        out_shape = [jax.ShapeDtypeStruct((N_CTX, D), F32), jax.ShapeDtypeStruct((N_LAT, D), F32)]
    else:
        in_specs += [pl.BlockSpec((1, D), lambda i, *_: (0, 0)),
                     _mod_spec(layer + 1, 1, TOK_TILE, row_axis=0),
                     _mod_spec(layer + 1, 0, TOK_TILE, row_axis=0)]
        args += [next_norm_g.reshape(1, D), mod, mod]
        out_specs = [tile_spec, tile_spec]
        out_shape = [jax.ShapeDtypeStruct((N_TOK, D), F32), jax.ShapeDtypeStruct((N_TOK, D), BF16)]
    grid_spec = pltpu.PrefetchScalarGridSpec(
        num_scalar_prefetch=3,
        grid=(N_TOK_TILES,),
        in_specs=in_specs,
        out_specs=out_specs,
        scratch_shapes=[
            pltpu.VMEM((2, TILE_ROWS, D), BF16),
            pltpu.SemaphoreType.DMA((2, N_GROUPS, len(RUN_BITS))),
        ],
    )
    return pl.pallas_call(
        functools.partial(_combine_kernel, split_out=split_out, emit_next=not split_out),
        grid_spec=grid_spec,
        out_shape=out_shape,
        compiler_params=_params("arbitrary"),
        name="moe_combine",
    )(*runs, *args)


def _moe_experts_and_combine(x, routed, mod, layer, w1, w3, w2, next_norm_g):
    hc, cc, comb, cnt = routed
    runs, end, plan = _moe_plan(cnt[:, 0, :N_GROUPS].astype(jnp.int32))
    hs, cs = _dispatch(hc, cc, runs, end)
    ys = _moe_experts(hs, cs, w1, w3, w2, layer, plan)
    return _combine(x, mod, layer, comb, ys, runs, next_norm_g)


def kernel(x_prompt, x_sample, cache_k, cache_v, c, c_ctx, ada_w, ada_b, norm_mix_g, norm_ffn_g,
           mix_in_w, pool_w, pool_scale, conv_w, conv_b, conv_ln_g, conv_ln_b, mix_out_w,
           qkv_w, q_norm_g, k_norm_g, rpb, attn_out_w, router_g_w, router_g_b, router_e_w,
           router_e_b, exp_w1, exp_w3, exp_w2):
    x = (x_prompt.reshape(N_CTX, D), x_sample.reshape(N_LAT, D))
    cvec8 = jnp.concatenate([c_ctx[None, :], c, jnp.zeros((8 - 1 - N_LAT_SEQ, D), F32)], axis=0)
    mod = _adaln_all(cvec8, ada_w, ada_b)
    n_attn = DEPTH // 2
    cache_k2 = cache_k.reshape(N_LAT_SEQ, n_attn, CTX_SEQ, D)
    cache_v2 = cache_v.reshape(N_LAT_SEQ, n_attn, CTX_SEQ, D)

    caches = None
    h = None
    for l in range(DEPTH):
        e = l // 2
        if l % 2 == 0:
            u = _mm_prologue(x, mod, l, norm_mix_g[l], mix_in_w, e, h=h)
            mixed = _seq_mixer(u, pool_w[e].astype(BF16), pool_scale[e], conv_w[e], conv_b[e],
                               conv_ln_g[e], conv_ln_b[e])
            out_w = mix_out_w
        else:
            q, k, v, new_k, new_v = _qkv_proj(h, e, qkv_w, q_norm_g[e], k_norm_g[e], caches)
            caches = (new_k, new_v)
            mixed = _ctx_attention(q, k, v)
            mixed = _na_attention(q, k, v, cache_k2[:, e], cache_v2[:, e], rpb[e], mixed)
            out_w = attn_out_w
        rw, rb = _router_weights(router_g_w[l], router_g_b[l], router_e_w[l], router_e_b[l])
        x, *routed = _out_proj_and_route(mixed, out_w, e, x, mod, l, norm_ffn_g[l], rw, rb)
        last = l == DEPTH - 1
        x, h = _moe_experts_and_combine(x, routed, mod, l, exp_w1, exp_w3, exp_w2,
                                        next_norm_g=None if last else norm_mix_g[l + 1])

    y_prompt = x.reshape(N_CTX_SEQ, CTX_SEQ, D)
    y_sample = h.reshape(N_LAT_SEQ, LAT_SEQ, D)
    cache_dims = (N_CTX_SEQ, n_attn, CTX_SEQ, N_HEADS, HEAD_DIM)
    return (y_prompt, y_sample, caches[0].reshape(cache_dims), caches[1].reshape(cache_dims))
```

```python
import functools

import numpy as np
import jax
import jax.numpy as jnp
from jax import lax
from jax.experimental import pallas as pl
from jax.experimental.pallas import tpu as pltpu

D = 2048
N_CTX_SEQ = 16
CTX_SEQ = 256
N_LAT_SEQ = 2
LAT_SEQ = 1024
N_CTX = N_CTX_SEQ * CTX_SEQ
N_LAT = N_LAT_SEQ * LAT_SEQ
N_TOK = N_CTX + N_LAT
DEPTH = 4
GRID_W = 64
GRID_H = LAT_SEQ // GRID_W
N_HEADS = 16
HEAD_DIM = 128
ATTN_SCALE = HEAD_DIM ** -0.5
D_POOL = 1024
D_CONV = 1024
POOL_WINDOWS = (2, 4, 8, 16)
POOL_GC = 256
CONV_W = 31
NA_KH = 8
NA_KW = 16
N_GROUPS = 4
EXP_PER_GROUP = 4
N_EXPERTS = 16
D_EXPERT = 512
EPS = 1e-6
NEG = -1e30

SEQ_TILE = 256
HALO = 16
ROUTER_LANES = 128
EXPERT_LANE0 = N_GROUPS
POS_LANE = EXPERT_LANE0 + N_EXPERTS
LOW_LANE0 = 32
VMEM_LIMIT = 56 * 1024 * 1024

TOK_TILE = 256
N_TOK_TILES = N_TOK // TOK_TILE
ROW_ALIGN = 16
RUN_BITS = (256, 128, 64, 32, 16)
TILE_ROWS = TOK_TILE + N_GROUPS * ROW_ALIGN
MOE_TILE = 1024
MOE_SUB = 128
FF_CHUNK = 512
OUT_CHUNK = 512
N_FF_CHUNKS = EXP_PER_GROUP * D_EXPERT // FF_CHUNK
N_OUT_CHUNKS = D // OUT_CHUNK
ZERO_ROWS = MOE_SUB
MAX_PADDED = N_TOK + N_TOK_TILES * N_GROUPS * (ROW_ALIGN - 1)
MOE_TILES_MAX = (MAX_PADDED + N_GROUPS * (MOE_TILE - 1)) // MOE_TILE
ROW_CAP = -(-(MAX_PADDED + N_GROUPS * (ZERO_ROWS + MOE_TILE - 1)) // MOE_TILE) * MOE_TILE

F32 = jnp.float32
BF16 = jnp.bfloat16


def _cond_of_row(row):
    row = jnp.minimum(row, N_TOK - 1)
    return jnp.where(row < N_CTX, 0, 1 + (row - N_CTX) // LAT_SEQ)


def _mod_spec(layer, which, tm, tn=D, row_axis=1, col_axis=None):
    def index_map(*ids):
        cond = _cond_of_row(ids[row_axis] * tm)
        col = 0 if col_axis is None else ids[col_axis]
        return (layer, cond, which, 0, col)
    return pl.BlockSpec((None, None, None, 1, tn), index_map)


def _silu(x):
    return x / (1.0 + jnp.exp(-x))


def _modnorm(x, g, sc, sh):
    ms = jnp.mean(x * x, axis=-1, keepdims=True)
    y = x * lax.rsqrt(ms + EPS) * g
    return y * (1.0 + sc) + sh


def _params(*sem):
    return pltpu.CompilerParams(dimension_semantics=sem, vmem_limit_bytes=VMEM_LIMIT)


def _ada_kernel(c_ref, w_ref, b_ref, o_ref):
    s = _silu(c_ref[...]).astype(BF16)
    o_ref[...] = jnp.dot(s, w_ref[...].astype(BF16), preferred_element_type=F32) + b_ref[...]


N_MOD = 6 * D


def _mod_table(raw):
    return raw[:, :3].reshape(raw.shape[0], 3, 6, 1, D)


def _adaln_first_layer(cvec8, ada_w, ada_b):
    tn = 1024
    return pl.pallas_call(
        _ada_kernel,
        grid=(N_MOD // tn,),
        in_specs=[
            pl.BlockSpec((8, D), lambda j: (0, 0)),
            pl.BlockSpec((None, D, tn), lambda j: (0, 0, j)),
            pl.BlockSpec((None, 1, tn), lambda j: (0, 0, j)),
        ],
        out_specs=pl.BlockSpec((None, 8, tn), lambda j: (0, 0, j)),
        out_shape=jax.ShapeDtypeStruct((1, 8, N_MOD), F32),
        compiler_params=_params("arbitrary"),
        name="adaln",
    )(cvec8, ada_w, ada_b.reshape(DEPTH, 1, N_MOD))


def _token_specs(x, tm, tn, idx):
    if not isinstance(x, tuple):
        return [pl.BlockSpec((tm, tn), lambda j, i: idx(i, j))], [x]
    n_ctx_tiles = N_CTX // tm
    ctx_map = lambda j, i: idx(jnp.minimum(i, n_ctx_tiles - 1), j)
    lat_map = lambda j, i: idx(jnp.maximum(i - n_ctx_tiles, 0), j)
    return [pl.BlockSpec((tm, tn), ctx_map), pl.BlockSpec((tm, tn), lat_map)], list(x)


def _token_rows(refs, i, tm):
    if len(refs) == 1:
        return refs[0][...]
    return jnp.where(i < N_CTX // tm, refs[0][...], refs[1][...])


def _cast_weights_once(i, pairs):
    @pl.when(i == 0)
    def _():
        for w_ref, wb_ref in pairs:
            wb_ref[...] = w_ref[...].astype(BF16)


def _mm_pro_kernel(*refs, n_x, tm):
    x_refs = refs[:n_x]
    g_ref, sc_ref, sh_ref, w_ref, o_ref, wb_ref = refs[n_x:]
    i = pl.program_id(1)
    _cast_weights_once(i, [(w_ref, wb_ref)])
    h = _modnorm(_token_rows(x_refs, i, tm), g_ref[...], sc_ref[...], sh_ref[...]).astype(BF16)
    o_ref[...] = jnp.dot(h, wb_ref[...], preferred_element_type=F32)


def _mm_plain_kernel(h_ref, w_ref, o_ref, wb_ref):
    _cast_weights_once(pl.program_id(1), [(w_ref, wb_ref)])
    o_ref[...] = jnp.dot(h_ref[...], wb_ref[...], preferred_element_type=F32)


def _mm_prologue(x, mod, layer, norm_g, w, w_idx, h=None, tm=512, tn=1024):
    f = w.shape[2]
    if h is not None:
        tm = 2 * tm
    w_spec = pl.BlockSpec((None, D, tn), lambda j, i: (w_idx, 0, j))
    if h is None:
        x_specs, x_args = _token_specs(x, tm, D, lambda i, j: (i, 0))
        body = functools.partial(_mm_pro_kernel, n_x=len(x_args), tm=tm)
        in_specs = x_specs + [pl.BlockSpec((1, D), lambda j, i: (0, 0)),
                              _mod_spec(layer, 1, tm), _mod_spec(layer, 0, tm), w_spec]
        args = x_args + [norm_g.reshape(1, D), mod, mod, w]
    else:
        body = _mm_plain_kernel
        in_specs = [pl.BlockSpec((tm, D), lambda j, i: (i, 0)), w_spec]
        args = [h, w]
    return pl.pallas_call(
        body,
        grid=(f // tn, N_TOK // tm),
        in_specs=in_specs,
        out_specs=pl.BlockSpec((tm, tn), lambda j, i: (i, j)),
        out_shape=jax.ShapeDtypeStruct((N_TOK, f), F32),
        scratch_shapes=[pltpu.VMEM((D, tn), BF16)],
        compiler_params=_params("arbitrary", "arbitrary"),
        name="mm_prologue",
    )(*args)


def _qkv_kernel(h_ref, wq_ref, wk_ref, wv_ref, qg_ref, kg_ref, *rest, tn, n_ctx_tiles, seqs_per_tile):
    q_ref, k_ref, v_ref, ck_ref, cv_ref, wqb_ref, wkb_ref, wvb_ref, kn_ref, vf_ref = rest[-10:]
    i = pl.program_id(1)
    _cast_weights_once(i, [(wq_ref, wqb_ref), (wk_ref, wkb_ref), (wv_ref, wvb_ref)])
    h = h_ref[...]
    q = jnp.dot(h, wqb_ref[...], preferred_element_type=F32)
    k = jnp.dot(h, wkb_ref[...], preferred_element_type=F32)
    v = jnp.dot(h, wvb_ref[...], preferred_element_type=F32)
    for hh in range(tn // HEAD_DIM):
        ls = slice(hh * HEAD_DIM, (hh + 1) * HEAD_DIM)
        qh = q[:, ls]
        kh = k[:, ls]
        qn = qh * lax.rsqrt(jnp.mean(qh * qh, axis=-1, keepdims=True) + EPS) * qg_ref[:, ls]
        kn = kh * lax.rsqrt(jnp.mean(kh * kh, axis=-1, keepdims=True) + EPS) * kg_ref[:, ls]
        q_ref[:, ls] = qn.astype(BF16)
        k_ref[:, ls] = kn.astype(BF16)
        kn_ref[:, ls] = kn
    v_ref[...] = v.astype(BF16)
    vf_ref[...] = v

    @pl.when(i < n_ctx_tiles)
    def _():
        ck_ref[...] = kn_ref[...].reshape(seqs_per_tile, CTX_SEQ, tn)
        cv_ref[...] = vf_ref[...].reshape(seqs_per_tile, CTX_SEQ, tn)


def _qkv_proj(h, w_idx, w, q_gain, k_gain, caches, tm=512, tn=512):
    n = h.shape[0]
    n_attn = DEPTH // 2
    ncol = D // tn
    seqs_per_tile = tm // CTX_SEQ
    n_ctx_tiles = N_CTX // tm
    cache_shape = jax.ShapeDtypeStruct((N_CTX_SEQ, n_attn, CTX_SEQ, D), F32)
    cache_spec = pl.BlockSpec(
        (seqs_per_tile, None, CTX_SEQ, tn),
        lambda j, i: (jnp.minimum(i, n_ctx_tiles - 1), w_idx, 0, j))
    in_specs = [
        pl.BlockSpec((tm, D), lambda j, i: (i, 0)),
        pl.BlockSpec((None, D, tn), lambda j, i: (w_idx, 0, j)),
        pl.BlockSpec((None, D, tn), lambda j, i: (w_idx, 0, ncol + j)),
        pl.BlockSpec((None, D, tn), lambda j, i: (w_idx, 0, 2 * ncol + j)),
        pl.BlockSpec((1, tn), lambda j, i: (0, j)),
        pl.BlockSpec((1, tn), lambda j, i: (0, j)),
    ]
    args = [h, w, w, w,
            jnp.tile(q_gain, N_HEADS).reshape(1, D), jnp.tile(k_gain, N_HEADS).reshape(1, D)]
    aliases = {}
    if caches is not None:
        in_specs += [pl.BlockSpec(memory_space=pl.ANY), pl.BlockSpec(memory_space=pl.ANY)]
        aliases = {len(args): 3, len(args) + 1: 4}
        args += list(caches)
    act_spec = pl.BlockSpec((tm, tn), lambda j, i: (i, j))
    act_shape = jax.ShapeDtypeStruct((n, D), BF16)
    return pl.pallas_call(
        functools.partial(_qkv_kernel, tn=tn, n_ctx_tiles=n_ctx_tiles, seqs_per_tile=seqs_per_tile),
        grid=(ncol, n // tm),
        in_specs=in_specs,
        out_specs=[act_spec, act_spec, act_spec, cache_spec, cache_spec],
        out_shape=[act_shape, act_shape, act_shape, cache_shape, cache_shape],
        input_output_aliases=aliases,
        scratch_shapes=[pltpu.VMEM((D, tn), BF16)] * 3 + [pltpu.VMEM((tm, tn), F32)] * 2,
        compiler_params=_params("arbitrary", "arbitrary"),
        name="qkv_proj",
    )(*args)


def _seq_mix_tile(i, u_ref, top_ref, bot_ref, pw_ref, ps_ref, cw_ref, cb_ref, lg_ref, lb_ref,
                  o_ref, zpad_ref, conv_ref):
    n_ctx_tiles = N_CTX // SEQ_TILE
    tiles_per_lat = LAT_SEQ // SEQ_TILE
    is_lat = i >= n_ctx_tiles
    chunk = jnp.where(is_lat, (i - n_ctx_tiles) % tiles_per_lat, 0)
    top_ok = jnp.logical_and(is_lat, chunk > 0)
    bot_ok = jnp.logical_and(is_lat, chunk < tiles_per_lat - 1)
    seq_len = jnp.where(is_lat, LAT_SEQ, CTX_SEQ)
    t = chunk * SEQ_TILE + lax.broadcasted_iota(jnp.int32, (SEQ_TILE, 1), 0)
    rows = SEQ_TILE + 2 * HALO

    for g, win in enumerate(POOL_WINDOWS):
        ls = slice(g * POOL_GC, (g + 1) * POOL_GC)
        mid = u_ref[:, ls]
        top = jnp.where(top_ok, top_ref[:, ls], 0.0)
        bot = jnp.where(bot_ok, bot_ref[:, ls], 0.0)
        up = jnp.concatenate([top, mid, bot], axis=0)
        s = pltpu.roll(up, 1, 0) + up
        if win >= 4:
            s = pltpu.roll(s, 1, 0) + pltpu.roll(s, rows - 1, 0)
        if win >= 8:
            s = pltpu.roll(s, 2, 0) + pltpu.roll(s, rows - 2, 0)
        if win >= 16:
            s = pltpu.roll(s, 4, 0) + pltpu.roll(s, rows - 4, 0)
        s = s[HALO:HALO + SEQ_TILE]
        lo = jnp.maximum(t - win // 2, 0)
        hi = jnp.minimum(t + (win - win // 2) - 1, seq_len - 1)
        cnt = (hi - lo + 1).astype(F32)
        p = (s / cnt - mid).astype(BF16)
        y = jnp.dot(p, pw_ref[g], preferred_element_type=F32) * ps_ref[:, ls]
        o_ref[:, ls] = y.astype(BF16)

    def glu(ref):
        return ref[:, D_POOL:D_POOL + D_CONV] * (1.0 / (1.0 + jnp.exp(-ref[:, D_POOL + D_CONV:])))

    zpad_ref[0:HALO, :] = jnp.where(top_ok, glu(top_ref), 0.0)
    zpad_ref[HALO:HALO + SEQ_TILE, :] = glu(u_ref)
    zpad_ref[HALO + SEQ_TILE:rows, :] = jnp.where(bot_ok, glu(bot_ref), 0.0)
    for lt in range(D_CONV // 128):
        ls = slice(lt * 128, (lt + 1) * 128)
        zp = zpad_ref[:, ls]
        acc = jnp.zeros((SEQ_TILE, 128), F32)
        for b in range(8):
            sb = zp if b == 0 else pltpu.roll(zp, rows - b, 0)
            for a in range(4):
                off = 8 * a + b
                if 1 <= off <= CONV_W:
                    acc = acc + cw_ref[off - 1:off, ls] * sb[8 * a:8 * a + SEQ_TILE]
        conv_ref[:, ls] = acc + cb_ref[:, ls]
    zc = conv_ref[...]
    mu = jnp.mean(zc, axis=-1, keepdims=True)
    d = zc - mu
    var = jnp.mean(d * d, axis=-1, keepdims=True)
    zn = d * lax.rsqrt(var + EPS) * lg_ref[...] + lb_ref[...]
    o_ref[:, D_POOL:] = _silu(zn).astype(BF16)


N_SEQ_INPUTS = 9


def _seq_kernel(*refs, with_ada):
    seq_in, refs = refs[:N_SEQ_INPUTS], refs[N_SEQ_INPUTS:]
    if with_ada:
        (c_ref, w_ref, b_ref), refs = refs[:3], refs[3:]
        o_ref, mod_ref, zpad_ref, conv_ref = refs
        _ada_kernel(c_ref, w_ref, b_ref, mod_ref)
    else:
        o_ref, zpad_ref, conv_ref = refs
    _seq_mix_tile(pl.program_id(0), *seq_in, o_ref, zpad_ref, conv_ref)


def _seq_mixer(u, pool_w_bf16, pool_scale, conv_w, conv_b, ln_g, ln_b, ada=None):
    n = u.shape[0]
    f_in = u.shape[1]
    n_steps = n // SEQ_TILE
    hb = SEQ_TILE // HALO
    n_hblocks = n // HALO
    const = lambda *shape: pl.BlockSpec(shape, lambda i: (0,) * len(shape))
    in_specs = [
        pl.BlockSpec((SEQ_TILE, f_in), lambda i: (i, 0)),
        pl.BlockSpec((HALO, f_in), lambda i: (jnp.maximum(i * hb - 1, 0), 0)),
        pl.BlockSpec((HALO, f_in), lambda i: (jnp.minimum((i + 1) * hb, n_hblocks - 1), 0)),
        const(len(POOL_WINDOWS), POOL_GC, POOL_GC),
        const(1, D_POOL), const(CONV_W, D_CONV), const(1, D_CONV), const(1, D_CONV), const(1, D_CONV),
    ]
    args = [u, u, u, pool_w_bf16, pool_scale.reshape(1, D_POOL), conv_w, conv_b.reshape(1, D_CONV),
            ln_g.reshape(1, D_CONV), ln_b.reshape(1, D_CONV)]
    out_specs = [pl.BlockSpec((SEQ_TILE, D_POOL + D_CONV), lambda i: (i, 0))]
    out_shape = [jax.ShapeDtypeStruct((n, D_POOL + D_CONV), BF16)]
    if ada is not None:
        cvec8, ada_w, ada_b = ada
        later = DEPTH - 1
        chunk = later * N_MOD // n_steps
        per_layer = N_MOD // chunk
        chunk_map = lambda i: (1 + i // per_layer, 0, i % per_layer)
        in_specs += [const(8, D),
                     pl.BlockSpec((None, D, chunk), chunk_map),
                     pl.BlockSpec((None, 1, chunk), chunk_map)]
        args += [cvec8, ada_w, ada_b.reshape(DEPTH, 1, N_MOD)]
        out_specs.append(pl.BlockSpec((None, 8, chunk), lambda i: (i // per_layer, 0, i % per_layer)))
        out_shape.append(jax.ShapeDtypeStruct((later, 8, N_MOD), F32))
    return pl.pallas_call(
        functools.partial(_seq_kernel, with_ada=ada is not None),
        grid=(n_steps,),
        in_specs=in_specs,
        out_specs=out_specs,
        out_shape=out_shape,
        scratch_shapes=[
            pltpu.VMEM((SEQ_TILE + 2 * HALO, D_CONV), F32),
            pltpu.VMEM((SEQ_TILE, D_CONV), F32),
        ],
        compiler_params=_params("arbitrary"),
        name="seq_mixer",
    )(*args)


_NT = (((1,), (1,)), ((), ()))


def _ctx_attn_kernel(q_ref, k_ref, v_ref, o_ref, s_ref, p_ref):
    for h in range(N_HEADS):
        ls = slice(h * HEAD_DIM, (h + 1) * HEAD_DIM)
        s_ref[h] = lax.dot_general(q_ref[:, ls], k_ref[:, ls], _NT, preferred_element_type=F32) * ATTN_SCALE
    for h in range(N_HEADS):
        s = s_ref[h]
        e = jnp.exp(s - jnp.max(s, axis=-1, keepdims=True))
        p_ref[h] = (e * (1.0 / jnp.sum(e, axis=-1, keepdims=True))).astype(BF16)
    for h in range(N_HEADS):
        ls = slice(h * HEAD_DIM, (h + 1) * HEAD_DIM)
        o_ref[:, ls] = jnp.dot(p_ref[h], v_ref[:, ls], preferred_element_type=F32).astype(BF16)


def _ctx_attention(q, k, v):
    n = q.shape[0]
    spec = pl.BlockSpec((CTX_SEQ, D), lambda b: (b, 0))
    return pl.pallas_call(
        _ctx_attn_kernel,
        grid=(N_CTX_SEQ,),
        in_specs=[spec, spec, spec],
        out_specs=spec,
        out_shape=jax.ShapeDtypeStruct((n, D), BF16),
        scratch_shapes=[pltpu.VMEM((N_HEADS, CTX_SEQ, CTX_SEQ), F32),
                        pltpu.VMEM((N_HEADS, CTX_SEQ, CTX_SEQ), BF16)],
        compiler_params=_params("arbitrary"),
        name="ctx_attention",
    )(q, k, v)


def _na_window_mask():
    cq = np.arange(GRID_W)[:, None]
    kc = np.arange(GRID_W)[None, :]
    win0 = np.clip(cq - NA_KW // 2, 0, GRID_W - NA_KW)
    mask = ((kc >= win0) & (kc < win0 + NA_KW)).astype(np.float32)
    return np.tile(mask, (1, NA_KH))


def _na_bias_rows(rpb_e):
    centre = NA_KW - 1
    pad = jnp.zeros(rpb_e.shape[:2] + (128 - (2 * NA_KW - 1),), F32)
    return jnp.concatenate([rpb_e[..., centre:], pad, rpb_e[..., :centre]], axis=-1)


def _na_key_row0(r):
    return min(max(r - NA_KH // 2, 0), GRID_H - NA_KH)


def _na_segments():
    segs = []
    for r in range(GRID_H):
        kr0 = _na_key_row0(r)
        if segs and segs[-1][2] == kr0:
            segs[-1][1] = r + 1
        else:
            segs.append([r, r + 1, kr0])
    return segs


def _na_kernel(q_ref, k_ref, v_ref, kc_ref, vc_ref, rows_ref, mask_ref, o_in_ref, o_ref,
               s_ref, p_ref, acc_ref, tz_ref):
    del o_in_ref
    band = NA_KH * GRID_W
    n_ctx = CTX_SEQ
    kc = kc_ref[...].astype(BF16)
    vc = vc_ref[...].astype(BF16)
    @pl.when(pl.program_id(1) == 0)
    def _():
        n_dr = 2 * NA_KH - 1
        toeplitz = [pltpu.roll(jnp.broadcast_to(rows_ref[d:d + 1, :], (GRID_W, 2 * GRID_W)), 0, 1,
                               stride=1, stride_axis=0) for d in range(n_dr)]
        left = lax.broadcasted_iota(jnp.int32, (GRID_W, 2 * GRID_W), 1) < GRID_W
        for d in range(n_dr - 1):
            tz_ref[d] = jnp.where(left, toeplitz[d], pltpu.roll(toeplitz[d + 1], GRID_W, 1))
    for r0, r1, kr0 in _na_segments():
        rs = slice(r0 * GRID_W, r1 * GRID_W)
        kb = k_ref[kr0 * GRID_W:kr0 * GRID_W + band, :]
        s_ref[rs, 0:band] = lax.dot_general(q_ref[rs, :], kb, _NT, preferred_element_type=F32) * ATTN_SCALE
    s_ref[:, band:band + n_ctx] = lax.dot_general(q_ref[...], kc, _NT, preferred_element_type=F32) * ATTN_SCALE
    mask = mask_ref[...] > 0.5
    for r in range(GRID_H):
        rs = slice(r * GRID_W, (r + 1) * GRID_W)
        kr0 = _na_key_row0(r)
        bias = jnp.concatenate(
            [tz_ref[kr0 + j - r + NA_KH - 1] for j in range(0, NA_KH, 2)], axis=-1)
        s_loc = jnp.where(mask, s_ref[rs, 0:band] + bias, NEG)
        s_ctx = s_ref[rs, band:band + n_ctx]
        m = jnp.maximum(jnp.max(s_loc, axis=-1, keepdims=True), jnp.max(s_ctx, axis=-1, keepdims=True))
        e_loc = jnp.exp(s_loc - m)
        e_ctx = jnp.exp(s_ctx - m)
        inv = 1.0 / (jnp.sum(e_loc, axis=-1, keepdims=True) + jnp.sum(e_ctx, axis=-1, keepdims=True))
        p_ref[rs, 0:band] = (e_loc * inv).astype(BF16)
        p_ref[rs, band:band + n_ctx] = (e_ctx * inv).astype(BF16)
    for r0, r1, kr0 in _na_segments():
        rs = slice(r0 * GRID_W, r1 * GRID_W)
        vb = v_ref[kr0 * GRID_W:kr0 * GRID_W + band, :]
        acc_ref[rs, :] = jnp.dot(p_ref[rs, 0:band], vb, preferred_element_type=F32)
    o_ctx = jnp.dot(p_ref[:, band:band + n_ctx], vc, preferred_element_type=F32)
    o_ref[...] = (acc_ref[...] + o_ctx).astype(BF16)


def _na_attention(q, k, v, cache_k_e, cache_v_e, rpb_e, o_ctx):
    mask = _na_window_mask()
    lat0 = N_CTX // LAT_SEQ
    n_dr = 2 * NA_KH - 1
    n_keys = NA_KH * GRID_W + CTX_SEQ
    tok_spec = pl.BlockSpec((LAT_SEQ, HEAD_DIM), lambda h, b: (lat0 + b, h))
    cache_spec = pl.BlockSpec((None, CTX_SEQ, HEAD_DIM), lambda h, b: (b, 0, h))
    return pl.pallas_call(
        _na_kernel,
        grid=(N_HEADS, N_LAT_SEQ),
        in_specs=[
            tok_spec, tok_spec, tok_spec, cache_spec, cache_spec,
            pl.BlockSpec((None, n_dr, 2 * GRID_W), lambda h, b: (h, 0, 0)),
            pl.BlockSpec((GRID_W, NA_KH * GRID_W), lambda h, b: (0, 0)),
            pl.BlockSpec(memory_space=pl.ANY),
        ],
        out_specs=tok_spec,
        out_shape=jax.ShapeDtypeStruct(o_ctx.shape, o_ctx.dtype),
        input_output_aliases={7: 0},
        scratch_shapes=[pltpu.VMEM((LAT_SEQ, n_keys), F32),
                        pltpu.VMEM((LAT_SEQ, n_keys), BF16),
                        pltpu.VMEM((LAT_SEQ, HEAD_DIM), F32),
                        pltpu.VMEM((n_dr - 1, GRID_W, 2 * GRID_W), F32)],
        compiler_params=_params("arbitrary", "arbitrary"),
        name="na_attention",
    )(q, k, v, cache_k_e, cache_v_e, _na_bias_rows(rpb_e), jnp.asarray(mask), o_ctx)


def _split_bf16(x, pieces):
    out = []
    for _ in range(pieces):
        p = x.astype(BF16)
        out.append(p)
        x = x - p.astype(F32)
    return out


def _moe_prenorm(x, g_ref, sc_ref, sh_ref):
    h = _modnorm(x, g_ref[...], sc_ref[...], sh_ref[...])
    return jnp.concatenate(_split_bf16(h, 2), axis=1)


def _router_logits(h_pieces, rw_ref, rb_ref):
    prod = jnp.dot(h_pieces, rw_ref[...], preferred_element_type=F32)
    return prod + pltpu.roll(prod, ROUTER_LANES - LOW_LANE0, 1) + rb_ref[...]


def _first_argmax(vals, valid, lane):
    v = jnp.where(valid, vals, -jnp.inf)
    m = jnp.max(v, axis=-1, keepdims=True)
    idx = jnp.min(jnp.where(jnp.logical_and(valid, v == m), lane, jnp.int32(1 << 20)),
                  axis=-1, keepdims=True)
    return m, idx


def _route_group(logits):
    lane = lax.broadcasted_iota(jnp.int32, logits.shape, 1)
    mg, gi = _first_argmax(logits, lane < N_GROUPS, lane)
    onehot_g = jnp.where(lane == gi, 1.0, 0.0)
    row = lax.broadcasted_iota(jnp.int32, (TOK_TILE, TOK_TILE), 0)
    col = lax.broadcasted_iota(jnp.int32, (TOK_TILE, TOK_TILE), 1)
    earlier = jnp.where(col < row, 1.0, 0.0).astype(BF16)
    rank = jnp.dot(earlier, onehot_g.astype(BF16), preferred_element_type=F32)
    return mg, gi, onehot_g, rank


def _route_weights(logits, group, comb_ref, cnt_ref):
    mg, gi, onehot_g, rank = group
    lane = lax.broadcasted_iota(jnp.int32, logits.shape, 1)
    is_g = lane < N_GROUPS
    pg = 1.0 / jnp.sum(jnp.where(is_g, jnp.exp(logits - mg), 0.0), axis=-1, keepdims=True)
    e_lane0 = EXPERT_LANE0 + gi * EXP_PER_GROUP
    is_e = jnp.logical_and(lane >= e_lane0, lane < e_lane0 + EXP_PER_GROUP)
    m1, i1 = _first_argmax(logits, is_e, lane)
    m2, i2 = _first_argmax(logits, jnp.logical_and(is_e, lane != i1), lane)
    e2 = jnp.exp(m2 - m1)
    den = 1.0 + e2
    w1 = (1.0 / den) * pg
    w2 = (e2 / den) * pg

    cnt = jnp.sum(onehot_g, axis=0, keepdims=True)
    padded = jnp.floor((cnt + (ROW_ALIGN - 1)) * (1.0 / ROW_ALIGN)) * ROW_ALIGN
    padded8 = jnp.broadcast_to(padded, (8, ROUTER_LANES))
    run0 = (pltpu.roll(padded8, 1, 1) + pltpu.roll(padded8, 2, 1) + pltpu.roll(padded8, 3, 1))[0:1, :]
    pos = jnp.sum(onehot_g * (rank + run0), axis=-1, keepdims=True)

    comb = jnp.where(lane == i1, w1, jnp.where(lane == i2, w2, jnp.where(lane == POS_LANE, pos, onehot_g)))
    comb_ref[...] = comb
    cnt_ref[...] = cnt
    return comb


def _sort_tile(h_hi, comb, hc_ref, cc_ref):
    pos_t = comb.T[POS_LANE:POS_LANE + 1, :]
    dest = lax.broadcasted_iota(jnp.int32, (TILE_ROWS, TOK_TILE), 0).astype(F32)
    perm = jnp.where(pos_t == dest, 1.0, 0.0).astype(BF16)
    hc_ref[...] = jnp.dot(perm, h_hi, preferred_element_type=F32).astype(BF16)
    c_parts = _split_bf16(comb, 3)
    cc_ref[...] = (jnp.dot(perm, c_parts[0], preferred_element_type=F32)
                   + jnp.dot(perm, c_parts[1], preferred_element_type=F32)
                   + jnp.dot(perm, c_parts[2], preferred_element_type=F32))


W_STAGE_COLS = 512


def _out_route_kernel(a_ref, w_hbm, gate_ref, g_ref, sc_ref, sh_ref, rw_ref, rb_ref, *refs, n_x, w_idx):
    x_refs = refs[:n_x]
    xo_ref, hc_ref, cc_ref, comb_ref, cnt_ref, wb_ref, stage_ref, prev_ref, sem = refs[n_x:]
    i = pl.program_id(0)

    @pl.when(i == 0)
    def _():
        prev_ref[...] = jnp.zeros_like(prev_ref)
        n_chunks = D // W_STAGE_COLS

        def chunk_copy(c):
            cols = pl.ds(c * W_STAGE_COLS, W_STAGE_COLS)
            return pltpu.make_async_copy(w_hbm.at[w_idx, :, cols], stage_ref.at[c % 2], sem.at[c % 2])

        chunk_copy(0).start()
        chunk_copy(1).start()
        for c in range(n_chunks):
            chunk_copy(c).wait()
            wb_ref[:, c * W_STAGE_COLS:(c + 1) * W_STAGE_COLS] = stage_ref[c % 2].astype(BF16)
            if c + 2 < n_chunks:
                chunk_copy(c + 2).start()

    h_prev = prev_ref[...]
    logits = _router_logits(h_prev, rw_ref, rb_ref)
    group = _route_group(logits)
    y = jnp.dot(a_ref[...], wb_ref[...], preferred_element_type=F32)
    x_mid = _token_rows(x_refs, jnp.minimum(i, N_TOK_TILES - 1), TOK_TILE) + gate_ref[...] * y
    xo_ref[...] = x_mid
    comb = _route_weights(logits, group, comb_ref, cnt_ref)
    _sort_tile(h_prev[:, :D], comb, hc_ref, cc_ref)
    prev_ref[...] = _moe_prenorm(x_mid, g_ref, sc_ref, sh_ref)


def _out_proj_and_route(a_bf16, w, w_idx, x, mod, layer, norm_g, rw, rb):
    tm = TOK_TILE
    k = a_bf16.shape[1]
    last = N_TOK_TILES - 1
    proj = lambda i: jnp.minimum(i, last)
    routed = lambda i: jnp.maximum(i - 1, 0)
    if isinstance(x, tuple):
        n_ctx_tiles = N_CTX // tm
        x_specs = [pl.BlockSpec((tm, D), lambda i: (jnp.minimum(proj(i), n_ctx_tiles - 1), 0)),
                   pl.BlockSpec((tm, D), lambda i: (jnp.maximum(proj(i) - n_ctx_tiles, 0), 0))]
        x_args = list(x)
    else:
        x_specs = [pl.BlockSpec((tm, D), lambda i: (proj(i), 0))]
        x_args = [x]

    def mod_of_proj(which):
        return pl.BlockSpec((None, None, None, 1, D),
                            lambda i: (layer, _cond_of_row(proj(i) * tm), which, 0, 0))

    return pl.pallas_call(
        functools.partial(_out_route_kernel, n_x=len(x_args), w_idx=w_idx),
        grid=(N_TOK_TILES + 1,),
        in_specs=[
            pl.BlockSpec((tm, k), lambda i: (proj(i), 0)),
            pl.BlockSpec(memory_space=pl.ANY),
            mod_of_proj(2),
            pl.BlockSpec((1, D), lambda i: (0, 0)),
            mod_of_proj(4),
            mod_of_proj(3),
            pl.BlockSpec((2 * D, ROUTER_LANES), lambda i: (0, 0)),
            pl.BlockSpec((1, ROUTER_LANES), lambda i: (0, 0)),
        ] + x_specs,
        out_specs=[
            pl.BlockSpec((tm, D), lambda i: (proj(i), 0)),
            pl.BlockSpec((None, TILE_ROWS, D), lambda i: (routed(i), 0, 0)),
            pl.BlockSpec((None, TILE_ROWS, ROUTER_LANES), lambda i: (routed(i), 0, 0)),
            pl.BlockSpec((tm, ROUTER_LANES), lambda i: (routed(i), 0)),
            pl.BlockSpec((None, 1, ROUTER_LANES), lambda i: (routed(i), 0, 0)),
        ],
        out_shape=[
            jax.ShapeDtypeStruct((N_TOK, D), F32),
            jax.ShapeDtypeStruct((N_TOK_TILES, TILE_ROWS, D), BF16),
            jax.ShapeDtypeStruct((N_TOK_TILES, TILE_ROWS, ROUTER_LANES), F32),
            jax.ShapeDtypeStruct((N_TOK, ROUTER_LANES), F32),
            jax.ShapeDtypeStruct((N_TOK_TILES, 1, ROUTER_LANES), F32),
        ],
        scratch_shapes=[
            pltpu.VMEM((k, D), BF16),
            pltpu.VMEM((2, k, W_STAGE_COLS), F32),
            pltpu.VMEM((tm, 2 * D), BF16),
            pltpu.SemaphoreType.DMA((2,)),
        ],
        compiler_params=_params("arbitrary"),
        name="out_proj_route",
    )(a_bf16, w, mod, norm_g.reshape(1, D), mod, mod, rw, rb, *x_args)


def _router_weights(rgw, rgb, rew, reb):
    n_out = N_GROUPS + N_EXPERTS
    rw = jnp.concatenate([rgw, rew.reshape(D, N_EXPERTS)], axis=1)
    hi = rw.astype(BF16)
    lo = (rw - hi.astype(F32)).astype(BF16)
    zeros = lambda n: jnp.zeros((D, n), BF16)
    top = jnp.concatenate([hi, zeros(LOW_LANE0 - n_out), lo, zeros(ROUTER_LANES - LOW_LANE0 - n_out)], axis=1)
    bottom = jnp.concatenate([hi, zeros(ROUTER_LANES - n_out)], axis=1)
    rb = jnp.concatenate([rgb, reb.reshape(N_EXPERTS)])
    rb = jnp.pad(rb, (0, ROUTER_LANES - n_out)).reshape(1, ROUTER_LANES)
    return jnp.concatenate([top, bottom], axis=0), rb


def _moe_plan(cnt):
    i32 = jnp.int32
    padded = (cnt + (ROW_ALIGN - 1)) // ROW_ALIGN * ROW_ALIGN
    run0 = jnp.cumsum(padded, axis=1) - padded
    length = jnp.sum(padded, axis=0)
    span = (length + ZERO_ROWS + MOE_TILE - 1) // MOE_TILE * MOE_TILE
    start = jnp.cumsum(span) - span
    off = start[None, :] + jnp.cumsum(padded, axis=0) - padded
    end = start + length
    need = length
    ntile = (need + MOE_TILE - 1) // MOE_TILE
    cum = jnp.cumsum(ntile)
    total = cum[-1]
    k = jnp.arange(MOE_TILES_MAX, dtype=i32)
    kk = jnp.minimum(k, jnp.maximum(total - 1, 0))
    grp = jnp.minimum(jnp.sum((kk[:, None] >= cum[None, :]).astype(i32), axis=1), N_GROUPS - 1)
    j = kk - (cum - ntile)[grp]
    blk = start[grp] // MOE_TILE + j
    nsub = jnp.clip((need[grp] - j * MOE_TILE + MOE_SUB - 1) // MOE_SUB, 0, MOE_TILE // MOE_SUB)
    nsub = jnp.where(k < total, nsub, 0)
    runs = tuple(a.reshape(-1).astype(i32) for a in (off, run0, padded))
    return runs, end.astype(i32), (blk.astype(i32), grp.astype(i32), nsub.astype(i32))


def _run_pieces(run_refs, tile, g):
    off_ref, run0_ref, padded_ref = run_refs
    idx = tile * N_GROUPS + g
    n = padded_ref[idx]
    src0 = run0_ref[idx]
    dst0 = off_ref[idx]
    pieces = []
    for b, bit in enumerate(RUN_BITS):
        done = n & (-2 * bit)
        pieces.append((b, (n & bit) != 0, pl.multiple_of(src0 + done, ROW_ALIGN),
                       pl.multiple_of(dst0 + done, ROW_ALIGN), bit))
    return pieces


def _dispatch_kernel(off_ref, run0_ref, padded_ref, end_ref, hc_ref, cc_ref, hs_ref, cs_ref,
                     hb_ref, cb_ref, zh_ref, zc_ref, sem, sem_z):
    run_refs = (off_ref, run0_ref, padded_ref)
    i = pl.program_id(0)
    n_steps = pl.num_programs(0)
    slot = i % 2

    def for_each_copy(tile, sl, fn):
        for g in range(N_GROUPS):
            for b, pred, src, dst, rows in _run_pieces(run_refs, tile, g):
                @pl.when(pred)
                def _():
                    fn(pltpu.make_async_copy(hb_ref.at[sl, pl.ds(src, rows), :],
                                             hs_ref.at[pl.ds(dst, rows), :], sem.at[sl, 0, g, b]))
                    fn(pltpu.make_async_copy(cb_ref.at[sl, pl.ds(src, rows), :],
                                             cs_ref.at[pl.ds(dst, rows), :], sem.at[sl, 1, g, b]))

    def zero_copies():
        cps = []
        for g in range(N_GROUPS):
            r0 = pl.multiple_of(end_ref[g], ROW_ALIGN)
            cps.append(pltpu.make_async_copy(zh_ref, hs_ref.at[pl.ds(r0, ZERO_ROWS), :], sem_z.at[0, g]))
            cps.append(pltpu.make_async_copy(zc_ref, cs_ref.at[pl.ds(r0, ZERO_ROWS), :], sem_z.at[1, g]))
        return cps

    @pl.when(i == 0)
    def _():
        zh_ref[...] = jnp.zeros_like(zh_ref)
        zc_ref[...] = jnp.zeros_like(zc_ref)
        for cp in zero_copies():
            cp.start()

    hb_ref[slot] = hc_ref[...]
    cb_ref[slot] = cc_ref[...]
    for_each_copy(i, slot, lambda cp: cp.start())

    @pl.when(i > 0)
    def _():
        for_each_copy(i - 1, 1 - slot, lambda cp: cp.wait())

    @pl.when(i == n_steps - 1)
    def _():
        for_each_copy(i, slot, lambda cp: cp.wait())
        for cp in zero_copies():
            cp.wait()


def _dispatch(hc, cc, runs, end):
    any_spec = pl.BlockSpec(memory_space=pl.ANY)
    grid_spec = pltpu.PrefetchScalarGridSpec(
        num_scalar_prefetch=4,
        grid=(N_TOK_TILES,),
        in_specs=[pl.BlockSpec((None, TILE_ROWS, D), lambda i, *_: (i, 0, 0)),
                  pl.BlockSpec((None, TILE_ROWS, ROUTER_LANES), lambda i, *_: (i, 0, 0))],
        out_specs=[any_spec, any_spec],
        scratch_shapes=[
            pltpu.VMEM((2, TILE_ROWS, D), BF16),
            pltpu.VMEM((2, TILE_ROWS, ROUTER_LANES), F32),
            pltpu.VMEM((ZERO_ROWS, D), BF16),
            pltpu.VMEM((ZERO_ROWS, ROUTER_LANES), F32),
            pltpu.SemaphoreType.DMA((2, 2, N_GROUPS, len(RUN_BITS))),
            pltpu.SemaphoreType.DMA((2, N_GROUPS)),
        ],
    )
    return pl.pallas_call(
        _dispatch_kernel,
        grid_spec=grid_spec,
        out_shape=[jax.ShapeDtypeStruct((ROW_CAP, D), BF16),
                   jax.ShapeDtypeStruct((ROW_CAP, ROUTER_LANES), F32)],
        compiler_params=_params("arbitrary"),
        name="moe_dispatch",
    )(*runs, end, hc, cc)


def _moe_kernel(blk_ref, grp_ref, nsub_ref, h_ref, c_ref, w1_ref, w3_ref, w2_ref, o_ref, hid_ref):
    k = pl.program_id(0)
    c = pl.program_id(1)
    ns = nsub_ref[k]
    n_sub_max = MOE_TILE // MOE_SUB

    def for_row_count(fn):
        for m in range(1, n_sub_max + 1):
            @pl.when(ns == m)
            def _():
                fn(m * MOE_SUB)

    @pl.when(jnp.logical_and(ns > 0, c < N_FF_CHUNKS))
    def _():
        w1 = w1_ref[...].astype(BF16)
        w3 = w3_ref[...].astype(BF16)
        e_lane = EXPERT_LANE0 + grp_ref[k] * EXP_PER_GROUP + c * FF_CHUNK // D_EXPERT

        def up(rows):
            h = h_ref[0:rows, :]
            a = jnp.dot(h, w1, preferred_element_type=F32)
            b = jnp.dot(h, w3, preferred_element_type=F32)
            comb = c_ref[0:rows, :]
            lane = lax.broadcasted_iota(jnp.int32, comb.shape, 1)
            cw = jnp.sum(jnp.where(lane == e_lane, comb, 0.0), axis=-1, keepdims=True)
            hid_ref[c, 0:rows, :] = (_silu(a) * b * cw).astype(BF16)

        for_row_count(up)

    @pl.when(jnp.logical_and(ns > 0, c >= N_FF_CHUNKS))
    def _():
        w2 = w2_ref[...].reshape(EXP_PER_GROUP * D_EXPERT, OUT_CHUNK).astype(BF16)

        def down(rows):
            hid = jnp.concatenate([hid_ref[j, 0:rows, :] for j in range(N_FF_CHUNKS)], axis=1)
            o_ref[0:rows, :] = jnp.dot(hid, w2, preferred_element_type=F32).astype(BF16)
            if rows < MOE_TILE:
                o_ref[rows:, :] = jnp.zeros((MOE_TILE - rows, OUT_CHUNK), BF16)

        for_row_count(down)


def _moe_experts(hs, cs, w1, w3, w2, layer, plan):
    per_e = D_EXPERT // FF_CHUNK
    n_steps = N_FF_CHUNKS + N_OUT_CHUNKS

    def step_of(k, c, nsub_ref):
        return jnp.where(nsub_ref[k] > 0, c, n_steps - 1)

    def w13_map(k, c, blk_ref, grp_ref, nsub_ref):
        cc = jnp.minimum(step_of(k, c, nsub_ref), N_FF_CHUNKS - 1)
        return (layer, grp_ref[k] * EXP_PER_GROUP + cc // per_e, 0, cc % per_e)

    def out_chunk(k, c, nsub_ref):
        return jnp.maximum(step_of(k, c, nsub_ref) - N_FF_CHUNKS, 0)

    w2_map = lambda k, c, blk_ref, grp_ref, nsub_ref: (layer, grp_ref[k], 0, out_chunk(k, c, nsub_ref))
    row_map = lambda k, c, blk_ref, grp_ref, nsub_ref: (blk_ref[k], 0)
    out_map = lambda k, c, blk_ref, grp_ref, nsub_ref: (blk_ref[k], out_chunk(k, c, nsub_ref))
    grid_spec = pltpu.PrefetchScalarGridSpec(
        num_scalar_prefetch=3,
        grid=(MOE_TILES_MAX, n_steps),
        in_specs=[
            pl.BlockSpec((MOE_TILE, D), row_map),
            pl.BlockSpec((MOE_TILE, ROUTER_LANES), row_map),
            pl.BlockSpec((None, None, D, FF_CHUNK), w13_map),
            pl.BlockSpec((None, None, D, FF_CHUNK), w13_map),
            pl.BlockSpec((None, EXP_PER_GROUP, D_EXPERT, OUT_CHUNK), w2_map),
        ],
        out_specs=pl.BlockSpec((MOE_TILE, OUT_CHUNK), out_map),
        scratch_shapes=[pltpu.VMEM((N_FF_CHUNKS, MOE_TILE, FF_CHUNK), BF16)],
    )
    return pl.pallas_call(
        _moe_kernel,
        grid_spec=grid_spec,
        out_shape=jax.ShapeDtypeStruct((ROW_CAP, D), BF16),
        compiler_params=_params("arbitrary", "arbitrary"),
        name="moe_experts",
    )(*plan, hs, cs, w1, w3, w2)


def _combine_kernel(off_ref, run0_ref, padded_ref, x_ref, gate_ref, comb_ref, ys_ref, *refs,
                    split_out, emit_next):
    if emit_next:
        (ng_ref, nsc_ref, nsh_ref), refs = refs[:3], refs[3:]
    out_refs, (yw_ref, sem) = refs[:-2], refs[-2:]
    run_refs = (off_ref, run0_ref, padded_ref)
    i = pl.program_id(0)
    n_steps = pl.num_programs(0)
    slot = i % 2

    def for_each_copy(tile, sl, fn):
        for g in range(N_GROUPS):
            for b, pred, src, dst, rows in _run_pieces(run_refs, tile, g):
                @pl.when(pred)
                def _():
                    fn(pltpu.make_async_copy(ys_ref.at[pl.ds(dst, rows), :],
                                             yw_ref.at[sl, pl.ds(src, rows), :], sem.at[sl, g, b]))

    def fetch(tile, sl):
        yw_ref[sl, TOK_TILE:, :] = jnp.zeros((TILE_ROWS - TOK_TILE, D), BF16)
        for_each_copy(tile, sl, lambda cp: cp.start())

    @pl.when(i == 0)
    def _():
        fetch(i, slot)

    @pl.when(i + 1 < n_steps)
    def _():
        fetch(i + 1, 1 - slot)

    for_each_copy(i, slot, lambda cp: cp.wait())

    pos = comb_ref[:, POS_LANE:POS_LANE + 1].astype(jnp.int32)
    wcol = lax.broadcasted_iota(jnp.int32, (TOK_TILE, TILE_ROWS), 1)
    perm = jnp.where(wcol == pos, 1.0, 0.0).astype(BF16)
    y = jnp.dot(perm, yw_ref[slot], preferred_element_type=F32)
    out = x_ref[...] + gate_ref[...] * y
    if emit_next:
        out_refs[1][...] = _modnorm(out, ng_ref[...], nsc_ref[...], nsh_ref[...]).astype(BF16)
    if not split_out:
        out_refs[0][...] = out
    else:
        @pl.when(i < N_CTX // TOK_TILE)
        def _():
            out_refs[0][...] = out

        @pl.when(i >= N_CTX // TOK_TILE)
        def _():
            out_refs[1][...] = out


def _combine(x, mod, layer, comb, ys, runs, next_norm_g):
    n_ctx_tiles = N_CTX // TOK_TILE
    split_out = next_norm_g is None
    tile_spec = pl.BlockSpec((TOK_TILE, D), lambda i, *_: (i, 0))
    in_specs = [
        tile_spec,
        _mod_spec(layer, 5, TOK_TILE, row_axis=0),
        pl.BlockSpec((TOK_TILE, ROUTER_LANES), lambda i, *_: (i, 0)),
        pl.BlockSpec(memory_space=pl.ANY),
    ]
    args = [x, mod, comb, ys]
    if split_out:
        out_specs = [pl.BlockSpec((TOK_TILE, D), lambda i, *_: (jnp.minimum(i, n_ctx_tiles - 1), 0)),
                     pl.BlockSpec((TOK_TILE, D), lambda i, *_: (jnp.maximum(i - n_ctx_tiles, 0), 0))]
        out_shape = [jax.ShapeDtypeStruct((N_CTX, D), F32), jax.ShapeDtypeStruct((N_LAT, D), F32)]
    else:
        in_specs += [pl.BlockSpec((1, D), lambda i, *_: (0, 0)),
                     _mod_spec(layer + 1, 1, TOK_TILE, row_axis=0),
                     _mod_spec(layer + 1, 0, TOK_TILE, row_axis=0)]
        args += [next_norm_g.reshape(1, D), mod, mod]
        out_specs = [tile_spec, tile_spec]
        out_shape = [jax.ShapeDtypeStruct((N_TOK, D), F32), jax.ShapeDtypeStruct((N_TOK, D), BF16)]
    grid_spec = pltpu.PrefetchScalarGridSpec(
        num_scalar_prefetch=3,
        grid=(N_TOK_TILES,),
        in_specs=in_specs,
        out_specs=out_specs,
        scratch_shapes=[
            pltpu.VMEM((2, TILE_ROWS, D), BF16),
            pltpu.SemaphoreType.DMA((2, N_GROUPS, len(RUN_BITS))),
        ],
    )
    return pl.pallas_call(
        functools.partial(_combine_kernel, split_out=split_out, emit_next=not split_out),
        grid_spec=grid_spec,
        out_shape=out_shape,
        compiler_params=_params("arbitrary"),
        name="moe_combine",
    )(*runs, *args)


def _moe_experts_and_combine(x, routed, mod, layer, w1, w3, w2, next_norm_g):
    hc, cc, comb, cnt = routed
    runs, end, plan = _moe_plan(cnt[:, 0, :N_GROUPS].astype(jnp.int32))
    hs, cs = _dispatch(hc, cc, runs, end)
    ys = _moe_experts(hs, cs, w1, w3, w2, layer, plan)
    return _combine(x, mod, layer, comb, ys, runs, next_norm_g)


def kernel(x_prompt, x_sample, cache_k, cache_v, c, c_ctx, ada_w, ada_b, norm_mix_g, norm_ffn_g,
           mix_in_w, pool_w, pool_scale, conv_w, conv_b, conv_ln_g, conv_ln_b, mix_out_w,
           qkv_w, q_norm_g, k_norm_g, rpb, attn_out_w, router_g_w, router_g_b, router_e_w,
           router_e_b, exp_w1, exp_w3, exp_w2):
    x = (x_prompt.reshape(N_CTX, D), x_sample.reshape(N_LAT, D))
    cvec8 = jnp.concatenate([c_ctx[None, :], c, jnp.zeros((8 - 1 - N_LAT_SEQ, D), F32)], axis=0)
    mod = _mod_table(_adaln_first_layer(cvec8, ada_w, ada_b))
    n_attn = DEPTH // 2
    cache_k2 = cache_k.reshape(N_LAT_SEQ, n_attn, CTX_SEQ, D)
    cache_v2 = cache_v.reshape(N_LAT_SEQ, n_attn, CTX_SEQ, D)

    caches = None
    h = None
    for l in range(DEPTH):
        e = l // 2
        if l % 2 == 0:
            u = _mm_prologue(x, mod, l, norm_mix_g[l], mix_in_w, e, h=h)
            mixed, *later_mod = _seq_mixer(u, pool_w[e].astype(BF16), pool_scale[e], conv_w[e], conv_b[e],
                                           conv_ln_g[e], conv_ln_b[e],
                                           ada=(cvec8, ada_w, ada_b) if l == 0 else None)
            if later_mod:
                mod = jnp.concatenate([mod, _mod_table(later_mod[0])], axis=0)
            out_w = mix_out_w
        else:
            q, k, v, new_k, new_v = _qkv_proj(h, e, qkv_w, q_norm_g[e], k_norm_g[e], caches)
            caches = (new_k, new_v)
            mixed = _ctx_attention(q, k, v)
            mixed = _na_attention(q, k, v, cache_k2[:, e], cache_v2[:, e], rpb[e], mixed)
            out_w = attn_out_w
        rw, rb = _router_weights(router_g_w[l], router_g_b[l], router_e_w[l], router_e_b[l])
        x, *routed = _out_proj_and_route(mixed, out_w, e, x, mod, l, norm_ffn_g[l], rw, rb)
        last = l == DEPTH - 1
        x, h = _moe_experts_and_combine(x, routed, mod, l, exp_w1, exp_w3, exp_w2,
                                        next_norm_g=None if last else norm_mix_g[l + 1])

    y_prompt = x.reshape(N_CTX_SEQ, CTX_SEQ, D)
    y_sample = h.reshape(N_LAT_SEQ, LAT_SEQ, D)
    cache_dims = (N_CTX_SEQ, n_attn, CTX_SEQ, N_HEADS, HEAD_DIM)
    return (y_prompt, y_sample, caches[0].reshape(cache_dims), caches[1].reshape(cache_dims))
```

```python
import functools

import numpy as np
import jax
import jax.numpy as jnp
from jax import lax
from jax.experimental import pallas as pl
from jax.experimental.pallas import tpu as pltpu

D = 2048
N_CTX_SEQ = 16
CTX_SEQ = 256
N_LAT_SEQ = 2
LAT_SEQ = 1024
N_CTX = N_CTX_SEQ * CTX_SEQ
N_LAT = N_LAT_SEQ * LAT_SEQ
N_TOK = N_CTX + N_LAT
DEPTH = 4
GRID_W = 64
GRID_H = LAT_SEQ // GRID_W
N_HEADS = 16
HEAD_DIM = 128
ATTN_SCALE = HEAD_DIM ** -0.5
D_POOL = 1024
D_CONV = 1024
POOL_WINDOWS = (2, 4, 8, 16)
POOL_GC = 256
CONV_W = 31
NA_KH = 8
NA_KW = 16
N_GROUPS = 4
EXP_PER_GROUP = 4
N_EXPERTS = 16
D_EXPERT = 512
EPS = 1e-6
NEG = -1e30

SEQ_TILE = 256
HALO = 16
ROUTER_LANES = 128
EXPERT_LANE0 = N_GROUPS
POS_LANE = EXPERT_LANE0 + N_EXPERTS
LOW_LANE0 = 32
VMEM_LIMIT = 56 * 1024 * 1024

TOK_TILE = 256
N_TOK_TILES = N_TOK // TOK_TILE
ROW_ALIGN = 16
RUN_BITS = (256, 128, 64, 32, 16)
TILE_ROWS = TOK_TILE + N_GROUPS * ROW_ALIGN
MOE_TILE = 1024
MOE_SUB = 128
FF_CHUNK = 512
OUT_CHUNK = 512
N_FF_CHUNKS = EXP_PER_GROUP * D_EXPERT // FF_CHUNK
N_OUT_CHUNKS = D // OUT_CHUNK
ZERO_ROWS = MOE_SUB
MAX_PADDED = N_TOK + N_TOK_TILES * N_GROUPS * (ROW_ALIGN - 1)
MOE_TILES_MAX = (MAX_PADDED + N_GROUPS * (MOE_TILE - 1)) // MOE_TILE
GROUP_CAP = -(-(N_TOK + N_TOK_TILES * (ROW_ALIGN - 1) + ZERO_ROWS) // MOE_TILE) * MOE_TILE
ROW_CAP = N_GROUPS * GROUP_CAP

F32 = jnp.float32
BF16 = jnp.bfloat16


def _cond_of_row(row):
    row = jnp.minimum(row, N_TOK - 1)
    return jnp.where(row < N_CTX, 0, 1 + (row - N_CTX) // LAT_SEQ)


def _mod_spec(layer, which, tm, tn=D, row_axis=1, col_axis=None):
    def index_map(*ids):
        cond = _cond_of_row(ids[row_axis] * tm)
        col = 0 if col_axis is None else ids[col_axis]
        return (layer, cond, which, 0, col)
    return pl.BlockSpec((None, None, None, 1, tn), index_map)


def _silu(x):
    return x / (1.0 + jnp.exp(-x))


def _modnorm(x, g, sc, sh):
    ms = jnp.mean(x * x, axis=-1, keepdims=True)
    y = x * lax.rsqrt(ms + EPS) * g
    return y * (1.0 + sc) + sh


def _params(*sem):
    return pltpu.CompilerParams(dimension_semantics=sem, vmem_limit_bytes=VMEM_LIMIT)


def _ada_kernel(c_ref, w_ref, b_ref, o_ref):
    s = _silu(c_ref[...]).astype(BF16)
    o_ref[...] = jnp.dot(s, w_ref[...].astype(BF16), preferred_element_type=F32) + b_ref[...]


N_MOD = 6 * D


def _mod_table(raw):
    return raw[:, :3].reshape(raw.shape[0], 3, 6, 1, D)


def _adaln_first_layer(cvec8, ada_w, ada_b):
    tn = 1024
    return pl.pallas_call(
        _ada_kernel,
        grid=(N_MOD // tn,),
        in_specs=[
            pl.BlockSpec((8, D), lambda j: (0, 0)),
            pl.BlockSpec((None, D, tn), lambda j: (0, 0, j)),
            pl.BlockSpec((None, 1, tn), lambda j: (0, 0, j)),
        ],
        out_specs=pl.BlockSpec((None, 8, tn), lambda j: (0, 0, j)),
        out_shape=jax.ShapeDtypeStruct((1, 8, N_MOD), F32),
        compiler_params=_params("arbitrary"),
        name="adaln",
    )(cvec8, ada_w, ada_b.reshape(DEPTH, 1, N_MOD))


def _token_specs(x, tm, tn, idx):
    if not isinstance(x, tuple):
        return [pl.BlockSpec((tm, tn), lambda j, i: idx(i, j))], [x]
    n_ctx_tiles = N_CTX // tm
    ctx_map = lambda j, i: idx(jnp.minimum(i, n_ctx_tiles - 1), j)
    lat_map = lambda j, i: idx(jnp.maximum(i - n_ctx_tiles, 0), j)
    return [pl.BlockSpec((tm, tn), ctx_map), pl.BlockSpec((tm, tn), lat_map)], list(x)


def _token_rows(refs, i, tm):
    if len(refs) == 1:
        return refs[0][...]
    return jnp.where(i < N_CTX // tm, refs[0][...], refs[1][...])


def _cast_weights_once(i, pairs):
    @pl.when(i == 0)
    def _():
        for w_ref, wb_ref in pairs:
            wb_ref[...] = w_ref[...].astype(BF16)


def _mm_pro_kernel(*refs, n_x, tm):
    x_refs = refs[:n_x]
    g_ref, sc_ref, sh_ref, w_ref, o_ref, wb_ref = refs[n_x:]
    i = pl.program_id(1)
    _cast_weights_once(i, [(w_ref, wb_ref)])
    h = _modnorm(_token_rows(x_refs, i, tm), g_ref[...], sc_ref[...], sh_ref[...]).astype(BF16)
    o_ref[...] = jnp.dot(h, wb_ref[...], preferred_element_type=F32)


def _mm_plain_kernel(h_ref, w_ref, o_ref, wb_ref):
    _cast_weights_once(pl.program_id(1), [(w_ref, wb_ref)])
    o_ref[...] = jnp.dot(h_ref[...], wb_ref[...], preferred_element_type=F32)


def _mm_prologue(x, mod, layer, norm_g, w, w_idx, h=None, tm=512, tn=1024):
    f = w.shape[2]
    if h is not None:
        tm = 2 * tm
    w_spec = pl.BlockSpec((None, D, tn), lambda j, i: (w_idx, 0, j))
    if h is None:
        x_specs, x_args = _token_specs(x, tm, D, lambda i, j: (i, 0))
        body = functools.partial(_mm_pro_kernel, n_x=len(x_args), tm=tm)
        in_specs = x_specs + [pl.BlockSpec((1, D), lambda j, i: (0, 0)),
                              _mod_spec(layer, 1, tm), _mod_spec(layer, 0, tm), w_spec]
        args = x_args + [norm_g.reshape(1, D), mod, mod, w]
    else:
        body = _mm_plain_kernel
        in_specs = [pl.BlockSpec((tm, D), lambda j, i: (i, 0)), w_spec]
        args = [h, w]
    return pl.pallas_call(
        body,
        grid=(f // tn, N_TOK // tm),
        in_specs=in_specs,
        out_specs=pl.BlockSpec((tm, tn), lambda j, i: (i, j)),
        out_shape=jax.ShapeDtypeStruct((N_TOK, f), F32),
        scratch_shapes=[pltpu.VMEM((D, tn), BF16)],
        compiler_params=_params("arbitrary", "arbitrary"),
        name="mm_prologue",
    )(*args)


def _qkv_kernel(h_ref, wq_ref, wk_ref, wv_ref, qg_ref, kg_ref, *rest, tn, n_ctx_tiles, seqs_per_tile):
    q_ref, k_ref, v_ref, ck_ref, cv_ref, wqb_ref, wkb_ref, wvb_ref, kn_ref, vf_ref = rest[-10:]
    i = pl.program_id(1)
    _cast_weights_once(i, [(wq_ref, wqb_ref), (wk_ref, wkb_ref), (wv_ref, wvb_ref)])
    h = h_ref[...]
    q = jnp.dot(h, wqb_ref[...], preferred_element_type=F32)
    k = jnp.dot(h, wkb_ref[...], preferred_element_type=F32)
    v = jnp.dot(h, wvb_ref[...], preferred_element_type=F32)
    for hh in range(tn // HEAD_DIM):
        ls = slice(hh * HEAD_DIM, (hh + 1) * HEAD_DIM)
        qh = q[:, ls]
        kh = k[:, ls]
        qn = qh * lax.rsqrt(jnp.mean(qh * qh, axis=-1, keepdims=True) + EPS) * qg_ref[:, ls]
        kn = kh * lax.rsqrt(jnp.mean(kh * kh, axis=-1, keepdims=True) + EPS) * kg_ref[:, ls]
        q_ref[:, ls] = qn.astype(BF16)
        k_ref[:, ls] = kn.astype(BF16)
        kn_ref[:, ls] = kn
    v_ref[...] = v.astype(BF16)
    vf_ref[...] = v

    @pl.when(i < n_ctx_tiles)
    def _():
        ck_ref[...] = kn_ref[...].reshape(seqs_per_tile, CTX_SEQ, tn)
        cv_ref[...] = vf_ref[...].reshape(seqs_per_tile, CTX_SEQ, tn)


def _qkv_proj(h, w_idx, w, q_gain, k_gain, caches, tm=512, tn=512):
    n = h.shape[0]
    n_attn = DEPTH // 2
    ncol = D // tn
    seqs_per_tile = tm // CTX_SEQ
    n_ctx_tiles = N_CTX // tm
    cache_shape = jax.ShapeDtypeStruct((N_CTX_SEQ, n_attn, CTX_SEQ, D), F32)
    cache_spec = pl.BlockSpec(
        (seqs_per_tile, None, CTX_SEQ, tn),
        lambda j, i: (jnp.minimum(i, n_ctx_tiles - 1), w_idx, 0, j))
    in_specs = [
        pl.BlockSpec((tm, D), lambda j, i: (i, 0)),
        pl.BlockSpec((None, D, tn), lambda j, i: (w_idx, 0, j)),
        pl.BlockSpec((None, D, tn), lambda j, i: (w_idx, 0, ncol + j)),
        pl.BlockSpec((None, D, tn), lambda j, i: (w_idx, 0, 2 * ncol + j)),
        pl.BlockSpec((1, tn), lambda j, i: (0, j)),
        pl.BlockSpec((1, tn), lambda j, i: (0, j)),
    ]
    args = [h, w, w, w,
            jnp.tile(q_gain, N_HEADS).reshape(1, D), jnp.tile(k_gain, N_HEADS).reshape(1, D)]
    aliases = {}
    if caches is not None:
        in_specs += [pl.BlockSpec(memory_space=pl.ANY), pl.BlockSpec(memory_space=pl.ANY)]
        aliases = {len(args): 3, len(args) + 1: 4}
        args += list(caches)
    act_spec = pl.BlockSpec((tm, tn), lambda j, i: (i, j))
    act_shape = jax.ShapeDtypeStruct((n, D), BF16)
    return pl.pallas_call(
        functools.partial(_qkv_kernel, tn=tn, n_ctx_tiles=n_ctx_tiles, seqs_per_tile=seqs_per_tile),
        grid=(ncol, n // tm),
        in_specs=in_specs,
        out_specs=[act_spec, act_spec, act_spec, cache_spec, cache_spec],
        out_shape=[act_shape, act_shape, act_shape, cache_shape, cache_shape],
        input_output_aliases=aliases,
        scratch_shapes=[pltpu.VMEM((D, tn), BF16)] * 3 + [pltpu.VMEM((tm, tn), F32)] * 2,
        compiler_params=_params("arbitrary", "arbitrary"),
        name="qkv_proj",
    )(*args)


def _seq_mix_tile(i, u_ref, top_ref, bot_ref, pw_ref, ps_ref, cw_ref, cb_ref, lg_ref, lb_ref,
                  o_ref, zpad_ref, conv_ref):
    n_ctx_tiles = N_CTX // SEQ_TILE
    tiles_per_lat = LAT_SEQ // SEQ_TILE
    is_lat = i >= n_ctx_tiles
    chunk = jnp.where(is_lat, (i - n_ctx_tiles) % tiles_per_lat, 0)
    top_ok = jnp.logical_and(is_lat, chunk > 0)
    bot_ok = jnp.logical_and(is_lat, chunk < tiles_per_lat - 1)
    seq_len = jnp.where(is_lat, LAT_SEQ, CTX_SEQ)
    t = chunk * SEQ_TILE + lax.broadcasted_iota(jnp.int32, (SEQ_TILE, 1), 0)
    rows = SEQ_TILE + 2 * HALO

    for g, win in enumerate(POOL_WINDOWS):
        ls = slice(g * POOL_GC, (g + 1) * POOL_GC)
        mid = u_ref[:, ls]
        top = jnp.where(top_ok, top_ref[:, ls], 0.0)
        bot = jnp.where(bot_ok, bot_ref[:, ls], 0.0)
        up = jnp.concatenate([top, mid, bot], axis=0)
        s = pltpu.roll(up, 1, 0) + up
        if win >= 4:
            s = pltpu.roll(s, 1, 0) + pltpu.roll(s, rows - 1, 0)
        if win >= 8:
            s = pltpu.roll(s, 2, 0) + pltpu.roll(s, rows - 2, 0)
        if win >= 16:
            s = pltpu.roll(s, 4, 0) + pltpu.roll(s, rows - 4, 0)
        s = s[HALO:HALO + SEQ_TILE]
        lo = jnp.maximum(t - win // 2, 0)
        hi = jnp.minimum(t + (win - win // 2) - 1, seq_len - 1)
        cnt = (hi - lo + 1).astype(F32)
        p = (s / cnt - mid).astype(BF16)
        y = jnp.dot(p, pw_ref[g], preferred_element_type=F32) * ps_ref[:, ls]
        o_ref[:, ls] = y.astype(BF16)

    def glu(ref):
        return ref[:, D_POOL:D_POOL + D_CONV] * (1.0 / (1.0 + jnp.exp(-ref[:, D_POOL + D_CONV:])))

    zpad_ref[0:HALO, :] = jnp.where(top_ok, glu(top_ref), 0.0)
    zpad_ref[HALO:HALO + SEQ_TILE, :] = glu(u_ref)
    zpad_ref[HALO + SEQ_TILE:rows, :] = jnp.where(bot_ok, glu(bot_ref), 0.0)
    for lt in range(D_CONV // 128):
        ls = slice(lt * 128, (lt + 1) * 128)
        zp = zpad_ref[:, ls]
        acc = jnp.zeros((SEQ_TILE, 128), F32)
        for b in range(8):
            sb = zp if b == 0 else pltpu.roll(zp, rows - b, 0)
            for a in range(4):
                off = 8 * a + b
                if 1 <= off <= CONV_W:
                    acc = acc + cw_ref[off - 1:off, ls] * sb[8 * a:8 * a + SEQ_TILE]
        conv_ref[:, ls] = acc + cb_ref[:, ls]
    zc = conv_ref[...]
    mu = jnp.mean(zc, axis=-1, keepdims=True)
    d = zc - mu
    var = jnp.mean(d * d, axis=-1, keepdims=True)
    zn = d * lax.rsqrt(var + EPS) * lg_ref[...] + lb_ref[...]
    o_ref[:, D_POOL:] = _silu(zn).astype(BF16)


N_SEQ_INPUTS = 9


def _seq_kernel(*refs, with_ada):
    seq_in, refs = refs[:N_SEQ_INPUTS], refs[N_SEQ_INPUTS:]
    if with_ada:
        (c_ref, w_ref, b_ref), refs = refs[:3], refs[3:]
        o_ref, mod_ref, zpad_ref, conv_ref = refs
        _ada_kernel(c_ref, w_ref, b_ref, mod_ref)
    else:
        o_ref, zpad_ref, conv_ref = refs
    _seq_mix_tile(pl.program_id(0), *seq_in, o_ref, zpad_ref, conv_ref)


def _seq_mixer(u, pool_w_bf16, pool_scale, conv_w, conv_b, ln_g, ln_b, ada=None):
    n = u.shape[0]
    f_in = u.shape[1]
    n_steps = n // SEQ_TILE
    hb = SEQ_TILE // HALO
    n_hblocks = n // HALO
    const = lambda *shape: pl.BlockSpec(shape, lambda i: (0,) * len(shape))
    in_specs = [
        pl.BlockSpec((SEQ_TILE, f_in), lambda i: (i, 0)),
        pl.BlockSpec((HALO, f_in), lambda i: (jnp.maximum(i * hb - 1, 0), 0)),
        pl.BlockSpec((HALO, f_in), lambda i: (jnp.minimum((i + 1) * hb, n_hblocks - 1), 0)),
        const(len(POOL_WINDOWS), POOL_GC, POOL_GC),
        const(1, D_POOL), const(CONV_W, D_CONV), const(1, D_CONV), const(1, D_CONV), const(1, D_CONV),
    ]
    args = [u, u, u, pool_w_bf16, pool_scale.reshape(1, D_POOL), conv_w, conv_b.reshape(1, D_CONV),
            ln_g.reshape(1, D_CONV), ln_b.reshape(1, D_CONV)]
    out_specs = [pl.BlockSpec((SEQ_TILE, D_POOL + D_CONV), lambda i: (i, 0))]
    out_shape = [jax.ShapeDtypeStruct((n, D_POOL + D_CONV), BF16)]
    if ada is not None:
        cvec8, ada_w, ada_b = ada
        later = DEPTH - 1
        chunk = later * N_MOD // n_steps
        per_layer = N_MOD // chunk
        chunk_map = lambda i: (1 + i // per_layer, 0, i % per_layer)
        in_specs += [const(8, D),
                     pl.BlockSpec((None, D, chunk), chunk_map),
                     pl.BlockSpec((None, 1, chunk), chunk_map)]
        args += [cvec8, ada_w, ada_b.reshape(DEPTH, 1, N_MOD)]
        out_specs.append(pl.BlockSpec((None, 8, chunk), lambda i: (i // per_layer, 0, i % per_layer)))
        out_shape.append(jax.ShapeDtypeStruct((later, 8, N_MOD), F32))
    return pl.pallas_call(
        functools.partial(_seq_kernel, with_ada=ada is not None),
        grid=(n_steps,),
        in_specs=in_specs,
        out_specs=out_specs,
        out_shape=out_shape,
        scratch_shapes=[
            pltpu.VMEM((SEQ_TILE + 2 * HALO, D_CONV), F32),
            pltpu.VMEM((SEQ_TILE, D_CONV), F32),
        ],
        compiler_params=_params("arbitrary"),
        name="seq_mixer",
    )(*args)


_NT = (((1,), (1,)), ((), ()))


def _ctx_attn_kernel(q_ref, k_ref, v_ref, o_ref, s_ref, p_ref):
    for h in range(N_HEADS):
        ls = slice(h * HEAD_DIM, (h + 1) * HEAD_DIM)
        s_ref[h] = lax.dot_general(q_ref[:, ls], k_ref[:, ls], _NT, preferred_element_type=F32) * ATTN_SCALE
    for h in range(N_HEADS):
        s = s_ref[h]
        e = jnp.exp(s - jnp.max(s, axis=-1, keepdims=True))
        p_ref[h] = (e * (1.0 / jnp.sum(e, axis=-1, keepdims=True))).astype(BF16)
    for h in range(N_HEADS):
        ls = slice(h * HEAD_DIM, (h + 1) * HEAD_DIM)
        o_ref[:, ls] = jnp.dot(p_ref[h], v_ref[:, ls], preferred_element_type=F32).astype(BF16)


def _ctx_attention(q, k, v):
    n = q.shape[0]
    spec = pl.BlockSpec((CTX_SEQ, D), lambda b: (b, 0))
    return pl.pallas_call(
        _ctx_attn_kernel,
        grid=(N_CTX_SEQ,),
        in_specs=[spec, spec, spec],
        out_specs=spec,
        out_shape=jax.ShapeDtypeStruct((n, D), BF16),
        scratch_shapes=[pltpu.VMEM((N_HEADS, CTX_SEQ, CTX_SEQ), F32),
                        pltpu.VMEM((N_HEADS, CTX_SEQ, CTX_SEQ), BF16)],
        compiler_params=_params("arbitrary"),
        name="ctx_attention",
    )(q, k, v)


def _na_window_mask():
    cq = np.arange(GRID_W)[:, None]
    kc = np.arange(GRID_W)[None, :]
    win0 = np.clip(cq - NA_KW // 2, 0, GRID_W - NA_KW)
    mask = ((kc >= win0) & (kc < win0 + NA_KW)).astype(np.float32)
    return np.tile(mask, (1, NA_KH))


def _na_bias_rows(rpb_e):
    centre = NA_KW - 1
    pad = jnp.zeros(rpb_e.shape[:2] + (128 - (2 * NA_KW - 1),), F32)
    return jnp.concatenate([rpb_e[..., centre:], pad, rpb_e[..., :centre]], axis=-1)


def _na_key_row0(r):
    return min(max(r - NA_KH // 2, 0), GRID_H - NA_KH)


def _na_segments():
    segs = []
    for r in range(GRID_H):
        kr0 = _na_key_row0(r)
        if segs and segs[-1][2] == kr0:
            segs[-1][1] = r + 1
        else:
            segs.append([r, r + 1, kr0])
    return segs


def _na_kernel(q_ref, k_ref, v_ref, kc_ref, vc_ref, rows_ref, mask_ref, o_in_ref, o_ref,
               s_ref, p_ref, acc_ref, tz_ref):
    del o_in_ref
    band = NA_KH * GRID_W
    n_ctx = CTX_SEQ
    kc = kc_ref[...].astype(BF16)
    vc = vc_ref[...].astype(BF16)
    @pl.when(pl.program_id(1) == 0)
    def _():
        n_dr = 2 * NA_KH - 1
        toeplitz = [pltpu.roll(jnp.broadcast_to(rows_ref[d:d + 1, :], (GRID_W, 2 * GRID_W)), 0, 1,
                               stride=1, stride_axis=0) for d in range(n_dr)]
        left = lax.broadcasted_iota(jnp.int32, (GRID_W, 2 * GRID_W), 1) < GRID_W
        for d in range(n_dr - 1):
            tz_ref[d] = jnp.where(left, toeplitz[d], pltpu.roll(toeplitz[d + 1], GRID_W, 1))
    for r0, r1, kr0 in _na_segments():
        rs = slice(r0 * GRID_W, r1 * GRID_W)
        kb = k_ref[kr0 * GRID_W:kr0 * GRID_W + band, :]
        s_ref[rs, 0:band] = lax.dot_general(q_ref[rs, :], kb, _NT, preferred_element_type=F32) * ATTN_SCALE
    s_ref[:, band:band + n_ctx] = lax.dot_general(q_ref[...], kc, _NT, preferred_element_type=F32) * ATTN_SCALE
    mask = mask_ref[...] > 0.5
    for r in range(GRID_H):
        rs = slice(r * GRID_W, (r + 1) * GRID_W)
        kr0 = _na_key_row0(r)
        bias = jnp.concatenate(
            [tz_ref[kr0 + j - r + NA_KH - 1] for j in range(0, NA_KH, 2)], axis=-1)
        s_loc = jnp.where(mask, s_ref[rs, 0:band] + bias, NEG)
        s_ctx = s_ref[rs, band:band + n_ctx]
        m = jnp.maximum(jnp.max(s_loc, axis=-1, keepdims=True), jnp.max(s_ctx, axis=-1, keepdims=True))
        e_loc = jnp.exp(s_loc - m)
        e_ctx = jnp.exp(s_ctx - m)
        inv = 1.0 / (jnp.sum(e_loc, axis=-1, keepdims=True) + jnp.sum(e_ctx, axis=-1, keepdims=True))
        p_ref[rs, 0:band] = (e_loc * inv).astype(BF16)
        p_ref[rs, band:band + n_ctx] = (e_ctx * inv).astype(BF16)
    for r0, r1, kr0 in _na_segments():
        rs = slice(r0 * GRID_W, r1 * GRID_W)
        vb = v_ref[kr0 * GRID_W:kr0 * GRID_W + band, :]
        acc_ref[rs, :] = jnp.dot(p_ref[rs, 0:band], vb, preferred_element_type=F32)
    o_ctx = jnp.dot(p_ref[:, band:band + n_ctx], vc, preferred_element_type=F32)
    o_ref[...] = (acc_ref[...] + o_ctx).astype(BF16)


def _na_attention(q, k, v, cache_k_e, cache_v_e, rpb_e, o_ctx):
    mask = _na_window_mask()
    lat0 = N_CTX // LAT_SEQ
    n_dr = 2 * NA_KH - 1
    n_keys = NA_KH * GRID_W + CTX_SEQ
    tok_spec = pl.BlockSpec((LAT_SEQ, HEAD_DIM), lambda h, b: (lat0 + b, h))
    cache_spec = pl.BlockSpec((None, CTX_SEQ, HEAD_DIM), lambda h, b: (b, 0, h))
    return pl.pallas_call(
        _na_kernel,
        grid=(N_HEADS, N_LAT_SEQ),
        in_specs=[
            tok_spec, tok_spec, tok_spec, cache_spec, cache_spec,
            pl.BlockSpec((None, n_dr, 2 * GRID_W), lambda h, b: (h, 0, 0)),
            pl.BlockSpec((GRID_W, NA_KH * GRID_W), lambda h, b: (0, 0)),
            pl.BlockSpec(memory_space=pl.ANY),
        ],
        out_specs=tok_spec,
        out_shape=jax.ShapeDtypeStruct(o_ctx.shape, o_ctx.dtype),
        input_output_aliases={7: 0},
        scratch_shapes=[pltpu.VMEM((LAT_SEQ, n_keys), F32),
                        pltpu.VMEM((LAT_SEQ, n_keys), BF16),
                        pltpu.VMEM((LAT_SEQ, HEAD_DIM), F32),
                        pltpu.VMEM((n_dr - 1, GRID_W, 2 * GRID_W), F32)],
        compiler_params=_params("arbitrary", "arbitrary"),
        name="na_attention",
    )(q, k, v, cache_k_e, cache_v_e, _na_bias_rows(rpb_e), jnp.asarray(mask), o_ctx)


def _split_bf16(x, pieces):
    out = []
    for _ in range(pieces):
        p = x.astype(BF16)
        out.append(p)
        x = x - p.astype(F32)
    return out


def _moe_prenorm(x, g_ref, sc_ref, sh_ref):
    h = _modnorm(x, g_ref[...], sc_ref[...], sh_ref[...])
    return jnp.concatenate(_split_bf16(h, 2), axis=1)


def _router_logits(h_pieces, rw_ref, rb_ref):
    prod = jnp.dot(h_pieces, rw_ref[...], preferred_element_type=F32)
    return prod + pltpu.roll(prod, ROUTER_LANES - LOW_LANE0, 1) + rb_ref[...]


def _first_argmax(vals, valid, lane):
    v = jnp.where(valid, vals, -jnp.inf)
    m = jnp.max(v, axis=-1, keepdims=True)
    idx = jnp.min(jnp.where(jnp.logical_and(valid, v == m), lane, jnp.int32(1 << 20)),
                  axis=-1, keepdims=True)
    return m, idx


def _route_group(logits):
    lane = lax.broadcasted_iota(jnp.int32, logits.shape, 1)
    mg, gi = _first_argmax(logits, lane < N_GROUPS, lane)
    onehot_g = jnp.where(lane == gi, 1.0, 0.0)
    row = lax.broadcasted_iota(jnp.int32, (TOK_TILE, TOK_TILE), 0)
    col = lax.broadcasted_iota(jnp.int32, (TOK_TILE, TOK_TILE), 1)
    earlier = jnp.where(col < row, 1.0, 0.0).astype(BF16)
    rank = jnp.dot(earlier, onehot_g.astype(BF16), preferred_element_type=F32)
    return mg, gi, onehot_g, rank


def _route_weights(logits, group, comb_ref, cnt_ref):
    mg, gi, onehot_g, rank = group
    lane = lax.broadcasted_iota(jnp.int32, logits.shape, 1)
    is_g = lane < N_GROUPS
    pg = 1.0 / jnp.sum(jnp.where(is_g, jnp.exp(logits - mg), 0.0), axis=-1, keepdims=True)
    e_lane0 = EXPERT_LANE0 + gi * EXP_PER_GROUP
    is_e = jnp.logical_and(lane >= e_lane0, lane < e_lane0 + EXP_PER_GROUP)
    m1, i1 = _first_argmax(logits, is_e, lane)
    m2, i2 = _first_argmax(logits, jnp.logical_and(is_e, lane != i1), lane)
    e2 = jnp.exp(m2 - m1)
    den = 1.0 + e2
    w1 = (1.0 / den) * pg
    w2 = (e2 / den) * pg

    cnt = jnp.sum(onehot_g, axis=0, keepdims=True)
    padded = jnp.floor((cnt + (ROW_ALIGN - 1)) * (1.0 / ROW_ALIGN)) * ROW_ALIGN
    padded8 = jnp.broadcast_to(padded, (8, ROUTER_LANES))
    run0 = (pltpu.roll(padded8, 1, 1) + pltpu.roll(padded8, 2, 1) + pltpu.roll(padded8, 3, 1))[0:1, :]
    pos = jnp.sum(onehot_g * (rank + run0), axis=-1, keepdims=True)

    comb = jnp.where(lane == i1, w1, jnp.where(lane == i2, w2, jnp.where(lane == POS_LANE, pos, onehot_g)))
    comb_ref[...] = comb
    cnt_ref[...] = cnt
    run_rows = [jnp.sum(jnp.where(lane[0:1, :] == g, padded, 0.0)).astype(jnp.int32) for g in range(N_GROUPS)]
    return comb, run_rows


def _sort_tile(h_hi, comb, hc_ref, cc_ref):
    pos_t = comb.T[POS_LANE:POS_LANE + 1, :]
    dest = lax.broadcasted_iota(jnp.int32, (TILE_ROWS, TOK_TILE), 0).astype(F32)
    perm = jnp.where(pos_t == dest, 1.0, 0.0).astype(BF16)
    hc_ref[...] = jnp.dot(perm, h_hi, preferred_element_type=F32).astype(BF16)
    c_parts = _split_bf16(comb, 3)
    cc_ref[...] = (jnp.dot(perm, c_parts[0], preferred_element_type=F32)
                   + jnp.dot(perm, c_parts[1], preferred_element_type=F32)
                   + jnp.dot(perm, c_parts[2], preferred_element_type=F32))


W_STAGE_COLS = 512


def _out_route_kernel(a_ref, w_hbm, gate_ref, g_ref, sc_ref, sh_ref, rw_ref, rb_ref, *refs, n_x, w_idx):
    x_refs = refs[:n_x]
    (xo_ref, hs_ref, cs_ref, comb_ref, cnt_ref, wb_ref, stage_ref, prev_ref, hcs_ref, ccs_ref,
     zh_ref, zc_ref, fill_ref, sent_ref, sem, sem_run, sem_zero) = refs[n_x:]
    i = pl.program_id(0)
    slot = i % 2

    @pl.when(i == 0)
    def _():
        prev_ref[...] = jnp.zeros_like(prev_ref)
        zh_ref[...] = jnp.zeros_like(zh_ref)
        zc_ref[...] = jnp.zeros_like(zc_ref)
        for g in range(N_GROUPS):
            fill_ref[g] = 0
        n_chunks = D // W_STAGE_COLS

        def chunk_copy(c):
            cols = pl.ds(c * W_STAGE_COLS, W_STAGE_COLS)
            return pltpu.make_async_copy(w_hbm.at[w_idx, :, cols], stage_ref.at[c % 2], sem.at[c % 2])

        chunk_copy(0).start()
        chunk_copy(1).start()
        for c in range(n_chunks):
            chunk_copy(c).wait()
            wb_ref[:, c * W_STAGE_COLS:(c + 1) * W_STAGE_COLS] = stage_ref[c % 2].astype(BF16)
            if c + 2 < n_chunks:
                chunk_copy(c + 2).start()

    h_prev = prev_ref[...]
    logits = _router_logits(h_prev, rw_ref, rb_ref)
    group = _route_group(logits)
    y = jnp.dot(a_ref[...], wb_ref[...], preferred_element_type=F32)
    x_mid = _token_rows(x_refs, jnp.minimum(i, N_TOK_TILES - 1), TOK_TILE) + gate_ref[...] * y
    xo_ref[...] = x_mid
    comb, run_rows = _route_weights(logits, group, comb_ref, cnt_ref)
    _sort_tile(h_prev[:, :D], comb, hcs_ref.at[slot], ccs_ref.at[slot])
    prev_ref[...] = _moe_prenorm(x_mid, g_ref, sc_ref, sh_ref)

    def for_each_copy(sl, rows, filled, fn):
        src0 = 0
        for g in range(N_GROUPS):
            for b, bit in enumerate(RUN_BITS):
                done = rows[g] & (-2 * bit)

                @pl.when((rows[g] & bit) != 0)
                def _():
                    src = pl.ds(pl.multiple_of(src0 + done, ROW_ALIGN), bit)
                    dst = pl.ds(pl.multiple_of(g * GROUP_CAP + filled[g] + done, ROW_ALIGN), bit)
                    fn(pltpu.make_async_copy(hcs_ref.at[sl, src, :], hs_ref.at[dst, :], sem_run.at[sl, 0, g, b]))
                    fn(pltpu.make_async_copy(ccs_ref.at[sl, src, :], cs_ref.at[dst, :], sem_run.at[sl, 1, g, b]))

            src0 = src0 + rows[g]

    @pl.when(i >= 1)
    def _():
        filled = [fill_ref[g] for g in range(N_GROUPS)]
        for_each_copy(slot, run_rows, filled, lambda cp: cp.start())
        for g in range(N_GROUPS):
            fill_ref[g] = filled[g] + run_rows[g]

        @pl.when(i >= 2)
        def _():
            sent = [sent_ref[g] for g in range(N_GROUPS)]
            for_each_copy(1 - slot, sent, [0] * N_GROUPS, lambda cp: cp.wait())

        for g in range(N_GROUPS):
            sent_ref[g] = run_rows[g]

    @pl.when(i == pl.num_programs(0) - 1)
    def _():
        for_each_copy(slot, [sent_ref[g] for g in range(N_GROUPS)], [0] * N_GROUPS, lambda cp: cp.wait())
        zero_copies = []
        for g in range(N_GROUPS):
            dst = pl.ds(pl.multiple_of(g * GROUP_CAP + fill_ref[g], ROW_ALIGN), ZERO_ROWS)
            zero_copies.append(pltpu.make_async_copy(zh_ref, hs_ref.at[dst, :], sem_zero.at[0, g]))
            zero_copies.append(pltpu.make_async_copy(zc_ref, cs_ref.at[dst, :], sem_zero.at[1, g]))
        for cp in zero_copies:
            cp.start()
        for cp in zero_copies:
            cp.wait()


def _out_proj_and_route(a_bf16, w, w_idx, x, mod, layer, norm_g, rw, rb):
    tm = TOK_TILE
    k = a_bf16.shape[1]
    last = N_TOK_TILES - 1
    proj = lambda i: jnp.minimum(i, last)
    routed = lambda i: jnp.maximum(i - 1, 0)
    if isinstance(x, tuple):
        n_ctx_tiles = N_CTX // tm
        x_specs = [pl.BlockSpec((tm, D), lambda i: (jnp.minimum(proj(i), n_ctx_tiles - 1), 0)),
                   pl.BlockSpec((tm, D), lambda i: (jnp.maximum(proj(i) - n_ctx_tiles, 0), 0))]
        x_args = list(x)
    else:
        x_specs = [pl.BlockSpec((tm, D), lambda i: (proj(i), 0))]
        x_args = [x]

    def mod_of_proj(which):
        return pl.BlockSpec((None, None, None, 1, D),
                            lambda i: (layer, _cond_of_row(proj(i) * tm), which, 0, 0))

    return pl.pallas_call(
        functools.partial(_out_route_kernel, n_x=len(x_args), w_idx=w_idx),
        grid=(N_TOK_TILES + 1,),
        in_specs=[
            pl.BlockSpec((tm, k), lambda i: (proj(i), 0)),
            pl.BlockSpec(memory_space=pl.ANY),
            mod_of_proj(2),
            pl.BlockSpec((1, D), lambda i: (0, 0)),
            mod_of_proj(4),
            mod_of_proj(3),
            pl.BlockSpec((2 * D, ROUTER_LANES), lambda i: (0, 0)),
            pl.BlockSpec((1, ROUTER_LANES), lambda i: (0, 0)),
        ] + x_specs,
        out_specs=[
            pl.BlockSpec((tm, D), lambda i: (proj(i), 0)),
            pl.BlockSpec(memory_space=pl.ANY),
            pl.BlockSpec(memory_space=pl.ANY),
            pl.BlockSpec((tm, ROUTER_LANES), lambda i: (routed(i), 0)),
            pl.BlockSpec((None, 1, ROUTER_LANES), lambda i: (routed(i), 0, 0)),
        ],
        out_shape=[
            jax.ShapeDtypeStruct((N_TOK, D), F32),
            jax.ShapeDtypeStruct((ROW_CAP, D), BF16),
            jax.ShapeDtypeStruct((ROW_CAP, ROUTER_LANES), F32),
            jax.ShapeDtypeStruct((N_TOK, ROUTER_LANES), F32),
            jax.ShapeDtypeStruct((N_TOK_TILES, 1, ROUTER_LANES), F32),
        ],
        scratch_shapes=[
            pltpu.VMEM((k, D), BF16),
            pltpu.VMEM((2, k, W_STAGE_COLS), F32),
            pltpu.VMEM((tm, 2 * D), BF16),
            pltpu.VMEM((2, TILE_ROWS, D), BF16),
            pltpu.VMEM((2, TILE_ROWS, ROUTER_LANES), F32),
            pltpu.VMEM((ZERO_ROWS, D), BF16),
            pltpu.VMEM((ZERO_ROWS, ROUTER_LANES), F32),
            pltpu.SMEM((N_GROUPS,), jnp.int32),
            pltpu.SMEM((N_GROUPS,), jnp.int32),
            pltpu.SemaphoreType.DMA((2,)),
            pltpu.SemaphoreType.DMA((2, 2, N_GROUPS, len(RUN_BITS))),
            pltpu.SemaphoreType.DMA((2, N_GROUPS)),
        ],
        compiler_params=_params("arbitrary"),
        name="out_proj_route",
    )(a_bf16, w, mod, norm_g.reshape(1, D), mod, mod, rw, rb, *x_args)


def _router_weights(rgw, rgb, rew, reb):
    n_out = N_GROUPS + N_EXPERTS
    rw = jnp.concatenate([rgw, rew.reshape(D, N_EXPERTS)], axis=1)
    hi = rw.astype(BF16)
    lo = (rw - hi.astype(F32)).astype(BF16)
    zeros = lambda n: jnp.zeros((D, n), BF16)
    top = jnp.concatenate([hi, zeros(LOW_LANE0 - n_out), lo, zeros(ROUTER_LANES - LOW_LANE0 - n_out)], axis=1)
    bottom = jnp.concatenate([hi, zeros(ROUTER_LANES - n_out)], axis=1)
    rb = jnp.concatenate([rgb, reb.reshape(N_EXPERTS)])
    rb = jnp.pad(rb, (0, ROUTER_LANES - n_out)).reshape(1, ROUTER_LANES)
    return jnp.concatenate([top, bottom], axis=0), rb


def _moe_plan(cnt):
    i32 = jnp.int32
    padded = (cnt + (ROW_ALIGN - 1)) // ROW_ALIGN * ROW_ALIGN
    run0 = jnp.cumsum(padded, axis=1) - padded
    length = jnp.sum(padded, axis=0)
    start = jnp.arange(N_GROUPS, dtype=i32) * GROUP_CAP
    off = start[None, :] + jnp.cumsum(padded, axis=0) - padded
    need = length
    ntile = (need + MOE_TILE - 1) // MOE_TILE
    cum = jnp.cumsum(ntile)
    total = cum[-1]
    k = jnp.arange(MOE_TILES_MAX, dtype=i32)
    kk = jnp.minimum(k, jnp.maximum(total - 1, 0))
    grp = jnp.minimum(jnp.sum((kk[:, None] >= cum[None, :]).astype(i32), axis=1), N_GROUPS - 1)
    j = kk - (cum - ntile)[grp]
    blk = start[grp] // MOE_TILE + j
    nsub = jnp.clip((need[grp] - j * MOE_TILE + MOE_SUB - 1) // MOE_SUB, 0, MOE_TILE // MOE_SUB)
    nsub = jnp.where(k < total, nsub, 0)
    runs = tuple(a.reshape(-1).astype(i32) for a in (off, run0, padded))
    return runs, (blk.astype(i32), grp.astype(i32), nsub.astype(i32))


def _run_pieces(run_refs, tile, g):
    off_ref, run0_ref, padded_ref = run_refs
    idx = tile * N_GROUPS + g
    n = padded_ref[idx]
    src0 = run0_ref[idx]
    dst0 = off_ref[idx]
    pieces = []
    for b, bit in enumerate(RUN_BITS):
        done = n & (-2 * bit)
        pieces.append((b, (n & bit) != 0, pl.multiple_of(src0 + done, ROW_ALIGN),
                       pl.multiple_of(dst0 + done, ROW_ALIGN), bit))
    return pieces


def _moe_kernel(blk_ref, grp_ref, nsub_ref, h_ref, c_ref, w1_ref, w3_ref, w2_ref, o_ref, hid_ref):
    k = pl.program_id(0)
    c = pl.program_id(1)
    ns = nsub_ref[k]
    n_sub_max = MOE_TILE // MOE_SUB

    def for_row_count(fn):
        for m in range(1, n_sub_max + 1):
            @pl.when(ns == m)
            def _():
                fn(m * MOE_SUB)

    @pl.when(jnp.logical_and(ns > 0, c < N_FF_CHUNKS))
    def _():
        w1 = w1_ref[...].astype(BF16)
        w3 = w3_ref[...].astype(BF16)
        e_lane = EXPERT_LANE0 + grp_ref[k] * EXP_PER_GROUP + c * FF_CHUNK // D_EXPERT

        def up(rows):
            h = h_ref[0:rows, :]
            a = jnp.dot(h, w1, preferred_element_type=F32)
            b = jnp.dot(h, w3, preferred_element_type=F32)
            comb = c_ref[0:rows, :]
            lane = lax.broadcasted_iota(jnp.int32, comb.shape, 1)
            cw = jnp.sum(jnp.where(lane == e_lane, comb, 0.0), axis=-1, keepdims=True)
            hid_ref[c, 0:rows, :] = (_silu(a) * b * cw).astype(BF16)

        for_row_count(up)

    @pl.when(jnp.logical_and(ns > 0, c >= N_FF_CHUNKS))
    def _():
        w2 = w2_ref[...].reshape(EXP_PER_GROUP * D_EXPERT, OUT_CHUNK).astype(BF16)

        def down(rows):
            hid = jnp.concatenate([hid_ref[j, 0:rows, :] for j in range(N_FF_CHUNKS)], axis=1)
            o_ref[0:rows, :] = jnp.dot(hid, w2, preferred_element_type=F32).astype(BF16)
            if rows < MOE_TILE:
                o_ref[rows:, :] = jnp.zeros((MOE_TILE - rows, OUT_CHUNK), BF16)

        for_row_count(down)


def _moe_experts(hs, cs, w1, w3, w2, layer, plan):
    per_e = D_EXPERT // FF_CHUNK
    n_steps = N_FF_CHUNKS + N_OUT_CHUNKS

    def step_of(k, c, nsub_ref):
        return jnp.where(nsub_ref[k] > 0, c, n_steps - 1)

    def w13_map(k, c, blk_ref, grp_ref, nsub_ref):
        cc = jnp.minimum(step_of(k, c, nsub_ref), N_FF_CHUNKS - 1)
        return (layer, grp_ref[k] * EXP_PER_GROUP + cc // per_e, 0, cc % per_e)

    def out_chunk(k, c, nsub_ref):
        return jnp.maximum(step_of(k, c, nsub_ref) - N_FF_CHUNKS, 0)

    w2_map = lambda k, c, blk_ref, grp_ref, nsub_ref: (layer, grp_ref[k], 0, out_chunk(k, c, nsub_ref))
    row_map = lambda k, c, blk_ref, grp_ref, nsub_ref: (blk_ref[k], 0)
    out_map = lambda k, c, blk_ref, grp_ref, nsub_ref: (blk_ref[k], out_chunk(k, c, nsub_ref))
    grid_spec = pltpu.PrefetchScalarGridSpec(
        num_scalar_prefetch=3,
        grid=(MOE_TILES_MAX, n_steps),
        in_specs=[
            pl.BlockSpec((MOE_TILE, D), row_map),
            pl.BlockSpec((MOE_TILE, ROUTER_LANES), row_map),
            pl.BlockSpec((None, None, D, FF_CHUNK), w13_map),
            pl.BlockSpec((None, None, D, FF_CHUNK), w13_map),
            pl.BlockSpec((None, EXP_PER_GROUP, D_EXPERT, OUT_CHUNK), w2_map),
        ],
        out_specs=pl.BlockSpec((MOE_TILE, OUT_CHUNK), out_map),
        scratch_shapes=[pltpu.VMEM((N_FF_CHUNKS, MOE_TILE, FF_CHUNK), BF16)],
    )
    return pl.pallas_call(
        _moe_kernel,
        grid_spec=grid_spec,
        out_shape=jax.ShapeDtypeStruct((ROW_CAP, D), BF16),
        compiler_params=_params("arbitrary", "arbitrary"),
        name="moe_experts",
    )(*plan, hs, cs, w1, w3, w2)


def _combine_kernel(off_ref, run0_ref, padded_ref, x_ref, gate_ref, comb_ref, ys_ref, *refs,
                    split_out, emit_next):
    if emit_next:
        (ng_ref, nsc_ref, nsh_ref), refs = refs[:3], refs[3:]
    out_refs, (yw_ref, sem) = refs[:-2], refs[-2:]
    run_refs = (off_ref, run0_ref, padded_ref)
    i = pl.program_id(0)
    n_steps = pl.num_programs(0)
    slot = i % 2

    def for_each_copy(tile, sl, fn):
        for g in range(N_GROUPS):
            for b, pred, src, dst, rows in _run_pieces(run_refs, tile, g):
                @pl.when(pred)
                def _():
                    fn(pltpu.make_async_copy(ys_ref.at[pl.ds(dst, rows), :],
                                             yw_ref.at[sl, pl.ds(src, rows), :], sem.at[sl, g, b]))

    def fetch(tile, sl):
        yw_ref[sl, TOK_TILE:, :] = jnp.zeros((TILE_ROWS - TOK_TILE, D), BF16)
        for_each_copy(tile, sl, lambda cp: cp.start())

    @pl.when(i == 0)
    def _():
        fetch(i, slot)

    @pl.when(i + 1 < n_steps)
    def _():
        fetch(i + 1, 1 - slot)

    for_each_copy(i, slot, lambda cp: cp.wait())

    pos = comb_ref[:, POS_LANE:POS_LANE + 1].astype(jnp.int32)
    wcol = lax.broadcasted_iota(jnp.int32, (TOK_TILE, TILE_ROWS), 1)
    perm = jnp.where(wcol == pos, 1.0, 0.0).astype(BF16)
    y = jnp.dot(perm, yw_ref[slot], preferred_element_type=F32)
    out = x_ref[...] + gate_ref[...] * y
    if emit_next:
        out_refs[1][...] = _modnorm(out, ng_ref[...], nsc_ref[...], nsh_ref[...]).astype(BF16)
    if not split_out:
        out_refs[0][...] = out
    else:
        @pl.when(i < N_CTX // TOK_TILE)
        def _():
            out_refs[0][...] = out

        @pl.when(i >= N_CTX // TOK_TILE)
        def _():
            out_refs[1][...] = out


def _combine(x, mod, layer, comb, ys, runs, next_norm_g):
    n_ctx_tiles = N_CTX // TOK_TILE
    split_out = next_norm_g is None
    tile_spec = pl.BlockSpec((TOK_TILE, D), lambda i, *_: (i, 0))
    in_specs = [
        tile_spec,
        _mod_spec(layer, 5, TOK_TILE, row_axis=0),
        pl.BlockSpec((TOK_TILE, ROUTER_LANES), lambda i, *_: (i, 0)),
        pl.BlockSpec(memory_space=pl.ANY),
    ]
    args = [x, mod, comb, ys]
    if split_out:
        out_specs = [pl.BlockSpec((TOK_TILE, D), lambda i, *_: (jnp.minimum(i, n_ctx_tiles - 1), 0)),
                     pl.BlockSpec((TOK_TILE, D), lambda i, *_: (jnp.maximum(i - n_ctx_tiles, 0), 0))]
        out_shape = [jax.ShapeDtypeStruct((N_CTX, D), F32), jax.ShapeDtypeStruct((N_LAT, D), F32)]
    else:
        in_specs += [pl.BlockSpec((1, D), lambda i, *_: (0, 0)),
                     _mod_spec(layer + 1, 1, TOK_TILE, row_axis=0),
                     _mod_spec(layer + 1, 0, TOK_TILE, row_axis=0)]
        args += [next_norm_g.reshape(1, D), mod, mod]
        out_specs = [tile_spec, tile_spec]
        out_shape = [jax.ShapeDtypeStruct((N_TOK, D), F32), jax.ShapeDtypeStruct((N_TOK, D), BF16)]
    grid_spec = pltpu.PrefetchScalarGridSpec(
        num_scalar_prefetch=3,
        grid=(N_TOK_TILES,),
        in_specs=in_specs,
        out_specs=out_specs,
        scratch_shapes=[
            pltpu.VMEM((2, TILE_ROWS, D), BF16),
            pltpu.SemaphoreType.DMA((2, N_GROUPS, len(RUN_BITS))),
        ],
    )
    return pl.pallas_call(
        functools.partial(_combine_kernel, split_out=split_out, emit_next=not split_out),
        grid_spec=grid_spec,
        out_shape=out_shape,
        compiler_params=_params("arbitrary"),
        name="moe_combine",
    )(*runs, *args)


def _moe_experts_and_combine(x, routed, mod, layer, w1, w3, w2, next_norm_g):
    hs, cs, comb, cnt = routed
    runs, plan = _moe_plan(cnt[:, 0, :N_GROUPS].astype(jnp.int32))
    ys = _moe_experts(hs, cs, w1, w3, w2, layer, plan)
    return _combine(x, mod, layer, comb, ys, runs, next_norm_g)


def kernel(x_prompt, x_sample, cache_k, cache_v, c, c_ctx, ada_w, ada_b, norm_mix_g, norm_ffn_g,
           mix_in_w, pool_w, pool_scale, conv_w, conv_b, conv_ln_g, conv_ln_b, mix_out_w,
           qkv_w, q_norm_g, k_norm_g, rpb, attn_out_w, router_g_w, router_g_b, router_e_w,
           router_e_b, exp_w1, exp_w3, exp_w2):
    x = (x_prompt.reshape(N_CTX, D), x_sample.reshape(N_LAT, D))
    cvec8 = jnp.concatenate([c_ctx[None, :], c, jnp.zeros((8 - 1 - N_LAT_SEQ, D), F32)], axis=0)
    mod = _mod_table(_adaln_first_layer(cvec8, ada_w, ada_b))
    n_attn = DEPTH // 2
    cache_k2 = cache_k.reshape(N_LAT_SEQ, n_attn, CTX_SEQ, D)
    cache_v2 = cache_v.reshape(N_LAT_SEQ, n_attn, CTX_SEQ, D)

    caches = None
    h = None
    for l in range(DEPTH):
        e = l // 2
        if l % 2 == 0:
            u = _mm_prologue(x, mod, l, norm_mix_g[l], mix_in_w, e, h=h)
            mixed, *later_mod = _seq_mixer(u, pool_w[e].astype(BF16), pool_scale[e], conv_w[e], conv_b[e],
                                           conv_ln_g[e], conv_ln_b[e],
                                           ada=(cvec8, ada_w, ada_b) if l == 0 else None)
            if later_mod:
                mod = jnp.concatenate([mod, _mod_table(later_mod[0])], axis=0)
            out_w = mix_out_w
        else:
            q, k, v, new_k, new_v = _qkv_proj(h, e, qkv_w, q_norm_g[e], k_norm_g[e], caches)
            caches = (new_k, new_v)
            mixed = _ctx_attention(q, k, v)
            mixed = _na_attention(q, k, v, cache_k2[:, e], cache_v2[:, e], rpb[e], mixed)
            out_w = attn_out_w
        rw, rb = _router_weights(router_g_w[l], router_g_b[l], router_e_w[l], router_e_b[l])
        x, *routed = _out_proj_and_route(mixed, out_w, e, x, mod, l, norm_ffn_g[l], rw, rb)
        last = l == DEPTH - 1
        x, h = _moe_experts_and_combine(x, routed, mod, l, exp_w1, exp_w3, exp_w2,
                                        next_norm_g=None if last else norm_mix_g[l + 1])

    y_prompt = x.reshape(N_CTX_SEQ, CTX_SEQ, D)
    y_sample = h.reshape(N_LAT_SEQ, LAT_SEQ, D)
    cache_dims = (N_CTX_SEQ, n_attn, CTX_SEQ, N_HEADS, HEAD_DIM)
    return (y_prompt, y_sample, caches[0].reshape(cache_dims), caches[1].reshape(cache_dims))
```

```python
import functools

import numpy as np
import jax
import jax.numpy as jnp
from jax import lax
from jax.experimental import pallas as pl
from jax.experimental.pallas import tpu as pltpu

D = 2048
N_CTX_SEQ = 16
CTX_SEQ = 256
N_LAT_SEQ = 2
LAT_SEQ = 1024
N_CTX = N_CTX_SEQ * CTX_SEQ
N_LAT = N_LAT_SEQ * LAT_SEQ
N_TOK = N_CTX + N_LAT
DEPTH = 4
GRID_W = 64
GRID_H = LAT_SEQ // GRID_W
N_HEADS = 16
HEAD_DIM = 128
ATTN_SCALE = HEAD_DIM ** -0.5
D_POOL = 1024
D_CONV = 1024
POOL_WINDOWS = (2, 4, 8, 16)
POOL_GC = 256
CONV_W = 31
NA_KH = 8
NA_KW = 16
N_GROUPS = 4
EXP_PER_GROUP = 4
N_EXPERTS = 16
D_EXPERT = 512
EPS = 1e-6
NEG = -1e30

LANES = 128
SUBLANES = 8
SEQ_TILE = 256
HALO = 16
ROUTER_LANES = 128
EXPERT_LANE0 = N_GROUPS
POS_LANE = EXPERT_LANE0 + N_EXPERTS
LOW_LANE0 = 32
VMEM_LIMIT = 56 * 1024 * 1024

TOK_TILE = 256
N_TOK_TILES = N_TOK // TOK_TILE
ROW_ALIGN = 16
RUN_BITS = (256, 128, 64, 32, 16)
TILE_ROWS = TOK_TILE + N_GROUPS * ROW_ALIGN
MOE_TILE = 1024
MOE_SUB = 128
FF_CHUNK = 512
OUT_CHUNK = 512
N_FF_CHUNKS = EXP_PER_GROUP * D_EXPERT // FF_CHUNK
N_OUT_CHUNKS = D // OUT_CHUNK
ZERO_ROWS = MOE_SUB
MAX_PADDED = N_TOK + N_TOK_TILES * N_GROUPS * (ROW_ALIGN - 1)
MOE_TILES_MAX = (MAX_PADDED + N_GROUPS * (MOE_TILE - 1)) // MOE_TILE
GROUP_CAP = -(-(N_TOK + N_TOK_TILES * (ROW_ALIGN - 1) + ZERO_ROWS) // MOE_TILE) * MOE_TILE
ROW_CAP = N_GROUPS * GROUP_CAP

F32 = jnp.float32
BF16 = jnp.bfloat16


def _cond_of_row(row):
    row = jnp.minimum(row, N_TOK - 1)
    return jnp.where(row < N_CTX, 0, 1 + (row - N_CTX) // LAT_SEQ)


def _mod_spec(layer, which, tm, tn=D, row_axis=1, col_axis=None):
    def index_map(*ids):
        cond = _cond_of_row(ids[row_axis] * tm)
        col = 0 if col_axis is None else ids[col_axis]
        return (layer, cond, which, 0, col)
    return pl.BlockSpec((None, None, None, 1, tn), index_map)


def _silu(x):
    return x / (1.0 + jnp.exp(-x))


def _modnorm(x, g, sc, sh):
    ms = jnp.mean(x * x, axis=-1, keepdims=True)
    y = x * lax.rsqrt(ms + EPS) * g
    return y * (1.0 + sc) + sh


def _params(*sem):
    return pltpu.CompilerParams(dimension_semantics=sem, vmem_limit_bytes=VMEM_LIMIT)


def _ada_kernel(c_ref, w_ref, b_ref, o_ref):
    s = _silu(c_ref[...]).astype(BF16)
    o_ref[...] = jnp.dot(s, w_ref[...].astype(BF16), preferred_element_type=F32) + b_ref[...]


N_MOD = 6 * D


def _mod_table(raw):
    return raw[:, :3].reshape(raw.shape[0], 3, 6, 1, D)


def _adaln_first_layer(cvec8, ada_w, ada_b):
    tn = 1024
    return pl.pallas_call(
        _ada_kernel,
        grid=(N_MOD // tn,),
        in_specs=[
            pl.BlockSpec((8, D), lambda j: (0, 0)),
            pl.BlockSpec((None, D, tn), lambda j: (0, 0, j)),
            pl.BlockSpec((None, 1, tn), lambda j: (0, 0, j)),
        ],
        out_specs=pl.BlockSpec((None, 8, tn), lambda j: (0, 0, j)),
        out_shape=jax.ShapeDtypeStruct((1, 8, N_MOD), F32),
        compiler_params=_params("arbitrary"),
        name="adaln",
    )(cvec8, ada_w, ada_b.reshape(DEPTH, 1, N_MOD))


def _token_specs(x, tm, tn, idx):
    if not isinstance(x, tuple):
        return [pl.BlockSpec((tm, tn), lambda j, i: idx(i, j))], [x]
    n_ctx_tiles = N_CTX // tm
    ctx_map = lambda j, i: idx(jnp.minimum(i, n_ctx_tiles - 1), j)
    lat_map = lambda j, i: idx(jnp.maximum(i - n_ctx_tiles, 0), j)
    return [pl.BlockSpec((tm, tn), ctx_map), pl.BlockSpec((tm, tn), lat_map)], list(x)


def _token_rows(refs, i, tm):
    if len(refs) == 1:
        return refs[0][...]
    return jnp.where(i < N_CTX // tm, refs[0][...], refs[1][...])


def _cast_weights_once(i, pairs):
    @pl.when(i == 0)
    def _():
        for w_ref, wb_ref in pairs:
            wb_ref[...] = w_ref[...].astype(BF16)


def _mm_pro_kernel(*refs, n_x, tm):
    x_refs = refs[:n_x]
    g_ref, sc_ref, sh_ref, w_ref, o_ref, wb_ref = refs[n_x:]
    i = pl.program_id(1)
    _cast_weights_once(i, [(w_ref, wb_ref)])
    h = _modnorm(_token_rows(x_refs, i, tm), g_ref[...], sc_ref[...], sh_ref[...]).astype(BF16)
    o_ref[...] = jnp.dot(h, wb_ref[...], preferred_element_type=F32)


def _mm_plain_kernel(h_ref, w_ref, o_ref, wb_ref):
    _cast_weights_once(pl.program_id(1), [(w_ref, wb_ref)])
    o_ref[...] = jnp.dot(h_ref[...], wb_ref[...], preferred_element_type=F32)


def _mm_prologue(x, mod, layer, norm_g, w, w_idx, h=None, tm=512, tn=1024):
    f = w.shape[2]
    if h is not None:
        tm = 2 * tm
    w_spec = pl.BlockSpec((None, D, tn), lambda j, i: (w_idx, 0, j))
    if h is None:
        x_specs, x_args = _token_specs(x, tm, D, lambda i, j: (i, 0))
        body = functools.partial(_mm_pro_kernel, n_x=len(x_args), tm=tm)
        in_specs = x_specs + [pl.BlockSpec((1, D), lambda j, i: (0, 0)),
                              _mod_spec(layer, 1, tm), _mod_spec(layer, 0, tm), w_spec]
        args = x_args + [norm_g.reshape(1, D), mod, mod, w]
    else:
        body = _mm_plain_kernel
        in_specs = [pl.BlockSpec((tm, D), lambda j, i: (i, 0)), w_spec]
        args = [h, w]
    return pl.pallas_call(
        body,
        grid=(f // tn, N_TOK // tm),
        in_specs=in_specs,
        out_specs=pl.BlockSpec((tm, tn), lambda j, i: (i, j)),
        out_shape=jax.ShapeDtypeStruct((N_TOK, f), F32),
        scratch_shapes=[pltpu.VMEM((D, tn), BF16)],
        compiler_params=_params("arbitrary", "arbitrary"),
        name="mm_prologue",
    )(*args)


def _qkv_kernel(h_ref, wq_ref, wk_ref, wv_ref, qg_ref, kg_ref, *rest, tn, n_ctx_tiles, seqs_per_tile):
    q_ref, k_ref, v_ref, ck_ref, cv_ref, wqb_ref, wkb_ref, wvb_ref, kn_ref, vf_ref = rest[-10:]
    i = pl.program_id(1)
    _cast_weights_once(i, [(wq_ref, wqb_ref), (wk_ref, wkb_ref), (wv_ref, wvb_ref)])
    h = h_ref[...]
    q = jnp.dot(h, wqb_ref[...], preferred_element_type=F32)
    k = jnp.dot(h, wkb_ref[...], preferred_element_type=F32)
    v = jnp.dot(h, wvb_ref[...], preferred_element_type=F32)
    for hh in range(tn // HEAD_DIM):
        ls = slice(hh * HEAD_DIM, (hh + 1) * HEAD_DIM)
        qh = q[:, ls]
        kh = k[:, ls]
        qn = qh * lax.rsqrt(jnp.mean(qh * qh, axis=-1, keepdims=True) + EPS) * qg_ref[:, ls]
        kn = kh * lax.rsqrt(jnp.mean(kh * kh, axis=-1, keepdims=True) + EPS) * kg_ref[:, ls]
        q_ref[:, ls] = qn.astype(BF16)
        k_ref[:, ls] = kn.astype(BF16)
        kn_ref[:, ls] = kn
    v_ref[...] = v.astype(BF16)
    vf_ref[...] = v

    @pl.when(i < n_ctx_tiles)
    def _():
        ck_ref[...] = kn_ref[...].reshape(seqs_per_tile, CTX_SEQ, tn)
        cv_ref[...] = vf_ref[...].reshape(seqs_per_tile, CTX_SEQ, tn)


def _qkv_proj(h, w_idx, w, q_gain, k_gain, caches, tm=512, tn=512):
    n = h.shape[0]
    n_attn = DEPTH // 2
    ncol = D // tn
    seqs_per_tile = tm // CTX_SEQ
    n_ctx_tiles = N_CTX // tm
    cache_shape = jax.ShapeDtypeStruct((N_CTX_SEQ, n_attn, CTX_SEQ, D), F32)
    cache_spec = pl.BlockSpec(
        (seqs_per_tile, None, CTX_SEQ, tn),
        lambda j, i: (jnp.minimum(i, n_ctx_tiles - 1), w_idx, 0, j))
    in_specs = [
        pl.BlockSpec((tm, D), lambda j, i: (i, 0)),
        pl.BlockSpec((None, D, tn), lambda j, i: (w_idx, 0, j)),
        pl.BlockSpec((None, D, tn), lambda j, i: (w_idx, 0, ncol + j)),
        pl.BlockSpec((None, D, tn), lambda j, i: (w_idx, 0, 2 * ncol + j)),
        pl.BlockSpec((1, tn), lambda j, i: (0, j)),
        pl.BlockSpec((1, tn), lambda j, i: (0, j)),
    ]
    args = [h, w, w, w,
            jnp.tile(q_gain, N_HEADS).reshape(1, D), jnp.tile(k_gain, N_HEADS).reshape(1, D)]
    aliases = {}
    if caches is not None:
        in_specs += [pl.BlockSpec(memory_space=pl.ANY), pl.BlockSpec(memory_space=pl.ANY)]
        aliases = {len(args): 3, len(args) + 1: 4}
        args += list(caches)
    act_spec = pl.BlockSpec((tm, tn), lambda j, i: (i, j))
    act_shape = jax.ShapeDtypeStruct((n, D), BF16)
    return pl.pallas_call(
        functools.partial(_qkv_kernel, tn=tn, n_ctx_tiles=n_ctx_tiles, seqs_per_tile=seqs_per_tile),
        grid=(ncol, n // tm),
        in_specs=in_specs,
        out_specs=[act_spec, act_spec, act_spec, cache_spec, cache_spec],
        out_shape=[act_shape, act_shape, act_shape, cache_shape, cache_shape],
        input_output_aliases=aliases,
        scratch_shapes=[pltpu.VMEM((D, tn), BF16)] * 3 + [pltpu.VMEM((tm, tn), F32)] * 2,
        compiler_params=_params("arbitrary", "arbitrary"),
        name="qkv_proj",
    )(*args)


def _seq_mix_tile(i, u_ref, top_ref, bot_ref, pw_ref, ps_ref, cw_ref, cb_ref, lg_ref, lb_ref,
                  o_ref, zpad_ref, conv_ref):
    n_ctx_tiles = N_CTX // SEQ_TILE
    tiles_per_lat = LAT_SEQ // SEQ_TILE
    is_lat = i >= n_ctx_tiles
    chunk = jnp.where(is_lat, (i - n_ctx_tiles) % tiles_per_lat, 0)
    top_ok = jnp.logical_and(is_lat, chunk > 0)
    bot_ok = jnp.logical_and(is_lat, chunk < tiles_per_lat - 1)
    seq_len = jnp.where(is_lat, LAT_SEQ, CTX_SEQ)
    t = chunk * SEQ_TILE + lax.broadcasted_iota(jnp.int32, (SEQ_TILE, 1), 0)
    rows = SEQ_TILE + 2 * HALO

    for g, win in enumerate(POOL_WINDOWS):
        ls = slice(g * POOL_GC, (g + 1) * POOL_GC)
        mid = u_ref[:, ls]
        top = jnp.where(top_ok, top_ref[:, ls], 0.0)
        bot = jnp.where(bot_ok, bot_ref[:, ls], 0.0)
        up = jnp.concatenate([top, mid, bot], axis=0)
        s = pltpu.roll(up, 1, 0) + up
        if win >= 4:
            s = pltpu.roll(s, 1, 0) + pltpu.roll(s, rows - 1, 0)
        if win >= 8:
            s = pltpu.roll(s, 2, 0) + pltpu.roll(s, rows - 2, 0)
        if win >= 16:
            s = pltpu.roll(s, 4, 0) + pltpu.roll(s, rows - 4, 0)
        s = s[HALO:HALO + SEQ_TILE]
        lo = jnp.maximum(t - win // 2, 0)
        hi = jnp.minimum(t + (win - win // 2) - 1, seq_len - 1)
        cnt = (hi - lo + 1).astype(F32)
        p = (s / cnt - mid).astype(BF16)
        y = jnp.dot(p, pw_ref[g], preferred_element_type=F32) * ps_ref[:, ls]
        o_ref[:, ls] = y.astype(BF16)

    def glu(ref):
        return ref[:, D_POOL:D_POOL + D_CONV] * (1.0 / (1.0 + jnp.exp(-ref[:, D_POOL + D_CONV:])))

    zpad_ref[0:HALO, :] = jnp.where(top_ok, glu(top_ref), 0.0)
    zpad_ref[HALO:HALO + SEQ_TILE, :] = glu(u_ref)
    zpad_ref[HALO + SEQ_TILE:rows, :] = jnp.where(bot_ok, glu(bot_ref), 0.0)
    for lt in range(D_CONV // LANES):
        ls = slice(lt * LANES, (lt + 1) * LANES)
        zp = zpad_ref[:, ls]
        acc = jnp.zeros((SEQ_TILE, LANES), F32)
        for b in range(SUBLANES):
            sb = zp if b == 0 else pltpu.roll(zp, rows - b, 0)
            for a in range(-(-(CONV_W + 1) // SUBLANES)):
                off = SUBLANES * a + b
                if 1 <= off <= CONV_W:
                    acc = acc + cw_ref[off - 1:off, ls] * sb[SUBLANES * a:SUBLANES * a + SEQ_TILE]
        conv_ref[:, ls] = acc + cb_ref[:, ls]
    zc = conv_ref[...]
    mu = jnp.mean(zc, axis=-1, keepdims=True)
    d = zc - mu
    var = jnp.mean(d * d, axis=-1, keepdims=True)
    zn = d * lax.rsqrt(var + EPS) * lg_ref[...] + lb_ref[...]
    o_ref[:, D_POOL:] = _silu(zn).astype(BF16)


N_SEQ_INPUTS = 9


def _seq_kernel(*refs, with_ada):
    seq_in, refs = refs[:N_SEQ_INPUTS], refs[N_SEQ_INPUTS:]
    if with_ada:
        (c_ref, w_ref, b_ref), refs = refs[:3], refs[3:]
        o_ref, mod_ref, zpad_ref, conv_ref = refs
        _ada_kernel(c_ref, w_ref, b_ref, mod_ref)
    else:
        o_ref, zpad_ref, conv_ref = refs
    _seq_mix_tile(pl.program_id(0), *seq_in, o_ref, zpad_ref, conv_ref)


def _seq_mixer(u, pool_w_bf16, pool_scale, conv_w, conv_b, ln_g, ln_b, ada=None):
    n = u.shape[0]
    f_in = u.shape[1]
    n_steps = n // SEQ_TILE
    hb = SEQ_TILE // HALO
    n_hblocks = n // HALO
    const = lambda *shape: pl.BlockSpec(shape, lambda i: (0,) * len(shape))
    in_specs = [
        pl.BlockSpec((SEQ_TILE, f_in), lambda i: (i, 0)),
        pl.BlockSpec((HALO, f_in), lambda i: (jnp.maximum(i * hb - 1, 0), 0)),
        pl.BlockSpec((HALO, f_in), lambda i: (jnp.minimum((i + 1) * hb, n_hblocks - 1), 0)),
        const(len(POOL_WINDOWS), POOL_GC, POOL_GC),
        const(1, D_POOL), const(CONV_W, D_CONV), const(1, D_CONV), const(1, D_CONV), const(1, D_CONV),
    ]
    args = [u, u, u, pool_w_bf16, pool_scale.reshape(1, D_POOL), conv_w, conv_b.reshape(1, D_CONV),
            ln_g.reshape(1, D_CONV), ln_b.reshape(1, D_CONV)]
    out_specs = [pl.BlockSpec((SEQ_TILE, D_POOL + D_CONV), lambda i: (i, 0))]
    out_shape = [jax.ShapeDtypeStruct((n, D_POOL + D_CONV), BF16)]
    if ada is not None:
        cvec8, ada_w, ada_b = ada
        later = DEPTH - 1
        chunk = later * N_MOD // n_steps
        per_layer = N_MOD // chunk
        chunk_map = lambda i: (1 + i // per_layer, 0, i % per_layer)
        in_specs += [const(8, D),
                     pl.BlockSpec((None, D, chunk), chunk_map),
                     pl.BlockSpec((None, 1, chunk), chunk_map)]
        args += [cvec8, ada_w, ada_b.reshape(DEPTH, 1, N_MOD)]
        out_specs.append(pl.BlockSpec((None, 8, chunk), lambda i: (i // per_layer, 0, i % per_layer)))
        out_shape.append(jax.ShapeDtypeStruct((later, 8, N_MOD), F32))
    return pl.pallas_call(
        functools.partial(_seq_kernel, with_ada=ada is not None),
        grid=(n_steps,),
        in_specs=in_specs,
        out_specs=out_specs,
        out_shape=out_shape,
        scratch_shapes=[
            pltpu.VMEM((SEQ_TILE + 2 * HALO, D_CONV), F32),
            pltpu.VMEM((SEQ_TILE, D_CONV), F32),
        ],
        compiler_params=_params("arbitrary"),
        name="seq_mixer",
    )(*args)


_NT = (((1,), (1,)), ((), ()))


def _ctx_attn_kernel(q_ref, k_ref, v_ref, o_ref, s_ref, p_ref):
    for h in range(N_HEADS):
        ls = slice(h * HEAD_DIM, (h + 1) * HEAD_DIM)
        s_ref[h] = lax.dot_general(q_ref[:, ls], k_ref[:, ls], _NT, preferred_element_type=F32) * ATTN_SCALE
    for h in range(N_HEADS):
        s = s_ref[h]
        e = jnp.exp(s - jnp.max(s, axis=-1, keepdims=True))
        p_ref[h] = (e * (1.0 / jnp.sum(e, axis=-1, keepdims=True))).astype(BF16)
    for h in range(N_HEADS):
        ls = slice(h * HEAD_DIM, (h + 1) * HEAD_DIM)
        o_ref[:, ls] = jnp.dot(p_ref[h], v_ref[:, ls], preferred_element_type=F32).astype(BF16)


def _ctx_attention(q, k, v):
    n = q.shape[0]
    spec = pl.BlockSpec((CTX_SEQ, D), lambda b: (b, 0))
    return pl.pallas_call(
        _ctx_attn_kernel,
        grid=(N_CTX_SEQ,),
        in_specs=[spec, spec, spec],
        out_specs=spec,
        out_shape=jax.ShapeDtypeStruct((n, D), BF16),
        scratch_shapes=[pltpu.VMEM((N_HEADS, CTX_SEQ, CTX_SEQ), F32),
                        pltpu.VMEM((N_HEADS, CTX_SEQ, CTX_SEQ), BF16)],
        compiler_params=_params("arbitrary"),
        name="ctx_attention",
    )(q, k, v)


def _na_window_mask():
    cq = np.arange(GRID_W)[:, None]
    kc = np.arange(GRID_W)[None, :]
    win0 = np.clip(cq - NA_KW // 2, 0, GRID_W - NA_KW)
    mask = ((kc >= win0) & (kc < win0 + NA_KW)).astype(np.float32)
    return np.tile(mask, (1, NA_KH))


def _na_bias_rows(rpb_e):
    centre = NA_KW - 1
    pad = jnp.zeros(rpb_e.shape[:2] + (2 * GRID_W - (2 * NA_KW - 1),), F32)
    return jnp.concatenate([rpb_e[..., centre:], pad, rpb_e[..., :centre]], axis=-1)


def _na_key_row0(r):
    return min(max(r - NA_KH // 2, 0), GRID_H - NA_KH)


def _na_segments():
    segs = []
    for r in range(GRID_H):
        kr0 = _na_key_row0(r)
        if segs and segs[-1][2] == kr0:
            segs[-1][1] = r + 1
        else:
            segs.append([r, r + 1, kr0])
    return segs


def _na_kernel(q_ref, k_ref, v_ref, kc_ref, vc_ref, rows_ref, mask_ref, o_in_ref, o_ref,
               s_ref, p_ref, acc_ref, tz_ref):
    del o_in_ref
    band = NA_KH * GRID_W
    n_ctx = CTX_SEQ
    kc = kc_ref[...].astype(BF16)
    vc = vc_ref[...].astype(BF16)
    @pl.when(pl.program_id(1) == 0)
    def _():
        n_dr = 2 * NA_KH - 1
        toeplitz = [pltpu.roll(jnp.broadcast_to(rows_ref[d:d + 1, :], (GRID_W, 2 * GRID_W)), 0, 1,
                               stride=1, stride_axis=0) for d in range(n_dr)]
        left = lax.broadcasted_iota(jnp.int32, (GRID_W, 2 * GRID_W), 1) < GRID_W
        for d in range(n_dr - 1):
            tz_ref[d] = jnp.where(left, toeplitz[d], pltpu.roll(toeplitz[d + 1], GRID_W, 1))
    for r0, r1, kr0 in _na_segments():
        rs = slice(r0 * GRID_W, r1 * GRID_W)
        kb = k_ref[kr0 * GRID_W:kr0 * GRID_W + band, :]
        s_ref[rs, 0:band] = lax.dot_general(q_ref[rs, :], kb, _NT, preferred_element_type=F32) * ATTN_SCALE
    s_ref[:, band:band + n_ctx] = lax.dot_general(q_ref[...], kc, _NT, preferred_element_type=F32) * ATTN_SCALE
    mask = mask_ref[...] > 0.5
    for r in range(GRID_H):
        rs = slice(r * GRID_W, (r + 1) * GRID_W)
        kr0 = _na_key_row0(r)
        bias = jnp.concatenate(
            [tz_ref[kr0 + j - r + NA_KH - 1] for j in range(0, NA_KH, 2)], axis=-1)
        s_loc = jnp.where(mask, s_ref[rs, 0:band] + bias, NEG)
        s_ctx = s_ref[rs, band:band + n_ctx]
        m = jnp.maximum(jnp.max(s_loc, axis=-1, keepdims=True), jnp.max(s_ctx, axis=-1, keepdims=True))
        e_loc = jnp.exp(s_loc - m)
        e_ctx = jnp.exp(s_ctx - m)
        inv = 1.0 / (jnp.sum(e_loc, axis=-1, keepdims=True) + jnp.sum(e_ctx, axis=-1, keepdims=True))
        p_ref[rs, 0:band] = (e_loc * inv).astype(BF16)
        p_ref[rs, band:band + n_ctx] = (e_ctx * inv).astype(BF16)
    for r0, r1, kr0 in _na_segments():
        rs = slice(r0 * GRID_W, r1 * GRID_W)
        vb = v_ref[kr0 * GRID_W:kr0 * GRID_W + band, :]
        acc_ref[rs, :] = jnp.dot(p_ref[rs, 0:band], vb, preferred_element_type=F32)
    o_ctx = jnp.dot(p_ref[:, band:band + n_ctx], vc, preferred_element_type=F32)
    o_ref[...] = (acc_ref[...] + o_ctx).astype(BF16)


def _na_attention(q, k, v, cache_k_e, cache_v_e, rpb_e, o_ctx):
    mask = _na_window_mask()
    lat0 = N_CTX // LAT_SEQ
    n_dr = 2 * NA_KH - 1
    n_keys = NA_KH * GRID_W + CTX_SEQ
    tok_spec = pl.BlockSpec((LAT_SEQ, HEAD_DIM), lambda h, b: (lat0 + b, h))
    cache_spec = pl.BlockSpec((None, CTX_SEQ, HEAD_DIM), lambda h, b: (b, 0, h))
    return pl.pallas_call(
        _na_kernel,
        grid=(N_HEADS, N_LAT_SEQ),
        in_specs=[
            tok_spec, tok_spec, tok_spec, cache_spec, cache_spec,
            pl.BlockSpec((None, n_dr, 2 * GRID_W), lambda h, b: (h, 0, 0)),
            pl.BlockSpec((GRID_W, NA_KH * GRID_W), lambda h, b: (0, 0)),
            pl.BlockSpec(memory_space=pl.ANY),
        ],
        out_specs=tok_spec,
        out_shape=jax.ShapeDtypeStruct(o_ctx.shape, o_ctx.dtype),
        input_output_aliases={7: 0},
        scratch_shapes=[pltpu.VMEM((LAT_SEQ, n_keys), F32),
                        pltpu.VMEM((LAT_SEQ, n_keys), BF16),
                        pltpu.VMEM((LAT_SEQ, HEAD_DIM), F32),
                        pltpu.VMEM((n_dr - 1, GRID_W, 2 * GRID_W), F32)],
        compiler_params=_params("arbitrary", "arbitrary"),
        name="na_attention",
    )(q, k, v, cache_k_e, cache_v_e, _na_bias_rows(rpb_e), jnp.asarray(mask), o_ctx)


def _split_bf16(x, pieces):
    out = []
    for _ in range(pieces):
        p = x.astype(BF16)
        out.append(p)
        x = x - p.astype(F32)
    return out


def _moe_prenorm(x, g_ref, sc_ref, sh_ref):
    h = _modnorm(x, g_ref[...], sc_ref[...], sh_ref[...])
    return jnp.concatenate(_split_bf16(h, 2), axis=1)


def _router_logits(h_pieces, rw_ref, rb_ref):
    prod = jnp.dot(h_pieces, rw_ref[...], preferred_element_type=F32)
    return prod + pltpu.roll(prod, ROUTER_LANES - LOW_LANE0, 1) + rb_ref[...]


def _first_argmax(vals, valid, lane):
    v = jnp.where(valid, vals, -jnp.inf)
    m = jnp.max(v, axis=-1, keepdims=True)
    idx = jnp.min(jnp.where(jnp.logical_and(valid, v == m), lane, jnp.int32(1 << 20)),
                  axis=-1, keepdims=True)
    return m, idx


def _route_group(logits):
    lane = lax.broadcasted_iota(jnp.int32, logits.shape, 1)
    mg, gi = _first_argmax(logits, lane < N_GROUPS, lane)
    onehot_g = jnp.where(lane == gi, 1.0, 0.0)
    row = lax.broadcasted_iota(jnp.int32, (TOK_TILE, TOK_TILE), 0)
    col = lax.broadcasted_iota(jnp.int32, (TOK_TILE, TOK_TILE), 1)
    earlier = jnp.where(col < row, 1.0, 0.0).astype(BF16)
    rank = jnp.dot(earlier, onehot_g.astype(BF16), preferred_element_type=F32)
    return mg, gi, onehot_g, rank


def _route_weights(logits, group, comb_ref, cnt_ref):
    mg, gi, onehot_g, rank = group
    lane = lax.broadcasted_iota(jnp.int32, logits.shape, 1)
    is_g = lane < N_GROUPS
    pg = 1.0 / jnp.sum(jnp.where(is_g, jnp.exp(logits - mg), 0.0), axis=-1, keepdims=True)
    e_lane0 = EXPERT_LANE0 + gi * EXP_PER_GROUP
    is_e = jnp.logical_and(lane >= e_lane0, lane < e_lane0 + EXP_PER_GROUP)
    m1, i1 = _first_argmax(logits, is_e, lane)
    m2, i2 = _first_argmax(logits, jnp.logical_and(is_e, lane != i1), lane)
    e2 = jnp.exp(m2 - m1)
    den = 1.0 + e2
    w1 = (1.0 / den) * pg
    w2 = (e2 / den) * pg

    cnt = jnp.sum(onehot_g, axis=0, keepdims=True)
    padded = jnp.floor((cnt + (ROW_ALIGN - 1)) * (1.0 / ROW_ALIGN)) * ROW_ALIGN
    padded8 = jnp.broadcast_to(padded, (SUBLANES, ROUTER_LANES))
    run0 = (pltpu.roll(padded8, 1, 1) + pltpu.roll(padded8, 2, 1) + pltpu.roll(padded8, 3, 1))[0:1, :]
    pos = jnp.sum(onehot_g * (rank + run0), axis=-1, keepdims=True)

    comb = jnp.where(lane == i1, w1, jnp.where(lane == i2, w2, jnp.where(lane == POS_LANE, pos, onehot_g)))
    comb_ref[...] = comb
    cnt_ref[...] = cnt
    run_rows = [jnp.sum(jnp.where(lane[0:1, :] == g, padded, 0.0)).astype(jnp.int32) for g in range(N_GROUPS)]
    return comb, run_rows


def _sort_tile(h_hi, comb, hc_ref, cc_ref):
    pos_t = comb.T[POS_LANE:POS_LANE + 1, :]
    dest = lax.broadcasted_iota(jnp.int32, (TILE_ROWS, TOK_TILE), 0).astype(F32)
    perm = jnp.where(pos_t == dest, 1.0, 0.0).astype(BF16)
    hc_ref[...] = jnp.dot(perm, h_hi, preferred_element_type=F32).astype(BF16)
    c_parts = _split_bf16(comb, 3)
    cc_ref[...] = (jnp.dot(perm, c_parts[0], preferred_element_type=F32)
                   + jnp.dot(perm, c_parts[1], preferred_element_type=F32)
                   + jnp.dot(perm, c_parts[2], preferred_element_type=F32))


W_STAGE_COLS = 512


def _out_route_kernel(a_ref, w_hbm, gate_ref, g_ref, sc_ref, sh_ref, rw_ref, rb_ref, *refs, n_x, w_idx):
    x_refs = refs[:n_x]
    (xo_ref, hs_ref, cs_ref, comb_ref, cnt_ref, wb_ref, stage_ref, prev_ref, hcs_ref, ccs_ref,
     zh_ref, zc_ref, fill_ref, sent_ref, sem, sem_run, sem_zero) = refs[n_x:]
    i = pl.program_id(0)
    slot = i % 2

    @pl.when(i == 0)
    def _():
        prev_ref[...] = jnp.zeros_like(prev_ref)
        zh_ref[...] = jnp.zeros_like(zh_ref)
        zc_ref[...] = jnp.zeros_like(zc_ref)
        for g in range(N_GROUPS):
            fill_ref[g] = 0
        n_chunks = D // W_STAGE_COLS

        def chunk_copy(c):
            cols = pl.ds(c * W_STAGE_COLS, W_STAGE_COLS)
            return pltpu.make_async_copy(w_hbm.at[w_idx, :, cols], stage_ref.at[c % 2], sem.at[c % 2])

        chunk_copy(0).start()
        chunk_copy(1).start()
        for c in range(n_chunks):
            chunk_copy(c).wait()
            wb_ref[:, c * W_STAGE_COLS:(c + 1) * W_STAGE_COLS] = stage_ref[c % 2].astype(BF16)
            if c + 2 < n_chunks:
                chunk_copy(c + 2).start()

    h_prev = prev_ref[...]
    logits = _router_logits(h_prev, rw_ref, rb_ref)
    group = _route_group(logits)
    y = jnp.dot(a_ref[...], wb_ref[...], preferred_element_type=F32)
    x_mid = _token_rows(x_refs, jnp.minimum(i, N_TOK_TILES - 1), TOK_TILE) + gate_ref[...] * y
    xo_ref[...] = x_mid
    comb, run_rows = _route_weights(logits, group, comb_ref, cnt_ref)
    _sort_tile(h_prev[:, :D], comb, hcs_ref.at[slot], ccs_ref.at[slot])
    prev_ref[...] = _moe_prenorm(x_mid, g_ref, sc_ref, sh_ref)

    def for_each_copy(sl, rows, filled, fn):
        src0 = 0
        for g in range(N_GROUPS):
            for b, bit in enumerate(RUN_BITS):
                done = rows[g] & (-2 * bit)

                @pl.when((rows[g] & bit) != 0)
                def _():
                    src = pl.ds(pl.multiple_of(src0 + done, ROW_ALIGN), bit)
                    dst = pl.ds(pl.multiple_of(g * GROUP_CAP + filled[g] + done, ROW_ALIGN), bit)
                    fn(pltpu.make_async_copy(hcs_ref.at[sl, src, :], hs_ref.at[dst, :], sem_run.at[sl, 0, g, b]))
                    fn(pltpu.make_async_copy(ccs_ref.at[sl, src, :], cs_ref.at[dst, :], sem_run.at[sl, 1, g, b]))

            src0 = src0 + rows[g]

    @pl.when(i >= 1)
    def _():
        filled = [fill_ref[g] for g in range(N_GROUPS)]
        for_each_copy(slot, run_rows, filled, lambda cp: cp.start())
        for g in range(N_GROUPS):
            fill_ref[g] = filled[g] + run_rows[g]

        @pl.when(i >= 2)
        def _():
            sent = [sent_ref[g] for g in range(N_GROUPS)]
            for_each_copy(1 - slot, sent, [0] * N_GROUPS, lambda cp: cp.wait())

        for g in range(N_GROUPS):
            sent_ref[g] = run_rows[g]

    @pl.when(i == pl.num_programs(0) - 1)
    def _():
        for_each_copy(slot, [sent_ref[g] for g in range(N_GROUPS)], [0] * N_GROUPS, lambda cp: cp.wait())
        zero_copies = []
        for g in range(N_GROUPS):
            dst = pl.ds(pl.multiple_of(g * GROUP_CAP + fill_ref[g], ROW_ALIGN), ZERO_ROWS)
            zero_copies.append(pltpu.make_async_copy(zh_ref, hs_ref.at[dst, :], sem_zero.at[0, g]))
            zero_copies.append(pltpu.make_async_copy(zc_ref, cs_ref.at[dst, :], sem_zero.at[1, g]))
        for cp in zero_copies:
            cp.start()
        for cp in zero_copies:
            cp.wait()


def _out_proj_and_route(a_bf16, w, w_idx, x, mod, layer, norm_g, rw, rb):
    tm = TOK_TILE
    k = a_bf16.shape[1]
    last = N_TOK_TILES - 1
    proj = lambda i: jnp.minimum(i, last)
    routed = lambda i: jnp.maximum(i - 1, 0)
    if isinstance(x, tuple):
        n_ctx_tiles = N_CTX // tm
        x_specs = [pl.BlockSpec((tm, D), lambda i: (jnp.minimum(proj(i), n_ctx_tiles - 1), 0)),
                   pl.BlockSpec((tm, D), lambda i: (jnp.maximum(proj(i) - n_ctx_tiles, 0), 0))]
        x_args = list(x)
    else:
        x_specs = [pl.BlockSpec((tm, D), lambda i: (proj(i), 0))]
        x_args = [x]

    def mod_of_proj(which):
        return pl.BlockSpec((None, None, None, 1, D),
                            lambda i: (layer, _cond_of_row(proj(i) * tm), which, 0, 0))

    return pl.pallas_call(
        functools.partial(_out_route_kernel, n_x=len(x_args), w_idx=w_idx),
        grid=(N_TOK_TILES + 1,),
        in_specs=[
            pl.BlockSpec((tm, k), lambda i: (proj(i), 0)),
            pl.BlockSpec(memory_space=pl.ANY),
            mod_of_proj(2),
            pl.BlockSpec((1, D), lambda i: (0, 0)),
            mod_of_proj(4),
            mod_of_proj(3),
            pl.BlockSpec((2 * D, ROUTER_LANES), lambda i: (0, 0)),
            pl.BlockSpec((1, ROUTER_LANES), lambda i: (0, 0)),
        ] + x_specs,
        out_specs=[
            pl.BlockSpec((tm, D), lambda i: (proj(i), 0)),
            pl.BlockSpec(memory_space=pl.ANY),
            pl.BlockSpec(memory_space=pl.ANY),
            pl.BlockSpec((tm, ROUTER_LANES), lambda i: (routed(i), 0)),
            pl.BlockSpec((None, 1, ROUTER_LANES), lambda i: (routed(i), 0, 0)),
        ],
        out_shape=[
            jax.ShapeDtypeStruct((N_TOK, D), F32),
            jax.ShapeDtypeStruct((ROW_CAP, D), BF16),
            jax.ShapeDtypeStruct((ROW_CAP, ROUTER_LANES), F32),
            jax.ShapeDtypeStruct((N_TOK, ROUTER_LANES), F32),
            jax.ShapeDtypeStruct((N_TOK_TILES, 1, ROUTER_LANES), F32),
        ],
        scratch_shapes=[
            pltpu.VMEM((k, D), BF16),
            pltpu.VMEM((2, k, W_STAGE_COLS), F32),
            pltpu.VMEM((tm, 2 * D), BF16),
            pltpu.VMEM((2, TILE_ROWS, D), BF16),
            pltpu.VMEM((2, TILE_ROWS, ROUTER_LANES), F32),
            pltpu.VMEM((ZERO_ROWS, D), BF16),
            pltpu.VMEM((ZERO_ROWS, ROUTER_LANES), F32),
            pltpu.SMEM((N_GROUPS,), jnp.int32),
            pltpu.SMEM((N_GROUPS,), jnp.int32),
            pltpu.SemaphoreType.DMA((2,)),
            pltpu.SemaphoreType.DMA((2, 2, N_GROUPS, len(RUN_BITS))),
            pltpu.SemaphoreType.DMA((2, N_GROUPS)),
        ],
        compiler_params=_params("arbitrary"),
        name="out_proj_route",
    )(a_bf16, w, mod, norm_g.reshape(1, D), mod, mod, rw, rb, *x_args)


def _router_weights(rgw, rgb, rew, reb):
    n_out = N_GROUPS + N_EXPERTS
    rw = jnp.concatenate([rgw, rew.reshape(D, N_EXPERTS)], axis=1)
    hi = rw.astype(BF16)
    lo = (rw - hi.astype(F32)).astype(BF16)
    zeros = lambda n: jnp.zeros((D, n), BF16)
    top = jnp.concatenate([hi, zeros(LOW_LANE0 - n_out), lo, zeros(ROUTER_LANES - LOW_LANE0 - n_out)], axis=1)
    bottom = jnp.concatenate([hi, zeros(ROUTER_LANES - n_out)], axis=1)
    rb = jnp.concatenate([rgb, reb.reshape(N_EXPERTS)])
    rb = jnp.pad(rb, (0, ROUTER_LANES - n_out)).reshape(1, ROUTER_LANES)
    return jnp.concatenate([top, bottom], axis=0), rb


def _moe_plan(cnt):
    i32 = jnp.int32
    padded = (cnt + (ROW_ALIGN - 1)) // ROW_ALIGN * ROW_ALIGN
    run0 = jnp.cumsum(padded, axis=1) - padded
    length = jnp.sum(padded, axis=0)
    start = jnp.arange(N_GROUPS, dtype=i32) * GROUP_CAP
    off = start[None, :] + jnp.cumsum(padded, axis=0) - padded
    need = length
    ntile = (need + MOE_TILE - 1) // MOE_TILE
    cum = jnp.cumsum(ntile)
    total = cum[-1]
    k = jnp.arange(MOE_TILES_MAX, dtype=i32)
    kk = jnp.minimum(k, jnp.maximum(total - 1, 0))
    grp = jnp.minimum(jnp.sum((kk[:, None] >= cum[None, :]).astype(i32), axis=1), N_GROUPS - 1)
    j = kk - (cum - ntile)[grp]
    blk = start[grp] // MOE_TILE + j
    nsub = jnp.clip((need[grp] - j * MOE_TILE + MOE_SUB - 1) // MOE_SUB, 0, MOE_TILE // MOE_SUB)
    nsub = jnp.where(k < total, nsub, 0)
    runs = tuple(a.reshape(-1).astype(i32) for a in (off, run0, padded))
    return runs, (blk.astype(i32), grp.astype(i32), nsub.astype(i32))


def _run_pieces(run_refs, tile, g):
    off_ref, run0_ref, padded_ref = run_refs
    idx = tile * N_GROUPS + g
    n = padded_ref[idx]
    src0 = run0_ref[idx]
    dst0 = off_ref[idx]
    pieces = []
    for b, bit in enumerate(RUN_BITS):
        done = n & (-2 * bit)
        pieces.append((b, (n & bit) != 0, pl.multiple_of(src0 + done, ROW_ALIGN),
                       pl.multiple_of(dst0 + done, ROW_ALIGN), bit))
    return pieces


def _moe_kernel(blk_ref, grp_ref, nsub_ref, h_ref, c_ref, w1_ref, w3_ref, w2_ref, o_ref, hid_ref):
    k = pl.program_id(0)
    c = pl.program_id(1)
    ns = nsub_ref[k]
    n_sub_max = MOE_TILE // MOE_SUB

    def for_row_count(fn):
        for m in range(1, n_sub_max + 1):
            @pl.when(ns == m)
            def _():
                fn(m * MOE_SUB)

    @pl.when(jnp.logical_and(ns > 0, c < N_FF_CHUNKS))
    def _():
        w1 = w1_ref[...].astype(BF16)
        w3 = w3_ref[...].astype(BF16)
        e_lane = EXPERT_LANE0 + grp_ref[k] * EXP_PER_GROUP + c * FF_CHUNK // D_EXPERT

        def up(rows):
            h = h_ref[0:rows, :]
            a = jnp.dot(h, w1, preferred_element_type=F32)
            b = jnp.dot(h, w3, preferred_element_type=F32)
            comb = c_ref[0:rows, :]
            lane = lax.broadcasted_iota(jnp.int32, comb.shape, 1)
            cw = jnp.sum(jnp.where(lane == e_lane, comb, 0.0), axis=-1, keepdims=True)
            hid_ref[c, 0:rows, :] = (_silu(a) * b * cw).astype(BF16)

        for_row_count(up)

    @pl.when(jnp.logical_and(ns > 0, c >= N_FF_CHUNKS))
    def _():
        w2 = w2_ref[...].reshape(EXP_PER_GROUP * D_EXPERT, OUT_CHUNK).astype(BF16)

        def down(rows):
            hid = jnp.concatenate([hid_ref[j, 0:rows, :] for j in range(N_FF_CHUNKS)], axis=1)
            o_ref[0:rows, :] = jnp.dot(hid, w2, preferred_element_type=F32).astype(BF16)
            if rows < MOE_TILE:
                o_ref[rows:, :] = jnp.zeros((MOE_TILE - rows, OUT_CHUNK), BF16)

        for_row_count(down)


def _moe_experts(hs, cs, w1, w3, w2, layer, plan):
    per_e = D_EXPERT // FF_CHUNK
    n_steps = N_FF_CHUNKS + N_OUT_CHUNKS

    def step_of(k, c, nsub_ref):
        return jnp.where(nsub_ref[k] > 0, c, n_steps - 1)

    def w13_map(k, c, blk_ref, grp_ref, nsub_ref):
        cc = jnp.minimum(step_of(k, c, nsub_ref), N_FF_CHUNKS - 1)
        return (layer, grp_ref[k] * EXP_PER_GROUP + cc // per_e, 0, cc % per_e)

    def out_chunk(k, c, nsub_ref):
        return jnp.maximum(step_of(k, c, nsub_ref) - N_FF_CHUNKS, 0)

    w2_map = lambda k, c, blk_ref, grp_ref, nsub_ref: (layer, grp_ref[k], 0, out_chunk(k, c, nsub_ref))
    row_map = lambda k, c, blk_ref, grp_ref, nsub_ref: (blk_ref[k], 0)
    out_map = lambda k, c, blk_ref, grp_ref, nsub_ref: (blk_ref[k], out_chunk(k, c, nsub_ref))
    grid_spec = pltpu.PrefetchScalarGridSpec(
        num_scalar_prefetch=3,
        grid=(MOE_TILES_MAX, n_steps),
        in_specs=[
            pl.BlockSpec((MOE_TILE, D), row_map),
            pl.BlockSpec((MOE_TILE, ROUTER_LANES), row_map),
            pl.BlockSpec((None, None, D, FF_CHUNK), w13_map),
            pl.BlockSpec((None, None, D, FF_CHUNK), w13_map),
            pl.BlockSpec((None, EXP_PER_GROUP, D_EXPERT, OUT_CHUNK), w2_map),
        ],
        out_specs=pl.BlockSpec((MOE_TILE, OUT_CHUNK), out_map),
        scratch_shapes=[pltpu.VMEM((N_FF_CHUNKS, MOE_TILE, FF_CHUNK), BF16)],
    )
    return pl.pallas_call(
        _moe_kernel,
        grid_spec=grid_spec,
        out_shape=jax.ShapeDtypeStruct((ROW_CAP, D), BF16),
        compiler_params=_params("arbitrary", "arbitrary"),
        name="moe_experts",
    )(*plan, hs, cs, w1, w3, w2)


def _combine_kernel(off_ref, run0_ref, padded_ref, x_ref, gate_ref, comb_ref, ys_ref, *refs,
                    split_out, emit_next):
    if emit_next:
        (ng_ref, nsc_ref, nsh_ref), refs = refs[:3], refs[3:]
    out_refs, (yw_ref, sem) = refs[:-2], refs[-2:]
    run_refs = (off_ref, run0_ref, padded_ref)
    i = pl.program_id(0)
    n_steps = pl.num_programs(0)
    slot = i % 2

    def for_each_copy(tile, sl, fn):
        for g in range(N_GROUPS):
            for b, pred, src, dst, rows in _run_pieces(run_refs, tile, g):
                @pl.when(pred)
                def _():
                    fn(pltpu.make_async_copy(ys_ref.at[pl.ds(dst, rows), :],
                                             yw_ref.at[sl, pl.ds(src, rows), :], sem.at[sl, g, b]))

    def fetch(tile, sl):
        yw_ref[sl, TOK_TILE:, :] = jnp.zeros((TILE_ROWS - TOK_TILE, D), BF16)
        for_each_copy(tile, sl, lambda cp: cp.start())

    @pl.when(i == 0)
    def _():
        fetch(i, slot)

    @pl.when(i + 1 < n_steps)
    def _():
        fetch(i + 1, 1 - slot)

    for_each_copy(i, slot, lambda cp: cp.wait())

    pos = comb_ref[:, POS_LANE:POS_LANE + 1].astype(jnp.int32)
    wcol = lax.broadcasted_iota(jnp.int32, (TOK_TILE, TILE_ROWS), 1)
    perm = jnp.where(wcol == pos, 1.0, 0.0).astype(BF16)
    y = jnp.dot(perm, yw_ref[slot], preferred_element_type=F32)
    out = x_ref[...] + gate_ref[...] * y
    if emit_next:
        out_refs[1][...] = _modnorm(out, ng_ref[...], nsc_ref[...], nsh_ref[...]).astype(BF16)
    if not split_out:
        out_refs[0][...] = out
    else:
        @pl.when(i < N_CTX // TOK_TILE)
        def _():
            out_refs[0][...] = out

        @pl.when(i >= N_CTX // TOK_TILE)
        def _():
            out_refs[1][...] = out


def _combine(x, mod, layer, comb, ys, runs, next_norm_g):
    n_ctx_tiles = N_CTX // TOK_TILE
    split_out = next_norm_g is None
    tile_spec = pl.BlockSpec((TOK_TILE, D), lambda i, *_: (i, 0))
    in_specs = [
        tile_spec,
        _mod_spec(layer, 5, TOK_TILE, row_axis=0),
        pl.BlockSpec((TOK_TILE, ROUTER_LANES), lambda i, *_: (i, 0)),
        pl.BlockSpec(memory_space=pl.ANY),
    ]
    args = [x, mod, comb, ys]
    if split_out:
        out_specs = [pl.BlockSpec((TOK_TILE, D), lambda i, *_: (jnp.minimum(i, n_ctx_tiles - 1), 0)),
                     pl.BlockSpec((TOK_TILE, D), lambda i, *_: (jnp.maximum(i - n_ctx_tiles, 0), 0))]
        out_shape = [jax.ShapeDtypeStruct((N_CTX, D), F32), jax.ShapeDtypeStruct((N_LAT, D), F32)]
    else:
        in_specs += [pl.BlockSpec((1, D), lambda i, *_: (0, 0)),
                     _mod_spec(layer + 1, 1, TOK_TILE, row_axis=0),
                     _mod_spec(layer + 1, 0, TOK_TILE, row_axis=0)]
        args += [next_norm_g.reshape(1, D), mod, mod]
        out_specs = [tile_spec, tile_spec]
        out_shape = [jax.ShapeDtypeStruct((N_TOK, D), F32), jax.ShapeDtypeStruct((N_TOK, D), BF16)]
    grid_spec = pltpu.PrefetchScalarGridSpec(
        num_scalar_prefetch=3,
        grid=(N_TOK_TILES,),
        in_specs=in_specs,
        out_specs=out_specs,
        scratch_shapes=[
            pltpu.VMEM((2, TILE_ROWS, D), BF16),
            pltpu.SemaphoreType.DMA((2, N_GROUPS, len(RUN_BITS))),
        ],
    )
    return pl.pallas_call(
        functools.partial(_combine_kernel, split_out=split_out, emit_next=not split_out),
        grid_spec=grid_spec,
        out_shape=out_shape,
        compiler_params=_params("arbitrary"),
        name="moe_combine",
    )(*runs, *args)


def _moe_experts_and_combine(x, routed, mod, layer, w1, w3, w2, next_norm_g):
    hs, cs, comb, cnt = routed
    runs, plan = _moe_plan(cnt[:, 0, :N_GROUPS].astype(jnp.int32))
    ys = _moe_experts(hs, cs, w1, w3, w2, layer, plan)
    return _combine(x, mod, layer, comb, ys, runs, next_norm_g)


def kernel(x_prompt, x_sample, cache_k, cache_v, c, c_ctx, ada_w, ada_b, norm_mix_g, norm_ffn_g,
           mix_in_w, pool_w, pool_scale, conv_w, conv_b, conv_ln_g, conv_ln_b, mix_out_w,
           qkv_w, q_norm_g, k_norm_g, rpb, attn_out_w, router_g_w, router_g_b, router_e_w,
           router_e_b, exp_w1, exp_w3, exp_w2):
    x = (x_prompt.reshape(N_CTX, D), x_sample.reshape(N_LAT, D))
    cvec8 = jnp.concatenate([c_ctx[None, :], c, jnp.zeros((8 - 1 - N_LAT_SEQ, D), F32)], axis=0)
    mod = _mod_table(_adaln_first_layer(cvec8, ada_w, ada_b))
    n_attn = DEPTH // 2
    cache_k2 = cache_k.reshape(N_LAT_SEQ, n_attn, CTX_SEQ, D)
    cache_v2 = cache_v.reshape(N_LAT_SEQ, n_attn, CTX_SEQ, D)

    caches = None
    h = None
    for l in range(DEPTH):
        e = l // 2
        if l % 2 == 0:
            u = _mm_prologue(x, mod, l, norm_mix_g[l], mix_in_w, e, h=h)
            mixed, *later_mod = _seq_mixer(u, pool_w[e].astype(BF16), pool_scale[e], conv_w[e], conv_b[e],
                                           conv_ln_g[e], conv_ln_b[e],
                                           ada=(cvec8, ada_w, ada_b) if l == 0 else None)
            if later_mod:
                mod = jnp.concatenate([mod, _mod_table(later_mod[0])], axis=0)
            out_w = mix_out_w
        else:
            q, k, v, new_k, new_v = _qkv_proj(h, e, qkv_w, q_norm_g[e], k_norm_g[e], caches)
            caches = (new_k, new_v)
            mixed = _ctx_attention(q, k, v)
            mixed = _na_attention(q, k, v, cache_k2[:, e], cache_v2[:, e], rpb[e], mixed)
            out_w = attn_out_w
        rw, rb = _router_weights(router_g_w[l], router_g_b[l], router_e_w[l], router_e_b[l])
        x, *routed = _out_proj_and_route(mixed, out_w, e, x, mod, l, norm_ffn_g[l], rw, rb)
        if l < DEPTH - 1:
            x, h = _moe_experts_and_combine(x, routed, mod, l, exp_w1, exp_w3, exp_w2,
                                            next_norm_g=norm_mix_g[l + 1])
        else:
            y_ctx, y_lat = _moe_experts_and_combine(x, routed, mod, l, exp_w1, exp_w3, exp_w2,
                                                    next_norm_g=None)

    y_prompt = y_ctx.reshape(N_CTX_SEQ, CTX_SEQ, D)
    y_sample = y_lat.reshape(N_LAT_SEQ, LAT_SEQ, D)
    cache_dims = (N_CTX_SEQ, n_attn, CTX_SEQ, N_HEADS, HEAD_DIM)
    return (y_prompt, y_sample, caches[0].reshape(cache_dims), caches[1].reshape(cache_dims))
```

```python
import functools

import numpy as np
import jax
import jax.numpy as jnp
from jax import lax
from jax.experimental import pallas as pl
from jax.experimental.pallas import tpu as pltpu

D = 2048
N_CTX_SEQ = 16
CTX_SEQ = 256
N_LAT_SEQ = 2
LAT_SEQ = 1024
N_CTX = N_CTX_SEQ * CTX_SEQ
N_LAT = N_LAT_SEQ * LAT_SEQ
N_TOK = N_CTX + N_LAT
DEPTH = 4
GRID_W = 64
GRID_H = LAT_SEQ // GRID_W
N_HEADS = 16
HEAD_DIM = 128
ATTN_SCALE = HEAD_DIM ** -0.5
D_POOL = 1024
D_CONV = 1024
POOL_WINDOWS = (2, 4, 8, 16)
POOL_GC = 256
CONV_W = 31
NA_KH = 8
NA_KW = 16
N_GROUPS = 4
EXP_PER_GROUP = 4
N_EXPERTS = 16
D_EXPERT = 512
EPS = 1e-6
NEG = -1e30

LANES = 128
SUBLANES = 8
SEQ_TILE = 256
HALO = 16
ROUTER_LANES = 128
EXPERT_LANE0 = N_GROUPS
POS_LANE = EXPERT_LANE0 + N_EXPERTS
LOW_LANE0 = 32
VMEM_LIMIT = 56 * 1024 * 1024

TOK_TILE = 256
N_TOK_TILES = N_TOK // TOK_TILE
ROW_ALIGN = 16
RUN_BITS = (256, 128, 64, 32, 16)
TILE_ROWS = TOK_TILE + N_GROUPS * ROW_ALIGN
MOE_TILE = 1024
MOE_SUB = 128
FF_CHUNK = 512
OUT_CHUNK = 512
N_FF_CHUNKS = EXP_PER_GROUP * D_EXPERT // FF_CHUNK
N_OUT_CHUNKS = D // OUT_CHUNK
ZERO_ROWS = MOE_SUB
MAX_PADDED = N_TOK + N_TOK_TILES * N_GROUPS * (ROW_ALIGN - 1)
MOE_TILES_MAX = (MAX_PADDED + N_GROUPS * (MOE_TILE - 1)) // MOE_TILE
GROUP_CAP = -(-(N_TOK + N_TOK_TILES * (ROW_ALIGN - 1) + ZERO_ROWS) // MOE_TILE) * MOE_TILE
ROW_CAP = N_GROUPS * GROUP_CAP

F32 = jnp.float32
BF16 = jnp.bfloat16


def _cond_of_row(row):
    row = jnp.minimum(row, N_TOK - 1)
    return jnp.where(row < N_CTX, 0, 1 + (row - N_CTX) // LAT_SEQ)


def _mod_spec(layer, which, tm, tn=D, row_axis=1, col_axis=None):
    def index_map(*ids):
        cond = _cond_of_row(ids[row_axis] * tm)
        col = 0 if col_axis is None else ids[col_axis]
        return (layer, cond, which, 0, col)
    return pl.BlockSpec((None, None, None, 1, tn), index_map)


def _silu(x):
    return x / (1.0 + jnp.exp(-x))


def _modnorm(x, g, sc, sh):
    ms = jnp.mean(x * x, axis=-1, keepdims=True)
    y = x * lax.rsqrt(ms + EPS) * g
    return y * (1.0 + sc) + sh


def _params(*sem):
    return pltpu.CompilerParams(dimension_semantics=sem, vmem_limit_bytes=VMEM_LIMIT)


def _ada_kernel(c_ref, w_ref, b_ref, o_ref):
    s = _silu(c_ref[...]).astype(BF16)
    o_ref[...] = jnp.dot(s, w_ref[...].astype(BF16), preferred_element_type=F32) + b_ref[...]


N_MOD = 6 * D


def _mod_table(raw):
    return raw[:, :3].reshape(raw.shape[0], 3, 6, 1, D)


def _adaln_first_layer(cvec8, ada_w, ada_b):
    tn = 1024
    return pl.pallas_call(
        _ada_kernel,
        grid=(N_MOD // tn,),
        in_specs=[
            pl.BlockSpec((8, D), lambda j: (0, 0)),
            pl.BlockSpec((None, D, tn), lambda j: (0, 0, j)),
            pl.BlockSpec((None, 1, tn), lambda j: (0, 0, j)),
        ],
        out_specs=pl.BlockSpec((None, 8, tn), lambda j: (0, 0, j)),
        out_shape=jax.ShapeDtypeStruct((1, 8, N_MOD), F32),
        compiler_params=_params("arbitrary"),
        name="adaln",
    )(cvec8, ada_w, ada_b.reshape(DEPTH, 1, N_MOD))


def _token_specs(x, tm, tn, idx):
    if not isinstance(x, tuple):
        return [pl.BlockSpec((tm, tn), lambda j, i: idx(i, j))], [x]
    n_ctx_tiles = N_CTX // tm
    ctx_map = lambda j, i: idx(jnp.minimum(i, n_ctx_tiles - 1), j)
    lat_map = lambda j, i: idx(jnp.maximum(i - n_ctx_tiles, 0), j)
    return [pl.BlockSpec((tm, tn), ctx_map), pl.BlockSpec((tm, tn), lat_map)], list(x)


def _token_rows(refs, i, tm):
    if len(refs) == 1:
        return refs[0][...]
    return jnp.where(i < N_CTX // tm, refs[0][...], refs[1][...])


def _cast_weights_once(i, pairs):
    @pl.when(i == 0)
    def _():
        for w_ref, wb_ref in pairs:
            wb_ref[...] = w_ref[...].astype(BF16)


def _mm_pro_kernel(*refs, n_x, tm):
    x_refs = refs[:n_x]
    g_ref, sc_ref, sh_ref, w_ref, o_ref, wb_ref = refs[n_x:]
    i = pl.program_id(1)
    _cast_weights_once(i, [(w_ref, wb_ref)])
    h = _modnorm(_token_rows(x_refs, i, tm), g_ref[...], sc_ref[...], sh_ref[...]).astype(BF16)
    o_ref[...] = jnp.dot(h, wb_ref[...], preferred_element_type=F32)


def _mm_plain_kernel(h_ref, w_ref, o_ref, wb_ref):
    _cast_weights_once(pl.program_id(1), [(w_ref, wb_ref)])
    o_ref[...] = jnp.dot(h_ref[...], wb_ref[...], preferred_element_type=F32)


def _mm_prologue(x, mod, layer, norm_g, w, w_idx, h=None, tm=512, tn=1024):
    f = w.shape[2]
    if h is not None:
        tm = 2 * tm
    w_spec = pl.BlockSpec((None, D, tn), lambda j, i: (w_idx, 0, j))
    if h is None:
        x_specs, x_args = _token_specs(x, tm, D, lambda i, j: (i, 0))
        body = functools.partial(_mm_pro_kernel, n_x=len(x_args), tm=tm)
        in_specs = x_specs + [pl.BlockSpec((1, D), lambda j, i: (0, 0)),
                              _mod_spec(layer, 1, tm), _mod_spec(layer, 0, tm), w_spec]
        args = x_args + [norm_g.reshape(1, D), mod, mod, w]
    else:
        body = _mm_plain_kernel
        in_specs = [pl.BlockSpec((tm, D), lambda j, i: (i, 0)), w_spec]
        args = [h, w]
    return pl.pallas_call(
        body,
        grid=(f // tn, N_TOK // tm),
        in_specs=in_specs,
        out_specs=pl.BlockSpec((tm, tn), lambda j, i: (i, j)),
        out_shape=jax.ShapeDtypeStruct((N_TOK, f), F32),
        scratch_shapes=[pltpu.VMEM((D, tn), BF16)],
        compiler_params=_params("arbitrary", "arbitrary"),
        name="mm_prologue",
    )(*args)


def _qkv_kernel(h_ref, wq_ref, wk_ref, wv_ref, qg_ref, kg_ref, *rest, tn, n_ctx_tiles, seqs_per_tile):
    q_ref, k_ref, v_ref, ck_ref, cv_ref, wqb_ref, wkb_ref, wvb_ref, kn_ref, vf_ref = rest[-10:]
    i = pl.program_id(1)
    _cast_weights_once(i, [(wq_ref, wqb_ref), (wk_ref, wkb_ref), (wv_ref, wvb_ref)])
    h = h_ref[...]
    q = jnp.dot(h, wqb_ref[...], preferred_element_type=F32)
    k = jnp.dot(h, wkb_ref[...], preferred_element_type=F32)
    v = jnp.dot(h, wvb_ref[...], preferred_element_type=F32)
    for hh in range(tn // HEAD_DIM):
        ls = slice(hh * HEAD_DIM, (hh + 1) * HEAD_DIM)
        qh = q[:, ls]
        kh = k[:, ls]
        qn = qh * lax.rsqrt(jnp.mean(qh * qh, axis=-1, keepdims=True) + EPS) * qg_ref[:, ls]
        kn = kh * lax.rsqrt(jnp.mean(kh * kh, axis=-1, keepdims=True) + EPS) * kg_ref[:, ls]
        q_ref[:, ls] = qn.astype(BF16)
        k_ref[:, ls] = kn.astype(BF16)
        kn_ref[:, ls] = kn
    v_ref[...] = v.astype(BF16)
    vf_ref[...] = v

    @pl.when(i < n_ctx_tiles)
    def _():
        ck_ref[...] = kn_ref[...].reshape(seqs_per_tile, CTX_SEQ, tn)
        cv_ref[...] = vf_ref[...].reshape(seqs_per_tile, CTX_SEQ, tn)


def _qkv_proj(h, w_idx, w, q_gain, k_gain, caches, tm=512, tn=512):
    n = h.shape[0]
    n_attn = DEPTH // 2
    ncol = D // tn
    seqs_per_tile = tm // CTX_SEQ
    n_ctx_tiles = N_CTX // tm
    cache_shape = jax.ShapeDtypeStruct((N_CTX_SEQ, n_attn, CTX_SEQ, D), F32)
    cache_spec = pl.BlockSpec(
        (seqs_per_tile, None, CTX_SEQ, tn),
        lambda j, i: (jnp.minimum(i, n_ctx_tiles - 1), w_idx, 0, j))
    in_specs = [
        pl.BlockSpec((tm, D), lambda j, i: (i, 0)),
        pl.BlockSpec((None, D, tn), lambda j, i: (w_idx, 0, j)),
        pl.BlockSpec((None, D, tn), lambda j, i: (w_idx, 0, ncol + j)),
        pl.BlockSpec((None, D, tn), lambda j, i: (w_idx, 0, 2 * ncol + j)),
        pl.BlockSpec((1, tn), lambda j, i: (0, j)),
        pl.BlockSpec((1, tn), lambda j, i: (0, j)),
    ]
    args = [h, w, w, w,
            jnp.tile(q_gain, N_HEADS).reshape(1, D), jnp.tile(k_gain, N_HEADS).reshape(1, D)]
    aliases = {}
    if caches is not None:
        in_specs += [pl.BlockSpec(memory_space=pl.ANY), pl.BlockSpec(memory_space=pl.ANY)]
        aliases = {len(args): 3, len(args) + 1: 4}
        args += list(caches)
    act_spec = pl.BlockSpec((tm, tn), lambda j, i: (i, j))
    act_shape = jax.ShapeDtypeStruct((n, D), BF16)
    return pl.pallas_call(
        functools.partial(_qkv_kernel, tn=tn, n_ctx_tiles=n_ctx_tiles, seqs_per_tile=seqs_per_tile),
        grid=(ncol, n // tm),
        in_specs=in_specs,
        out_specs=[act_spec, act_spec, act_spec, cache_spec, cache_spec],
        out_shape=[act_shape, act_shape, act_shape, cache_shape, cache_shape],
        input_output_aliases=aliases,
        scratch_shapes=[pltpu.VMEM((D, tn), BF16)] * 3 + [pltpu.VMEM((tm, tn), F32)] * 2,
        compiler_params=_params("arbitrary", "arbitrary"),
        name="qkv_proj",
    )(*args)


def _seq_mix_tile(i, u_ref, top_ref, bot_ref, pw_ref, ps_ref, cw_ref, cb_ref, lg_ref, lb_ref,
                  o_ref, zpad_ref, conv_ref):
    n_ctx_tiles = N_CTX // SEQ_TILE
    tiles_per_lat = LAT_SEQ // SEQ_TILE
    is_lat = i >= n_ctx_tiles
    chunk = jnp.where(is_lat, (i - n_ctx_tiles) % tiles_per_lat, 0)
    top_ok = jnp.logical_and(is_lat, chunk > 0)
    bot_ok = jnp.logical_and(is_lat, chunk < tiles_per_lat - 1)
    seq_len = jnp.where(is_lat, LAT_SEQ, CTX_SEQ)
    t = chunk * SEQ_TILE + lax.broadcasted_iota(jnp.int32, (SEQ_TILE, 1), 0)
    rows = SEQ_TILE + 2 * HALO

    for g, win in enumerate(POOL_WINDOWS):
        ls = slice(g * POOL_GC, (g + 1) * POOL_GC)
        mid = u_ref[:, ls]
        top = jnp.where(top_ok, top_ref[:, ls], 0.0)
        bot = jnp.where(bot_ok, bot_ref[:, ls], 0.0)
        up = jnp.concatenate([top, mid, bot], axis=0)
        s = pltpu.roll(up, 1, 0) + up
        if win >= 4:
            s = pltpu.roll(s, 1, 0) + pltpu.roll(s, rows - 1, 0)
        if win >= 8:
            s = pltpu.roll(s, 2, 0) + pltpu.roll(s, rows - 2, 0)
        if win >= 16:
            s = pltpu.roll(s, 4, 0) + pltpu.roll(s, rows - 4, 0)
        s = s[HALO:HALO + SEQ_TILE]
        lo = jnp.maximum(t - win // 2, 0)
        hi = jnp.minimum(t + (win - win // 2) - 1, seq_len - 1)
        cnt = (hi - lo + 1).astype(F32)
        p = (s / cnt - mid).astype(BF16)
        y = jnp.dot(p, pw_ref[g], preferred_element_type=F32) * ps_ref[:, ls]
        o_ref[:, ls] = y.astype(BF16)

    def glu(ref):
        return ref[:, D_POOL:D_POOL + D_CONV] * (1.0 / (1.0 + jnp.exp(-ref[:, D_POOL + D_CONV:])))

    zpad_ref[0:HALO, :] = jnp.where(top_ok, glu(top_ref), 0.0)
    zpad_ref[HALO:HALO + SEQ_TILE, :] = glu(u_ref)
    zpad_ref[HALO + SEQ_TILE:rows, :] = jnp.where(bot_ok, glu(bot_ref), 0.0)
    for lt in range(D_CONV // LANES):
        ls = slice(lt * LANES, (lt + 1) * LANES)
        zp = zpad_ref[:, ls]
        acc = jnp.zeros((SEQ_TILE, LANES), F32)
        for b in range(SUBLANES):
            sb = zp if b == 0 else pltpu.roll(zp, rows - b, 0)
            for a in range(-(-(CONV_W + 1) // SUBLANES)):
                off = SUBLANES * a + b
                if 1 <= off <= CONV_W:
                    acc = acc + cw_ref[off - 1:off, ls] * sb[SUBLANES * a:SUBLANES * a + SEQ_TILE]
        conv_ref[:, ls] = acc + cb_ref[:, ls]
    zc = conv_ref[...]
    mu = jnp.mean(zc, axis=-1, keepdims=True)
    d = zc - mu
    var = jnp.mean(d * d, axis=-1, keepdims=True)
    zn = d * lax.rsqrt(var + EPS) * lg_ref[...] + lb_ref[...]
    o_ref[:, D_POOL:] = _silu(zn).astype(BF16)


N_SEQ_INPUTS = 9


def _seq_kernel(*refs, with_ada):
    seq_in, refs = refs[:N_SEQ_INPUTS], refs[N_SEQ_INPUTS:]
    if with_ada:
        (c_ref, w_ref, b_ref), refs = refs[:3], refs[3:]
        o_ref, mod_ref, zpad_ref, conv_ref = refs
        _ada_kernel(c_ref, w_ref, b_ref, mod_ref)
    else:
        o_ref, zpad_ref, conv_ref = refs
    _seq_mix_tile(pl.program_id(0), *seq_in, o_ref, zpad_ref, conv_ref)


def _seq_mixer(u, pool_w_bf16, pool_scale, conv_w, conv_b, ln_g, ln_b, ada=None):
    n = u.shape[0]
    f_in = u.shape[1]
    n_steps = n // SEQ_TILE
    hb = SEQ_TILE // HALO
    n_hblocks = n // HALO
    const = lambda *shape: pl.BlockSpec(shape, lambda i: (0,) * len(shape))
    in_specs = [
        pl.BlockSpec((SEQ_TILE, f_in), lambda i: (i, 0)),
        pl.BlockSpec((HALO, f_in), lambda i: (jnp.maximum(i * hb - 1, 0), 0)),
        pl.BlockSpec((HALO, f_in), lambda i: (jnp.minimum((i + 1) * hb, n_hblocks - 1), 0)),
        const(len(POOL_WINDOWS), POOL_GC, POOL_GC),
        const(1, D_POOL), const(CONV_W, D_CONV), const(1, D_CONV), const(1, D_CONV), const(1, D_CONV),
    ]
    args = [u, u, u, pool_w_bf16, pool_scale.reshape(1, D_POOL), conv_w, conv_b.reshape(1, D_CONV),
            ln_g.reshape(1, D_CONV), ln_b.reshape(1, D_CONV)]
    out_specs = [pl.BlockSpec((SEQ_TILE, D_POOL + D_CONV), lambda i: (i, 0))]
    out_shape = [jax.ShapeDtypeStruct((n, D_POOL + D_CONV), BF16)]
    if ada is not None:
        cvec8, ada_w, ada_b = ada
        later = DEPTH - 1
        chunk = later * N_MOD // n_steps
        per_layer = N_MOD // chunk
        chunk_map = lambda i: (1 + i // per_layer, 0, i % per_layer)
        in_specs += [const(8, D),
                     pl.BlockSpec((None, D, chunk), chunk_map),
                     pl.BlockSpec((None, 1, chunk), chunk_map)]
        args += [cvec8, ada_w, ada_b.reshape(DEPTH, 1, N_MOD)]
        out_specs.append(pl.BlockSpec((None, 8, chunk), lambda i: (i // per_layer, 0, i % per_layer)))
        out_shape.append(jax.ShapeDtypeStruct((later, 8, N_MOD), F32))
    return pl.pallas_call(
        functools.partial(_seq_kernel, with_ada=ada is not None),
        grid=(n_steps,),
        in_specs=in_specs,
        out_specs=out_specs,
        out_shape=out_shape,
        scratch_shapes=[
            pltpu.VMEM((SEQ_TILE + 2 * HALO, D_CONV), F32),
            pltpu.VMEM((SEQ_TILE, D_CONV), F32),
        ],
        compiler_params=_params("arbitrary"),
        name="seq_mixer",
    )(*args)


_NT = (((1,), (1,)), ((), ()))


def _ctx_attn_kernel(q_ref, k_ref, v_ref, o_ref, s_ref, p_ref):
    for h in range(N_HEADS):
        ls = slice(h * HEAD_DIM, (h + 1) * HEAD_DIM)
        s_ref[h] = lax.dot_general(q_ref[:, ls], k_ref[:, ls], _NT, preferred_element_type=F32) * ATTN_SCALE
    for h in range(N_HEADS):
        s = s_ref[h]
        e = jnp.exp(s - jnp.max(s, axis=-1, keepdims=True))
        p_ref[h] = (e * (1.0 / jnp.sum(e, axis=-1, keepdims=True))).astype(BF16)
    for h in range(N_HEADS):
        ls = slice(h * HEAD_DIM, (h + 1) * HEAD_DIM)
        o_ref[:, ls] = jnp.dot(p_ref[h], v_ref[:, ls], preferred_element_type=F32).astype(BF16)


def _ctx_attention(q, k, v):
    n = q.shape[0]
    spec = pl.BlockSpec((CTX_SEQ, D), lambda b: (b, 0))
    return pl.pallas_call(
        _ctx_attn_kernel,
        grid=(N_CTX_SEQ,),
        in_specs=[spec, spec, spec],
        out_specs=spec,
        out_shape=jax.ShapeDtypeStruct((n, D), BF16),
        scratch_shapes=[pltpu.VMEM((N_HEADS, CTX_SEQ, CTX_SEQ), F32),
                        pltpu.VMEM((N_HEADS, CTX_SEQ, CTX_SEQ), BF16)],
        compiler_params=_params("arbitrary"),
        name="ctx_attention",
    )(q, k, v)


def _na_window_mask():
    cq = np.arange(GRID_W)[:, None]
    kc = np.arange(GRID_W)[None, :]
    win0 = np.clip(cq - NA_KW // 2, 0, GRID_W - NA_KW)
    mask = ((kc >= win0) & (kc < win0 + NA_KW)).astype(np.float32)
    return np.tile(mask, (1, NA_KH))


def _na_bias_rows(rpb_e):
    centre = NA_KW - 1
    pad = jnp.zeros(rpb_e.shape[:2] + (2 * GRID_W - (2 * NA_KW - 1),), F32)
    return jnp.concatenate([rpb_e[..., centre:], pad, rpb_e[..., :centre]], axis=-1)


def _na_key_row0(r):
    return min(max(r - NA_KH // 2, 0), GRID_H - NA_KH)


def _na_segments():
    segs = []
    for r in range(GRID_H):
        kr0 = _na_key_row0(r)
        if segs and segs[-1][2] == kr0:
            segs[-1][1] = r + 1
        else:
            segs.append([r, r + 1, kr0])
    return segs


def _na_kernel(q_ref, k_ref, v_ref, kc_ref, vc_ref, rows_ref, mask_ref, o_in_ref, o_ref,
               s_ref, p_ref, acc_ref, tz_ref):
    del o_in_ref
    band = NA_KH * GRID_W
    n_ctx = CTX_SEQ
    head_rows = pl.ds(pl.program_id(0), CTX_SEQ, stride=N_HEADS)
    kc = kc_ref[head_rows, :].astype(BF16)
    vc = vc_ref[head_rows, :].astype(BF16)
    @pl.when(pl.program_id(1) == 0)
    def _():
        n_dr = 2 * NA_KH - 1
        toeplitz = [pltpu.roll(jnp.broadcast_to(rows_ref[d:d + 1, :], (GRID_W, 2 * GRID_W)), 0, 1,
                               stride=1, stride_axis=0) for d in range(n_dr)]
        left = lax.broadcasted_iota(jnp.int32, (GRID_W, 2 * GRID_W), 1) < GRID_W
        for d in range(n_dr - 1):
            tz_ref[d] = jnp.where(left, toeplitz[d], pltpu.roll(toeplitz[d + 1], GRID_W, 1))
    for r0, r1, kr0 in _na_segments():
        rs = slice(r0 * GRID_W, r1 * GRID_W)
        kb = k_ref[kr0 * GRID_W:kr0 * GRID_W + band, :]
        s_ref[rs, 0:band] = lax.dot_general(q_ref[rs, :], kb, _NT, preferred_element_type=F32) * ATTN_SCALE
    s_ref[:, band:band + n_ctx] = lax.dot_general(q_ref[...], kc, _NT, preferred_element_type=F32) * ATTN_SCALE
    mask = mask_ref[...] > 0.5
    for r in range(GRID_H):
        rs = slice(r * GRID_W, (r + 1) * GRID_W)
        kr0 = _na_key_row0(r)
        bias = jnp.concatenate(
            [tz_ref[kr0 + j - r + NA_KH - 1] for j in range(0, NA_KH, 2)], axis=-1)
        s_loc = jnp.where(mask, s_ref[rs, 0:band] + bias, NEG)
        s_ctx = s_ref[rs, band:band + n_ctx]
        m = jnp.maximum(jnp.max(s_loc, axis=-1, keepdims=True), jnp.max(s_ctx, axis=-1, keepdims=True))
        e_loc = jnp.exp(s_loc - m)
        e_ctx = jnp.exp(s_ctx - m)
        inv = 1.0 / (jnp.sum(e_loc, axis=-1, keepdims=True) + jnp.sum(e_ctx, axis=-1, keepdims=True))
        p_ref[rs, 0:band] = (e_loc * inv).astype(BF16)
        p_ref[rs, band:band + n_ctx] = (e_ctx * inv).astype(BF16)
    for r0, r1, kr0 in _na_segments():
        rs = slice(r0 * GRID_W, r1 * GRID_W)
        vb = v_ref[kr0 * GRID_W:kr0 * GRID_W + band, :]
        acc_ref[rs, :] = jnp.dot(p_ref[rs, 0:band], vb, preferred_element_type=F32)
    o_ctx = jnp.dot(p_ref[:, band:band + n_ctx], vc, preferred_element_type=F32)
    o_ref[...] = (acc_ref[...] + o_ctx).astype(BF16)


def _na_attention(q, k, v, cache_k, cache_v, attn_idx, rpb_e, o_ctx):
    mask = _na_window_mask()
    lat0 = N_CTX // LAT_SEQ
    n_dr = 2 * NA_KH - 1
    n_keys = NA_KH * GRID_W + CTX_SEQ
    tok_spec = pl.BlockSpec((LAT_SEQ, HEAD_DIM), lambda h, b: (lat0 + b, h))
    cache_spec = pl.BlockSpec((None, None, CTX_SEQ * N_HEADS, HEAD_DIM), lambda h, b: (b, attn_idx, 0, 0))
    return pl.pallas_call(
        _na_kernel,
        grid=(N_HEADS, N_LAT_SEQ),
        in_specs=[
            tok_spec, tok_spec, tok_spec, cache_spec, cache_spec,
            pl.BlockSpec((None, n_dr, 2 * GRID_W), lambda h, b: (h, 0, 0)),
            pl.BlockSpec((GRID_W, NA_KH * GRID_W), lambda h, b: (0, 0)),
            pl.BlockSpec(memory_space=pl.ANY),
        ],
        out_specs=tok_spec,
        out_shape=jax.ShapeDtypeStruct(o_ctx.shape, o_ctx.dtype),
        input_output_aliases={7: 0},
        scratch_shapes=[pltpu.VMEM((LAT_SEQ, n_keys), F32),
                        pltpu.VMEM((LAT_SEQ, n_keys), BF16),
                        pltpu.VMEM((LAT_SEQ, HEAD_DIM), F32),
                        pltpu.VMEM((n_dr - 1, GRID_W, 2 * GRID_W), F32)],
        compiler_params=_params("arbitrary", "arbitrary"),
        name="na_attention",
    )(q, k, v, cache_k, cache_v, _na_bias_rows(rpb_e), jnp.asarray(mask), o_ctx)


def _split_bf16(x, pieces):
    out = []
    for _ in range(pieces):
        p = x.astype(BF16)
        out.append(p)
        x = x - p.astype(F32)
    return out


def _moe_prenorm(x, g_ref, sc_ref, sh_ref):
    h = _modnorm(x, g_ref[...], sc_ref[...], sh_ref[...])
    return jnp.concatenate(_split_bf16(h, 2), axis=1)


def _router_logits(h_pieces, rw_ref, rb_ref):
    prod = jnp.dot(h_pieces, rw_ref[...], preferred_element_type=F32)
    return prod + pltpu.roll(prod, ROUTER_LANES - LOW_LANE0, 1) + rb_ref[...]


def _first_argmax(vals, valid, lane):
    v = jnp.where(valid, vals, -jnp.inf)
    m = jnp.max(v, axis=-1, keepdims=True)
    idx = jnp.min(jnp.where(jnp.logical_and(valid, v == m), lane, jnp.int32(1 << 20)),
                  axis=-1, keepdims=True)
    return m, idx


def _route_group(logits):
    lane = lax.broadcasted_iota(jnp.int32, logits.shape, 1)
    mg, gi = _first_argmax(logits, lane < N_GROUPS, lane)
    onehot_g = jnp.where(lane == gi, 1.0, 0.0)
    row = lax.broadcasted_iota(jnp.int32, (TOK_TILE, TOK_TILE), 0)
    col = lax.broadcasted_iota(jnp.int32, (TOK_TILE, TOK_TILE), 1)
    earlier = jnp.where(col < row, 1.0, 0.0).astype(BF16)
    rank = jnp.dot(earlier, onehot_g.astype(BF16), preferred_element_type=F32)
    return mg, gi, onehot_g, rank


def _route_weights(logits, group, comb_ref, cnt_ref):
    mg, gi, onehot_g, rank = group
    lane = lax.broadcasted_iota(jnp.int32, logits.shape, 1)
    is_g = lane < N_GROUPS
    pg = 1.0 / jnp.sum(jnp.where(is_g, jnp.exp(logits - mg), 0.0), axis=-1, keepdims=True)
    e_lane0 = EXPERT_LANE0 + gi * EXP_PER_GROUP
    is_e = jnp.logical_and(lane >= e_lane0, lane < e_lane0 + EXP_PER_GROUP)
    m1, i1 = _first_argmax(logits, is_e, lane)
    m2, i2 = _first_argmax(logits, jnp.logical_and(is_e, lane != i1), lane)
    e2 = jnp.exp(m2 - m1)
    den = 1.0 + e2
    w1 = (1.0 / den) * pg
    w2 = (e2 / den) * pg

    cnt = jnp.sum(onehot_g, axis=0, keepdims=True)
    padded = jnp.floor((cnt + (ROW_ALIGN - 1)) * (1.0 / ROW_ALIGN)) * ROW_ALIGN
    padded8 = jnp.broadcast_to(padded, (SUBLANES, ROUTER_LANES))
    run0 = (pltpu.roll(padded8, 1, 1) + pltpu.roll(padded8, 2, 1) + pltpu.roll(padded8, 3, 1))[0:1, :]
    pos = jnp.sum(onehot_g * (rank + run0), axis=-1, keepdims=True)

    comb = jnp.where(lane == i1, w1, jnp.where(lane == i2, w2, jnp.where(lane == POS_LANE, pos, onehot_g)))
    comb_ref[...] = comb
    cnt_ref[...] = cnt
    run_rows = [jnp.sum(jnp.where(lane[0:1, :] == g, padded, 0.0)).astype(jnp.int32) for g in range(N_GROUPS)]
    return comb, run_rows


def _sort_tile(h_hi, comb, hc_ref, cc_ref):
    pos_t = comb.T[POS_LANE:POS_LANE + 1, :]
    dest = lax.broadcasted_iota(jnp.int32, (TILE_ROWS, TOK_TILE), 0).astype(F32)
    perm = jnp.where(pos_t == dest, 1.0, 0.0).astype(BF16)
    hc_ref[...] = jnp.dot(perm, h_hi, preferred_element_type=F32).astype(BF16)
    c_parts = _split_bf16(comb, 3)
    cc_ref[...] = (jnp.dot(perm, c_parts[0], preferred_element_type=F32)
                   + jnp.dot(perm, c_parts[1], preferred_element_type=F32)
                   + jnp.dot(perm, c_parts[2], preferred_element_type=F32))


W_STAGE_COLS = 512


def _out_route_kernel(a_ref, w_hbm, gate_ref, g_ref, sc_ref, sh_ref, rw_ref, rb_ref, *refs, n_x, w_idx):
    x_refs = refs[:n_x]
    (xo_ref, hs_ref, cs_ref, comb_ref, cnt_ref, wb_ref, stage_ref, prev_ref, hcs_ref, ccs_ref,
     zh_ref, zc_ref, fill_ref, sent_ref, sem, sem_run, sem_zero) = refs[n_x:]
    i = pl.program_id(0)
    slot = i % 2

    @pl.when(i == 0)
    def _():
        prev_ref[...] = jnp.zeros_like(prev_ref)
        zh_ref[...] = jnp.zeros_like(zh_ref)
        zc_ref[...] = jnp.zeros_like(zc_ref)
        for g in range(N_GROUPS):
            fill_ref[g] = 0
        n_chunks = D // W_STAGE_COLS

        def chunk_copy(c):
            cols = pl.ds(c * W_STAGE_COLS, W_STAGE_COLS)
            return pltpu.make_async_copy(w_hbm.at[w_idx, :, cols], stage_ref.at[c % 2], sem.at[c % 2])

        chunk_copy(0).start()
        chunk_copy(1).start()
        for c in range(n_chunks):
            chunk_copy(c).wait()
            wb_ref[:, c * W_STAGE_COLS:(c + 1) * W_STAGE_COLS] = stage_ref[c % 2].astype(BF16)
            if c + 2 < n_chunks:
                chunk_copy(c + 2).start()

    h_prev = prev_ref[...]
    logits = _router_logits(h_prev, rw_ref, rb_ref)
    group = _route_group(logits)
    y = jnp.dot(a_ref[...], wb_ref[...], preferred_element_type=F32)
    x_mid = _token_rows(x_refs, jnp.minimum(i, N_TOK_TILES - 1), TOK_TILE) + gate_ref[...] * y
    xo_ref[...] = x_mid
    comb, run_rows = _route_weights(logits, group, comb_ref, cnt_ref)
    _sort_tile(h_prev[:, :D], comb, hcs_ref.at[slot], ccs_ref.at[slot])
    prev_ref[...] = _moe_prenorm(x_mid, g_ref, sc_ref, sh_ref)

    def for_each_copy(sl, rows, filled, fn):
        src0 = 0
        for g in range(N_GROUPS):
            for b, bit in enumerate(RUN_BITS):
                done = rows[g] & (-2 * bit)

                @pl.when((rows[g] & bit) != 0)
                def _():
                    src = pl.ds(pl.multiple_of(src0 + done, ROW_ALIGN), bit)
                    dst = pl.ds(pl.multiple_of(g * GROUP_CAP + filled[g] + done, ROW_ALIGN), bit)
                    fn(pltpu.make_async_copy(hcs_ref.at[sl, src, :], hs_ref.at[dst, :], sem_run.at[sl, 0, g, b]))
                    fn(pltpu.make_async_copy(ccs_ref.at[sl, src, :], cs_ref.at[dst, :], sem_run.at[sl, 1, g, b]))

            src0 = src0 + rows[g]

    @pl.when(i >= 1)
    def _():
        filled = [fill_ref[g] for g in range(N_GROUPS)]
        for_each_copy(slot, run_rows, filled, lambda cp: cp.start())
        for g in range(N_GROUPS):
            fill_ref[g] = filled[g] + run_rows[g]

        @pl.when(i >= 2)
        def _():
            sent = [sent_ref[g] for g in range(N_GROUPS)]
            for_each_copy(1 - slot, sent, [0] * N_GROUPS, lambda cp: cp.wait())

        for g in range(N_GROUPS):
            sent_ref[g] = run_rows[g]

    @pl.when(i == pl.num_programs(0) - 1)
    def _():
        for_each_copy(slot, [sent_ref[g] for g in range(N_GROUPS)], [0] * N_GROUPS, lambda cp: cp.wait())
        zero_copies = []
        for g in range(N_GROUPS):
            dst = pl.ds(pl.multiple_of(g * GROUP_CAP + fill_ref[g], ROW_ALIGN), ZERO_ROWS)
            zero_copies.append(pltpu.make_async_copy(zh_ref, hs_ref.at[dst, :], sem_zero.at[0, g]))
            zero_copies.append(pltpu.make_async_copy(zc_ref, cs_ref.at[dst, :], sem_zero.at[1, g]))
        for cp in zero_copies:
            cp.start()
        for cp in zero_copies:
            cp.wait()


def _out_proj_and_route(a_bf16, w, w_idx, x, mod, layer, norm_g, rw, rb):
    tm = TOK_TILE
    k = a_bf16.shape[1]
    last = N_TOK_TILES - 1
    proj = lambda i: jnp.minimum(i, last)
    routed = lambda i: jnp.maximum(i - 1, 0)
    if isinstance(x, tuple):
        n_ctx_tiles = N_CTX // tm
        x_specs = [pl.BlockSpec((tm, D), lambda i: (jnp.minimum(proj(i), n_ctx_tiles - 1), 0)),
                   pl.BlockSpec((tm, D), lambda i: (jnp.maximum(proj(i) - n_ctx_tiles, 0), 0))]
        x_args = list(x)
    else:
        x_specs = [pl.BlockSpec((tm, D), lambda i: (proj(i), 0))]
        x_args = [x]

    def mod_of_proj(which):
        return pl.BlockSpec((None, None, None, 1, D),
                            lambda i: (layer, _cond_of_row(proj(i) * tm), which, 0, 0))

    return pl.pallas_call(
        functools.partial(_out_route_kernel, n_x=len(x_args), w_idx=w_idx),
        grid=(N_TOK_TILES + 1,),
        in_specs=[
            pl.BlockSpec((tm, k), lambda i: (proj(i), 0)),
            pl.BlockSpec(memory_space=pl.ANY),
            mod_of_proj(2),
            pl.BlockSpec((1, D), lambda i: (0, 0)),
            mod_of_proj(4),
            mod_of_proj(3),
            pl.BlockSpec((2 * D, ROUTER_LANES), lambda i: (0, 0)),
            pl.BlockSpec((1, ROUTER_LANES), lambda i: (0, 0)),
        ] + x_specs,
        out_specs=[
            pl.BlockSpec((tm, D), lambda i: (proj(i), 0)),
            pl.BlockSpec(memory_space=pl.ANY),
            pl.BlockSpec(memory_space=pl.ANY),
            pl.BlockSpec((tm, ROUTER_LANES), lambda i: (routed(i), 0)),
            pl.BlockSpec((None, 1, ROUTER_LANES), lambda i: (routed(i), 0, 0)),
        ],
        out_shape=[
            jax.ShapeDtypeStruct((N_TOK, D), F32),
            jax.ShapeDtypeStruct((ROW_CAP, D), BF16),
            jax.ShapeDtypeStruct((ROW_CAP, ROUTER_LANES), F32),
            jax.ShapeDtypeStruct((N_TOK, ROUTER_LANES), F32),
            jax.ShapeDtypeStruct((N_TOK_TILES, 1, ROUTER_LANES), F32),
        ],
        scratch_shapes=[
            pltpu.VMEM((k, D), BF16),
            pltpu.VMEM((2, k, W_STAGE_COLS), F32),
            pltpu.VMEM((tm, 2 * D), BF16),
            pltpu.VMEM((2, TILE_ROWS, D), BF16),
            pltpu.VMEM((2, TILE_ROWS, ROUTER_LANES), F32),
            pltpu.VMEM((ZERO_ROWS, D), BF16),
            pltpu.VMEM((ZERO_ROWS, ROUTER_LANES), F32),
            pltpu.SMEM((N_GROUPS,), jnp.int32),
            pltpu.SMEM((N_GROUPS,), jnp.int32),
            pltpu.SemaphoreType.DMA((2,)),
            pltpu.SemaphoreType.DMA((2, 2, N_GROUPS, len(RUN_BITS))),
            pltpu.SemaphoreType.DMA((2, N_GROUPS)),
        ],
        compiler_params=_params("arbitrary"),
        name="out_proj_route",
    )(a_bf16, w, mod, norm_g.reshape(1, D), mod, mod, rw, rb, *x_args)


def _router_weights(rgw, rgb, rew, reb):
    n_out = N_GROUPS + N_EXPERTS
    rw = jnp.concatenate([rgw, rew.reshape(D, N_EXPERTS)], axis=1)
    hi = rw.astype(BF16)
    lo = (rw - hi.astype(F32)).astype(BF16)
    zeros = lambda n: jnp.zeros((D, n), BF16)
    top = jnp.concatenate([hi, zeros(LOW_LANE0 - n_out), lo, zeros(ROUTER_LANES - LOW_LANE0 - n_out)], axis=1)
    bottom = jnp.concatenate([hi, zeros(ROUTER_LANES - n_out)], axis=1)
    rb = jnp.concatenate([rgb, reb.reshape(N_EXPERTS)])
    rb = jnp.pad(rb, (0, ROUTER_LANES - n_out)).reshape(1, ROUTER_LANES)
    return jnp.concatenate([top, bottom], axis=0), rb


def _moe_plan(cnt):
    i32 = jnp.int32
    padded = (cnt + (ROW_ALIGN - 1)) // ROW_ALIGN * ROW_ALIGN
    run0 = jnp.cumsum(padded, axis=1) - padded
    length = jnp.sum(padded, axis=0)
    start = jnp.arange(N_GROUPS, dtype=i32) * GROUP_CAP
    off = start[None, :] + jnp.cumsum(padded, axis=0) - padded
    need = length
    ntile = (need + MOE_TILE - 1) // MOE_TILE
    cum = jnp.cumsum(ntile)
    total = cum[-1]
    k = jnp.arange(MOE_TILES_MAX, dtype=i32)
    kk = jnp.minimum(k, jnp.maximum(total - 1, 0))
    grp = jnp.minimum(jnp.sum((kk[:, None] >= cum[None, :]).astype(i32), axis=1), N_GROUPS - 1)
    j = kk - (cum - ntile)[grp]
    blk = start[grp] // MOE_TILE + j
    nsub = jnp.clip((need[grp] - j * MOE_TILE + MOE_SUB - 1) // MOE_SUB, 0, MOE_TILE // MOE_SUB)
    nsub = jnp.where(k < total, nsub, 0)
    runs = tuple(a.reshape(-1).astype(i32) for a in (off, run0, padded))
    return runs, (blk.astype(i32), grp.astype(i32), nsub.astype(i32))


def _run_pieces(run_refs, tile, g):
    off_ref, run0_ref, padded_ref = run_refs
    idx = tile * N_GROUPS + g
    n = padded_ref[idx]
    src0 = run0_ref[idx]
    dst0 = off_ref[idx]
    pieces = []
    for b, bit in enumerate(RUN_BITS):
        done = n & (-2 * bit)
        pieces.append((b, (n & bit) != 0, pl.multiple_of(src0 + done, ROW_ALIGN),
                       pl.multiple_of(dst0 + done, ROW_ALIGN), bit))
    return pieces


def _moe_kernel(blk_ref, grp_ref, nsub_ref, h_ref, c_ref, w1_ref, w3_ref, w2_ref, o_ref, hid_ref):
    k = pl.program_id(0)
    c = pl.program_id(1)
    ns = nsub_ref[k]
    n_sub_max = MOE_TILE // MOE_SUB

    def for_row_count(fn):
        for m in range(1, n_sub_max + 1):
            @pl.when(ns == m)
            def _():
                fn(m * MOE_SUB)

    @pl.when(jnp.logical_and(ns > 0, c < N_FF_CHUNKS))
    def _():
        w1 = w1_ref[...].astype(BF16)
        w3 = w3_ref[...].astype(BF16)
        e_lane = EXPERT_LANE0 + grp_ref[k] * EXP_PER_GROUP + c * FF_CHUNK // D_EXPERT

        def up(rows):
            h = h_ref[0:rows, :]
            a = jnp.dot(h, w1, preferred_element_type=F32)
            b = jnp.dot(h, w3, preferred_element_type=F32)
            comb = c_ref[0:rows, :]
            lane = lax.broadcasted_iota(jnp.int32, comb.shape, 1)
            cw = jnp.sum(jnp.where(lane == e_lane, comb, 0.0), axis=-1, keepdims=True)
            hid_ref[c, 0:rows, :] = (_silu(a) * b * cw).astype(BF16)

        for_row_count(up)

    @pl.when(jnp.logical_and(ns > 0, c >= N_FF_CHUNKS))
    def _():
        w2 = w2_ref[...].reshape(EXP_PER_GROUP * D_EXPERT, OUT_CHUNK).astype(BF16)

        def down(rows):
            hid = jnp.concatenate([hid_ref[j, 0:rows, :] for j in range(N_FF_CHUNKS)], axis=1)
            o_ref[0:rows, :] = jnp.dot(hid, w2, preferred_element_type=F32).astype(BF16)
            if rows < MOE_TILE:
                o_ref[rows:, :] = jnp.zeros((MOE_TILE - rows, OUT_CHUNK), BF16)

        for_row_count(down)


def _moe_experts(hs, cs, w1, w3, w2, layer, plan):
    per_e = D_EXPERT // FF_CHUNK
    n_steps = N_FF_CHUNKS + N_OUT_CHUNKS

    def step_of(k, c, nsub_ref):
        return jnp.where(nsub_ref[k] > 0, c, n_steps - 1)

    def w13_map(k, c, blk_ref, grp_ref, nsub_ref):
        cc = jnp.minimum(step_of(k, c, nsub_ref), N_FF_CHUNKS - 1)
        return (layer, grp_ref[k] * EXP_PER_GROUP + cc // per_e, 0, cc % per_e)

    def out_chunk(k, c, nsub_ref):
        return jnp.maximum(step_of(k, c, nsub_ref) - N_FF_CHUNKS, 0)

    w2_map = lambda k, c, blk_ref, grp_ref, nsub_ref: (layer, grp_ref[k], 0, out_chunk(k, c, nsub_ref))
    row_map = lambda k, c, blk_ref, grp_ref, nsub_ref: (blk_ref[k], 0)
    out_map = lambda k, c, blk_ref, grp_ref, nsub_ref: (blk_ref[k], out_chunk(k, c, nsub_ref))
    grid_spec = pltpu.PrefetchScalarGridSpec(
        num_scalar_prefetch=3,
        grid=(MOE_TILES_MAX, n_steps),
        in_specs=[
            pl.BlockSpec((MOE_TILE, D), row_map),
            pl.BlockSpec((MOE_TILE, ROUTER_LANES), row_map),
            pl.BlockSpec((None, None, D, FF_CHUNK), w13_map),
            pl.BlockSpec((None, None, D, FF_CHUNK), w13_map),
            pl.BlockSpec((None, EXP_PER_GROUP, D_EXPERT, OUT_CHUNK), w2_map),
        ],
        out_specs=pl.BlockSpec((MOE_TILE, OUT_CHUNK), out_map),
        scratch_shapes=[pltpu.VMEM((N_FF_CHUNKS, MOE_TILE, FF_CHUNK), BF16)],
    )
    return pl.pallas_call(
        _moe_kernel,
        grid_spec=grid_spec,
        out_shape=jax.ShapeDtypeStruct((ROW_CAP, D), BF16),
        compiler_params=_params("arbitrary", "arbitrary"),
        name="moe_experts",
    )(*plan, hs, cs, w1, w3, w2)


def _combine_kernel(off_ref, run0_ref, padded_ref, x_ref, gate_ref, comb_ref, ys_ref, *refs,
                    split_out, emit_next):
    if emit_next:
        (ng_ref, nsc_ref, nsh_ref), refs = refs[:3], refs[3:]
    out_refs, (yw_ref, sem) = refs[:-2], refs[-2:]
    run_refs = (off_ref, run0_ref, padded_ref)
    i = pl.program_id(0)
    n_steps = pl.num_programs(0)
    slot = i % 2

    def for_each_copy(tile, sl, fn):
        for g in range(N_GROUPS):
            for b, pred, src, dst, rows in _run_pieces(run_refs, tile, g):
                @pl.when(pred)
                def _():
                    fn(pltpu.make_async_copy(ys_ref.at[pl.ds(dst, rows), :],
                                             yw_ref.at[sl, pl.ds(src, rows), :], sem.at[sl, g, b]))

    def fetch(tile, sl):
        yw_ref[sl, TOK_TILE:, :] = jnp.zeros((TILE_ROWS - TOK_TILE, D), BF16)
        for_each_copy(tile, sl, lambda cp: cp.start())

    @pl.when(i == 0)
    def _():
        fetch(i, slot)

    @pl.when(i + 1 < n_steps)
    def _():
        fetch(i + 1, 1 - slot)

    for_each_copy(i, slot, lambda cp: cp.wait())

    pos = comb_ref[:, POS_LANE:POS_LANE + 1].astype(jnp.int32)
    wcol = lax.broadcasted_iota(jnp.int32, (TOK_TILE, TILE_ROWS), 1)
    perm = jnp.where(wcol == pos, 1.0, 0.0).astype(BF16)
    y = jnp.dot(perm, yw_ref[slot], preferred_element_type=F32)
    out = x_ref[...] + gate_ref[...] * y
    if emit_next:
        out_refs[1][...] = _modnorm(out, ng_ref[...], nsc_ref[...], nsh_ref[...]).astype(BF16)
    if not split_out:
        out_refs[0][...] = out
    else:
        @pl.when(i < N_CTX // TOK_TILE)
        def _():
            out_refs[0][...] = out

        @pl.when(i >= N_CTX // TOK_TILE)
        def _():
            out_refs[1][...] = out


def _combine(x, mod, layer, comb, ys, runs, next_norm_g):
    n_ctx_tiles = N_CTX // TOK_TILE
    split_out = next_norm_g is None
    tile_spec = pl.BlockSpec((TOK_TILE, D), lambda i, *_: (i, 0))
    in_specs = [
        tile_spec,
        _mod_spec(layer, 5, TOK_TILE, row_axis=0),
        pl.BlockSpec((TOK_TILE, ROUTER_LANES), lambda i, *_: (i, 0)),
        pl.BlockSpec(memory_space=pl.ANY),
    ]
    args = [x, mod, comb, ys]
    if split_out:
        out_specs = [pl.BlockSpec((TOK_TILE, D), lambda i, *_: (jnp.minimum(i, n_ctx_tiles - 1), 0)),
                     pl.BlockSpec((TOK_TILE, D), lambda i, *_: (jnp.maximum(i - n_ctx_tiles, 0), 0))]
        out_shape = [jax.ShapeDtypeStruct((N_CTX, D), F32), jax.ShapeDtypeStruct((N_LAT, D), F32)]
    else:
        in_specs += [pl.BlockSpec((1, D), lambda i, *_: (0, 0)),
                     _mod_spec(layer + 1, 1, TOK_TILE, row_axis=0),
                     _mod_spec(layer + 1, 0, TOK_TILE, row_axis=0)]
        args += [next_norm_g.reshape(1, D), mod, mod]
        out_specs = [tile_spec, tile_spec]
        out_shape = [jax.ShapeDtypeStruct((N_TOK, D), F32), jax.ShapeDtypeStruct((N_TOK, D), BF16)]
    grid_spec = pltpu.PrefetchScalarGridSpec(
        num_scalar_prefetch=3,
        grid=(N_TOK_TILES,),
        in_specs=in_specs,
        out_specs=out_specs,
        scratch_shapes=[
            pltpu.VMEM((2, TILE_ROWS, D), BF16),
            pltpu.SemaphoreType.DMA((2, N_GROUPS, len(RUN_BITS))),
        ],
    )
    return pl.pallas_call(
        functools.partial(_combine_kernel, split_out=split_out, emit_next=not split_out),
        grid_spec=grid_spec,
        out_shape=out_shape,
        compiler_params=_params("arbitrary"),
        name="moe_combine",
    )(*runs, *args)


def _moe_experts_and_combine(x, routed, mod, layer, w1, w3, w2, next_norm_g):
    hs, cs, comb, cnt = routed
    runs, plan = _moe_plan(cnt[:, 0, :N_GROUPS].astype(jnp.int32))
    ys = _moe_experts(hs, cs, w1, w3, w2, layer, plan)
    return _combine(x, mod, layer, comb, ys, runs, next_norm_g)


def kernel(x_prompt, x_sample, cache_k, cache_v, c, c_ctx, ada_w, ada_b, norm_mix_g, norm_ffn_g,
           mix_in_w, pool_w, pool_scale, conv_w, conv_b, conv_ln_g, conv_ln_b, mix_out_w,
           qkv_w, q_norm_g, k_norm_g, rpb, attn_out_w, router_g_w, router_g_b, router_e_w,
           router_e_b, exp_w1, exp_w3, exp_w2):
    x = (x_prompt.reshape(N_CTX, D), x_sample.reshape(N_LAT, D))
    cvec8 = jnp.concatenate([c_ctx[None, :], c, jnp.zeros((8 - 1 - N_LAT_SEQ, D), F32)], axis=0)
    mod = _mod_table(_adaln_first_layer(cvec8, ada_w, ada_b))
    n_attn = DEPTH // 2
    cache_k2 = cache_k.reshape(N_LAT_SEQ, n_attn, CTX_SEQ * N_HEADS, HEAD_DIM)
    cache_v2 = cache_v.reshape(N_LAT_SEQ, n_attn, CTX_SEQ * N_HEADS, HEAD_DIM)

    caches = None
    h = None
    for l in range(DEPTH):
        e = l // 2
        if l % 2 == 0:
            u = _mm_prologue(x, mod, l, norm_mix_g[l], mix_in_w, e, h=h)
            mixed, *later_mod = _seq_mixer(u, pool_w[e].astype(BF16), pool_scale[e], conv_w[e], conv_b[e],
                                           conv_ln_g[e], conv_ln_b[e],
                                           ada=(cvec8, ada_w, ada_b) if l == 0 else None)
            if later_mod:
                mod = jnp.concatenate([mod, _mod_table(later_mod[0])], axis=0)
            out_w = mix_out_w
        else:
            q, k, v, new_k, new_v = _qkv_proj(h, e, qkv_w, q_norm_g[e], k_norm_g[e], caches)
            caches = (new_k, new_v)
            mixed = _ctx_attention(q, k, v)
            mixed = _na_attention(q, k, v, cache_k2, cache_v2, e, rpb[e], mixed)
            out_w = attn_out_w
        rw, rb = _router_weights(router_g_w[l], router_g_b[l], router_e_w[l], router_e_b[l])
        x, *routed = _out_proj_and_route(mixed, out_w, e, x, mod, l, norm_ffn_g[l], rw, rb)
        if l < DEPTH - 1:
            x, h = _moe_experts_and_combine(x, routed, mod, l, exp_w1, exp_w3, exp_w2,
                                            next_norm_g=norm_mix_g[l + 1])
        else:
            y_ctx, y_lat = _moe_experts_and_combine(x, routed, mod, l, exp_w1, exp_w3, exp_w2,
                                                    next_norm_g=None)

    y_prompt = y_ctx.reshape(N_CTX_SEQ, CTX_SEQ, D)
    y_sample = y_lat.reshape(N_LAT_SEQ, LAT_SEQ, D)
    cache_dims = (N_CTX_SEQ, n_attn, CTX_SEQ, N_HEADS, HEAD_DIM)
    return (y_prompt, y_sample, caches[0].reshape(cache_dims), caches[1].reshape(cache_dims))
```

```python
import functools

import numpy as np
import jax
import jax.numpy as jnp
from jax import lax
from jax.experimental import pallas as pl
from jax.experimental.pallas import tpu as pltpu

D = 2048
N_CTX_SEQ = 16
CTX_SEQ = 256
N_LAT_SEQ = 2
LAT_SEQ = 1024
N_CTX = N_CTX_SEQ * CTX_SEQ
N_LAT = N_LAT_SEQ * LAT_SEQ
N_TOK = N_CTX + N_LAT
DEPTH = 4
GRID_W = 64
GRID_H = LAT_SEQ // GRID_W
N_HEADS = 16
HEAD_DIM = 128
ATTN_SCALE = HEAD_DIM ** -0.5
D_POOL = 1024
D_CONV = 1024
POOL_WINDOWS = (2, 4, 8, 16)
POOL_GC = 256
CONV_W = 31
NA_KH = 8
NA_KW = 16
N_GROUPS = 4
EXP_PER_GROUP = 4
N_EXPERTS = 16
D_EXPERT = 512
EPS = 1e-6
NEG = -1e30

LANES = 128
SUBLANES = 8
SEQ_TILE = 256
HALO = 16
ROUTER_LANES = 128
EXPERT_LANE0 = N_GROUPS
POS_LANE = EXPERT_LANE0 + N_EXPERTS
LOW_LANE0 = 32
VMEM_LIMIT = 56 * 1024 * 1024

TOK_TILE = 256
N_TOK_TILES = N_TOK // TOK_TILE
ROW_ALIGN = 16
RUN_BITS = (256, 128, 64, 32, 16)
TILE_ROWS = TOK_TILE + N_GROUPS * ROW_ALIGN
MOE_TILE = 1024
MOE_SUB = 128
FF_CHUNK = 512
OUT_CHUNK = 512
N_FF_CHUNKS = EXP_PER_GROUP * D_EXPERT // FF_CHUNK
N_OUT_CHUNKS = D // OUT_CHUNK
ZERO_ROWS = MOE_SUB
MAX_PADDED = N_TOK + N_TOK_TILES * N_GROUPS * (ROW_ALIGN - 1)
MOE_TILES_MAX = (MAX_PADDED + N_GROUPS * (MOE_TILE - 1)) // MOE_TILE
GROUP_CAP = -(-(N_TOK + N_TOK_TILES * (ROW_ALIGN - 1) + ZERO_ROWS) // MOE_TILE) * MOE_TILE
ROW_CAP = N_GROUPS * GROUP_CAP

F32 = jnp.float32
BF16 = jnp.bfloat16


def _cond_of_row(row):
    row = jnp.minimum(row, N_TOK - 1)
    return jnp.where(row < N_CTX, 0, 1 + (row - N_CTX) // LAT_SEQ)


def _mod_spec(layer, which, tm, tn=D, row_axis=1, col_axis=None):
    def index_map(*ids):
        cond = _cond_of_row(ids[row_axis] * tm)
        col = 0 if col_axis is None else ids[col_axis]
        return (layer, cond, which, 0, col)
    return pl.BlockSpec((None, None, None, 1, tn), index_map)


def _silu(x):
    return x / (1.0 + jnp.exp(-x))


def _modnorm(x, g, sc, sh):
    ms = jnp.mean(x * x, axis=-1, keepdims=True)
    y = x * lax.rsqrt(ms + EPS) * g
    return y * (1.0 + sc) + sh


def _params(*sem):
    return pltpu.CompilerParams(dimension_semantics=sem, vmem_limit_bytes=VMEM_LIMIT)


def _ada_kernel(c_ref, w_ref, b_ref, o_ref):
    s = _silu(c_ref[...]).astype(BF16)
    o_ref[...] = jnp.dot(s, w_ref[...].astype(BF16), preferred_element_type=F32) + b_ref[...]


N_MOD = 6 * D


def _mod_table(raw):
    return raw[:, :3].reshape(raw.shape[0], 3, 6, 1, D)


def _adaln_first_layer(cvec8, ada_w, ada_b):
    tn = 1024
    return pl.pallas_call(
        _ada_kernel,
        grid=(N_MOD // tn,),
        in_specs=[
            pl.BlockSpec((8, D), lambda j: (0, 0)),
            pl.BlockSpec((None, D, tn), lambda j: (0, 0, j)),
            pl.BlockSpec((None, 1, tn), lambda j: (0, 0, j)),
        ],
        out_specs=pl.BlockSpec((None, 8, tn), lambda j: (0, 0, j)),
        out_shape=jax.ShapeDtypeStruct((1, 8, N_MOD), F32),
        compiler_params=_params("arbitrary"),
        name="adaln",
    )(cvec8, ada_w, ada_b.reshape(DEPTH, 1, N_MOD))


def _token_specs(x, tm, tn, idx):
    if not isinstance(x, tuple):
        return [pl.BlockSpec((tm, tn), lambda j, i: idx(i, j))], [x]
    n_ctx_tiles = N_CTX // tm
    ctx_map = lambda j, i: idx(jnp.minimum(i, n_ctx_tiles - 1), j)
    lat_map = lambda j, i: idx(jnp.maximum(i - n_ctx_tiles, 0), j)
    return [pl.BlockSpec((tm, tn), ctx_map), pl.BlockSpec((tm, tn), lat_map)], list(x)


def _token_rows(refs, i, tm):
    if len(refs) == 1:
        return refs[0][...]
    return jnp.where(i < N_CTX // tm, refs[0][...], refs[1][...])


def _cast_weights_once(i, pairs):
    @pl.when(i == 0)
    def _():
        for w_ref, wb_ref in pairs:
            wb_ref[...] = w_ref[...].astype(BF16)


def _mm_pro_kernel(*refs, n_x, tm):
    x_refs = refs[:n_x]
    g_ref, sc_ref, sh_ref, w_ref, o_ref, wb_ref = refs[n_x:]
    i = pl.program_id(1)
    _cast_weights_once(i, [(w_ref, wb_ref)])
    h = _modnorm(_token_rows(x_refs, i, tm), g_ref[...], sc_ref[...], sh_ref[...]).astype(BF16)
    o_ref[...] = jnp.dot(h, wb_ref[...], preferred_element_type=F32)


def _mm_plain_kernel(h_ref, w_ref, o_ref, wb_ref):
    _cast_weights_once(pl.program_id(1), [(w_ref, wb_ref)])
    o_ref[...] = jnp.dot(h_ref[...], wb_ref[...], preferred_element_type=F32)


def _mm_prologue(x, mod, layer, norm_g, w, w_idx, h=None, tm=512, tn=1024):
    f = w.shape[2]
    if h is not None:
        tm = 2 * tm
    w_spec = pl.BlockSpec((None, D, tn), lambda j, i: (w_idx, 0, j))
    if h is None:
        x_specs, x_args = _token_specs(x, tm, D, lambda i, j: (i, 0))
        body = functools.partial(_mm_pro_kernel, n_x=len(x_args), tm=tm)
        in_specs = x_specs + [pl.BlockSpec((1, D), lambda j, i: (0, 0)),
                              _mod_spec(layer, 1, tm), _mod_spec(layer, 0, tm), w_spec]
        args = x_args + [norm_g.reshape(1, D), mod, mod, w]
    else:
        body = _mm_plain_kernel
        in_specs = [pl.BlockSpec((tm, D), lambda j, i: (i, 0)), w_spec]
        args = [h, w]
    return pl.pallas_call(
        body,
        grid=(f // tn, N_TOK // tm),
        in_specs=in_specs,
        out_specs=pl.BlockSpec((tm, tn), lambda j, i: (i, j)),
        out_shape=jax.ShapeDtypeStruct((N_TOK, f), F32),
        scratch_shapes=[pltpu.VMEM((D, tn), BF16)],
        compiler_params=_params("arbitrary", "arbitrary"),
        name="mm_prologue",
    )(*args)


def _qkv_kernel(h_ref, wq_ref, wk_ref, wv_ref, qg_ref, kg_ref, *rest, tn, n_ctx_tiles, seqs_per_tile):
    q_ref, k_ref, v_ref, ck_ref, cv_ref, wqb_ref, wkb_ref, wvb_ref, kn_ref, vf_ref = rest[-10:]
    i = pl.program_id(1)
    _cast_weights_once(i, [(wq_ref, wqb_ref), (wk_ref, wkb_ref), (wv_ref, wvb_ref)])
    h = h_ref[...]
    q = jnp.dot(h, wqb_ref[...], preferred_element_type=F32)
    k = jnp.dot(h, wkb_ref[...], preferred_element_type=F32)
    v = jnp.dot(h, wvb_ref[...], preferred_element_type=F32)
    for hh in range(tn // HEAD_DIM):
        ls = slice(hh * HEAD_DIM, (hh + 1) * HEAD_DIM)
        qh = q[:, ls]
        kh = k[:, ls]
        qn = qh * lax.rsqrt(jnp.mean(qh * qh, axis=-1, keepdims=True) + EPS) * qg_ref[:, ls]
        kn = kh * lax.rsqrt(jnp.mean(kh * kh, axis=-1, keepdims=True) + EPS) * kg_ref[:, ls]
        q_ref[:, ls] = qn.astype(BF16)
        k_ref[:, ls] = kn.astype(BF16)
        kn_ref[:, ls] = kn
    v_ref[...] = v.astype(BF16)
    vf_ref[...] = v

    @pl.when(i < n_ctx_tiles)
    def _():
        ck_ref[...] = kn_ref[...].reshape(seqs_per_tile, CTX_SEQ, tn)
        cv_ref[...] = vf_ref[...].reshape(seqs_per_tile, CTX_SEQ, tn)


def _qkv_proj(h, w_idx, w, q_gain, k_gain, caches, tm=512, tn=512):
    n = h.shape[0]
    n_attn = DEPTH // 2
    ncol = D // tn
    seqs_per_tile = tm // CTX_SEQ
    n_ctx_tiles = N_CTX // tm
    cache_shape = jax.ShapeDtypeStruct((N_CTX_SEQ, n_attn, CTX_SEQ, D), F32)
    cache_spec = pl.BlockSpec(
        (seqs_per_tile, None, CTX_SEQ, tn),
        lambda j, i: (jnp.minimum(i, n_ctx_tiles - 1), w_idx, 0, j))
    in_specs = [
        pl.BlockSpec((tm, D), lambda j, i: (i, 0)),
        pl.BlockSpec((None, D, tn), lambda j, i: (w_idx, 0, j)),
        pl.BlockSpec((None, D, tn), lambda j, i: (w_idx, 0, ncol + j)),
        pl.BlockSpec((None, D, tn), lambda j, i: (w_idx, 0, 2 * ncol + j)),
        pl.BlockSpec((1, tn), lambda j, i: (0, j)),
        pl.BlockSpec((1, tn), lambda j, i: (0, j)),
    ]
    args = [h, w, w, w,
            jnp.tile(q_gain, N_HEADS).reshape(1, D), jnp.tile(k_gain, N_HEADS).reshape(1, D)]
    aliases = {}
    if caches is not None:
        in_specs += [pl.BlockSpec(memory_space=pl.ANY), pl.BlockSpec(memory_space=pl.ANY)]
        aliases = {len(args): 3, len(args) + 1: 4}
        args += list(caches)
    act_spec = pl.BlockSpec((tm, tn), lambda j, i: (i, j))
    act_shape = jax.ShapeDtypeStruct((n, D), BF16)
    return pl.pallas_call(
        functools.partial(_qkv_kernel, tn=tn, n_ctx_tiles=n_ctx_tiles, seqs_per_tile=seqs_per_tile),
        grid=(ncol, n // tm),
        in_specs=in_specs,
        out_specs=[act_spec, act_spec, act_spec, cache_spec, cache_spec],
        out_shape=[act_shape, act_shape, act_shape, cache_shape, cache_shape],
        input_output_aliases=aliases,
        scratch_shapes=[pltpu.VMEM((D, tn), BF16)] * 3 + [pltpu.VMEM((tm, tn), F32)] * 2,
        compiler_params=_params("arbitrary", "arbitrary"),
        name="qkv_proj",
    )(*args)


def _seq_mix_tile(i, u_ref, top_ref, bot_ref, pw_ref, ps_ref, cw_ref, cb_ref, lg_ref, lb_ref,
                  o_ref, zpad_ref, conv_ref):
    n_ctx_tiles = N_CTX // SEQ_TILE
    tiles_per_lat = LAT_SEQ // SEQ_TILE
    is_lat = i >= n_ctx_tiles
    chunk = jnp.where(is_lat, (i - n_ctx_tiles) % tiles_per_lat, 0)
    top_ok = jnp.logical_and(is_lat, chunk > 0)
    bot_ok = jnp.logical_and(is_lat, chunk < tiles_per_lat - 1)
    seq_len = jnp.where(is_lat, LAT_SEQ, CTX_SEQ)
    t = chunk * SEQ_TILE + lax.broadcasted_iota(jnp.int32, (SEQ_TILE, 1), 0)
    rows = SEQ_TILE + 2 * HALO

    for g, win in enumerate(POOL_WINDOWS):
        ls = slice(g * POOL_GC, (g + 1) * POOL_GC)
        mid = u_ref[:, ls]
        top = jnp.where(top_ok, top_ref[:, ls], 0.0)
        bot = jnp.where(bot_ok, bot_ref[:, ls], 0.0)
        up = jnp.concatenate([top, mid, bot], axis=0)
        s = pltpu.roll(up, 1, 0) + up
        if win >= 4:
            s = pltpu.roll(s, 1, 0) + pltpu.roll(s, rows - 1, 0)
        if win >= 8:
            s = pltpu.roll(s, 2, 0) + pltpu.roll(s, rows - 2, 0)
        if win >= 16:
            s = pltpu.roll(s, 4, 0) + pltpu.roll(s, rows - 4, 0)
        s = s[HALO:HALO + SEQ_TILE]
        lo = jnp.maximum(t - win // 2, 0)
        hi = jnp.minimum(t + (win - win // 2) - 1, seq_len - 1)
        cnt = (hi - lo + 1).astype(F32)
        p = (s / cnt - mid).astype(BF16)
        y = jnp.dot(p, pw_ref[g], preferred_element_type=F32) * ps_ref[:, ls]
        o_ref[:, ls] = y.astype(BF16)

    def glu(ref):
        return ref[:, D_POOL:D_POOL + D_CONV] * (1.0 / (1.0 + jnp.exp(-ref[:, D_POOL + D_CONV:])))

    zpad_ref[0:HALO, :] = jnp.where(top_ok, glu(top_ref), 0.0)
    zpad_ref[HALO:HALO + SEQ_TILE, :] = glu(u_ref)
    zpad_ref[HALO + SEQ_TILE:rows, :] = jnp.where(bot_ok, glu(bot_ref), 0.0)
    for lt in range(D_CONV // LANES):
        ls = slice(lt * LANES, (lt + 1) * LANES)
        zp = zpad_ref[:, ls]
        acc = jnp.zeros((SEQ_TILE, LANES), F32)
        for b in range(SUBLANES):
            sb = zp if b == 0 else pltpu.roll(zp, rows - b, 0)
            for a in range(-(-(CONV_W + 1) // SUBLANES)):
                off = SUBLANES * a + b
                if 1 <= off <= CONV_W:
                    acc = acc + cw_ref[off - 1:off, ls] * sb[SUBLANES * a:SUBLANES * a + SEQ_TILE]
        conv_ref[:, ls] = acc + cb_ref[:, ls]
    zc = conv_ref[...]
    mu = jnp.mean(zc, axis=-1, keepdims=True)
    d = zc - mu
    var = jnp.mean(d * d, axis=-1, keepdims=True)
    zn = d * lax.rsqrt(var + EPS) * lg_ref[...] + lb_ref[...]
    o_ref[:, D_POOL:] = _silu(zn).astype(BF16)


N_SEQ_INPUTS = 9


def _seq_kernel(*refs, with_ada):
    seq_in, refs = refs[:N_SEQ_INPUTS], refs[N_SEQ_INPUTS:]
    if with_ada:
        (c_ref, w_ref, b_ref), refs = refs[:3], refs[3:]
        o_ref, mod_ref, zpad_ref, conv_ref = refs
        _ada_kernel(c_ref, w_ref, b_ref, mod_ref)
    else:
        o_ref, zpad_ref, conv_ref = refs
    _seq_mix_tile(pl.program_id(0), *seq_in, o_ref, zpad_ref, conv_ref)


def _seq_mixer(u, pool_w_bf16, pool_scale, conv_w, conv_b, ln_g, ln_b, ada=None):
    n = u.shape[0]
    f_in = u.shape[1]
    n_steps = n // SEQ_TILE
    hb = SEQ_TILE // HALO
    n_hblocks = n // HALO
    const = lambda *shape: pl.BlockSpec(shape, lambda i: (0,) * len(shape))
    in_specs = [
        pl.BlockSpec((SEQ_TILE, f_in), lambda i: (i, 0)),
        pl.BlockSpec((HALO, f_in), lambda i: (jnp.maximum(i * hb - 1, 0), 0)),
        pl.BlockSpec((HALO, f_in), lambda i: (jnp.minimum((i + 1) * hb, n_hblocks - 1), 0)),
        const(len(POOL_WINDOWS), POOL_GC, POOL_GC),
        const(1, D_POOL), const(CONV_W, D_CONV), const(1, D_CONV), const(1, D_CONV), const(1, D_CONV),
    ]
    args = [u, u, u, pool_w_bf16, pool_scale.reshape(1, D_POOL), conv_w, conv_b.reshape(1, D_CONV),
            ln_g.reshape(1, D_CONV), ln_b.reshape(1, D_CONV)]
    out_specs = [pl.BlockSpec((SEQ_TILE, D_POOL + D_CONV), lambda i: (i, 0))]
    out_shape = [jax.ShapeDtypeStruct((n, D_POOL + D_CONV), BF16)]
    if ada is not None:
        cvec8, ada_w, ada_b = ada
        later = DEPTH - 1
        chunk = later * N_MOD // n_steps
        per_layer = N_MOD // chunk
        chunk_map = lambda i: (1 + i // per_layer, 0, i % per_layer)
        in_specs += [const(8, D),
                     pl.BlockSpec((None, D, chunk), chunk_map),
                     pl.BlockSpec((None, 1, chunk), chunk_map)]
        args += [cvec8, ada_w, ada_b.reshape(DEPTH, 1, N_MOD)]
        out_specs.append(pl.BlockSpec((None, 8, chunk), lambda i: (i // per_layer, 0, i % per_layer)))
        out_shape.append(jax.ShapeDtypeStruct((later, 8, N_MOD), F32))
    return pl.pallas_call(
        functools.partial(_seq_kernel, with_ada=ada is not None),
        grid=(n_steps,),
        in_specs=in_specs,
        out_specs=out_specs,
        out_shape=out_shape,
        scratch_shapes=[
            pltpu.VMEM((SEQ_TILE + 2 * HALO, D_CONV), F32),
            pltpu.VMEM((SEQ_TILE, D_CONV), F32),
        ],
        compiler_params=_params("arbitrary"),
        name="seq_mixer",
    )(*args)


_NT = (((1,), (1,)), ((), ()))


def _ctx_attn_kernel(q_ref, k_ref, v_ref, o_ref, s_ref):
    for h in range(N_HEADS):
        ls = slice(h * HEAD_DIM, (h + 1) * HEAD_DIM)
        s_ref[h] = lax.dot_general(q_ref[:, ls], k_ref[:, ls], _NT, preferred_element_type=F32) * ATTN_SCALE
    for h in range(N_HEADS):
        ls = slice(h * HEAD_DIM, (h + 1) * HEAD_DIM)
        s = s_ref[h]
        e = jnp.exp(s - jnp.max(s, axis=-1, keepdims=True))
        inv = 1.0 / jnp.sum(e, axis=-1, keepdims=True)
        o = jnp.dot(e.astype(BF16), v_ref[:, ls], preferred_element_type=F32)
        o_ref[:, ls] = (o * inv).astype(BF16)


def _ctx_attention(q, k, v):
    n = q.shape[0]
    spec = pl.BlockSpec((CTX_SEQ, D), lambda b: (b, 0))
    return pl.pallas_call(
        _ctx_attn_kernel,
        grid=(N_CTX_SEQ,),
        in_specs=[spec, spec, spec],
        out_specs=spec,
        out_shape=jax.ShapeDtypeStruct((n, D), BF16),
        scratch_shapes=[pltpu.VMEM((N_HEADS, CTX_SEQ, CTX_SEQ), F32)],
        compiler_params=_params("arbitrary"),
        name="ctx_attention",
    )(q, k, v)


def _na_window_mask():
    cq = np.arange(GRID_W)[:, None]
    kc = np.arange(GRID_W)[None, :]
    win0 = np.clip(cq - NA_KW // 2, 0, GRID_W - NA_KW)
    mask = ((kc >= win0) & (kc < win0 + NA_KW)).astype(np.float32)
    return np.tile(mask, (1, NA_KH))


def _na_bias_rows(rpb_e):
    centre = NA_KW - 1
    pad = jnp.zeros(rpb_e.shape[:2] + (2 * GRID_W - (2 * NA_KW - 1),), F32)
    return jnp.concatenate([rpb_e[..., centre:], pad, rpb_e[..., :centre]], axis=-1)


def _na_key_row0(r):
    return min(max(r - NA_KH // 2, 0), GRID_H - NA_KH)


def _na_segments():
    segs = []
    for r in range(GRID_H):
        kr0 = _na_key_row0(r)
        if segs and segs[-1][2] == kr0:
            segs[-1][1] = r + 1
        else:
            segs.append([r, r + 1, kr0])
    return segs


def _na_kernel(q_ref, k_ref, v_ref, kc_ref, vc_ref, rows_ref, mask_ref, o_in_ref, o_ref,
               s_ref, p_ref, acc_ref, tz_ref):
    del o_in_ref
    band = NA_KH * GRID_W
    n_ctx = CTX_SEQ
    kc = kc_ref[...].astype(BF16)
    vc = vc_ref[...].astype(BF16)
    @pl.when(pl.program_id(1) == 0)
    def _():
        n_dr = 2 * NA_KH - 1
        toeplitz = [pltpu.roll(jnp.broadcast_to(rows_ref[d:d + 1, :], (GRID_W, 2 * GRID_W)), 0, 1,
                               stride=1, stride_axis=0) for d in range(n_dr)]
        left = lax.broadcasted_iota(jnp.int32, (GRID_W, 2 * GRID_W), 1) < GRID_W
        for d in range(n_dr - 1):
            tz_ref[d] = jnp.where(left, toeplitz[d], pltpu.roll(toeplitz[d + 1], GRID_W, 1))
    for r0, r1, kr0 in _na_segments():
        rs = slice(r0 * GRID_W, r1 * GRID_W)
        kb = k_ref[kr0 * GRID_W:kr0 * GRID_W + band, :]
        s_ref[rs, 0:band] = lax.dot_general(q_ref[rs, :], kb, _NT, preferred_element_type=F32) * ATTN_SCALE
    s_ref[:, band:band + n_ctx] = lax.dot_general(q_ref[...], kc, _NT, preferred_element_type=F32) * ATTN_SCALE
    mask = mask_ref[...] > 0.5
    for r in range(GRID_H):
        rs = slice(r * GRID_W, (r + 1) * GRID_W)
        kr0 = _na_key_row0(r)
        bias = jnp.concatenate(
            [tz_ref[kr0 + j - r + NA_KH - 1] for j in range(0, NA_KH, 2)], axis=-1)
        s_loc = jnp.where(mask, s_ref[rs, 0:band] + bias, NEG)
        s_ctx = s_ref[rs, band:band + n_ctx]
        m = jnp.maximum(jnp.max(s_loc, axis=-1, keepdims=True), jnp.max(s_ctx, axis=-1, keepdims=True))
        e_loc = jnp.exp(s_loc - m)
        e_ctx = jnp.exp(s_ctx - m)
        inv = 1.0 / (jnp.sum(e_loc, axis=-1, keepdims=True) + jnp.sum(e_ctx, axis=-1, keepdims=True))
        p_ref[rs, 0:band] = (e_loc * inv).astype(BF16)
        p_ref[rs, band:band + n_ctx] = (e_ctx * inv).astype(BF16)
    for r0, r1, kr0 in _na_segments():
        rs = slice(r0 * GRID_W, r1 * GRID_W)
        vb = v_ref[kr0 * GRID_W:kr0 * GRID_W + band, :]
        acc_ref[rs, :] = jnp.dot(p_ref[rs, 0:band], vb, preferred_element_type=F32)
    o_ctx = jnp.dot(p_ref[:, band:band + n_ctx], vc, preferred_element_type=F32)
    o_ref[...] = (acc_ref[...] + o_ctx).astype(BF16)


def _na_attention(q, k, v, cache_k_e, cache_v_e, rpb_e, o_ctx):
    mask = _na_window_mask()
    lat0 = N_CTX // LAT_SEQ
    n_dr = 2 * NA_KH - 1
    n_keys = NA_KH * GRID_W + CTX_SEQ
    tok_spec = pl.BlockSpec((LAT_SEQ, HEAD_DIM), lambda h, b: (lat0 + b, h))
    cache_spec = pl.BlockSpec((None, CTX_SEQ, HEAD_DIM), lambda h, b: (b, 0, h))
    return pl.pallas_call(
        _na_kernel,
        grid=(N_HEADS, N_LAT_SEQ),
        in_specs=[
            tok_spec, tok_spec, tok_spec, cache_spec, cache_spec,
            pl.BlockSpec((None, n_dr, 2 * GRID_W), lambda h, b: (h, 0, 0)),
            pl.BlockSpec((GRID_W, NA_KH * GRID_W), lambda h, b: (0, 0)),
            pl.BlockSpec(memory_space=pl.ANY),
        ],
        out_specs=tok_spec,
        out_shape=jax.ShapeDtypeStruct(o_ctx.shape, o_ctx.dtype),
        input_output_aliases={7: 0},
        scratch_shapes=[pltpu.VMEM((LAT_SEQ, n_keys), F32),
                        pltpu.VMEM((LAT_SEQ, n_keys), BF16),
                        pltpu.VMEM((LAT_SEQ, HEAD_DIM), F32),
                        pltpu.VMEM((n_dr - 1, GRID_W, 2 * GRID_W), F32)],
        compiler_params=_params("arbitrary", "arbitrary"),
        name="na_attention",
    )(q, k, v, cache_k_e, cache_v_e, _na_bias_rows(rpb_e), jnp.asarray(mask), o_ctx)


def _split_bf16(x, pieces):
    out = []
    for _ in range(pieces):
        p = x.astype(BF16)
        out.append(p)
        x = x - p.astype(F32)
    return out


def _moe_prenorm(x, g_ref, sc_ref, sh_ref):
    h = _modnorm(x, g_ref[...], sc_ref[...], sh_ref[...])
    return jnp.concatenate(_split_bf16(h, 2), axis=1)


def _router_logits(h_pieces, rw_ref, rb_ref):
    prod = jnp.dot(h_pieces, rw_ref[...], preferred_element_type=F32)
    return prod + pltpu.roll(prod, ROUTER_LANES - LOW_LANE0, 1) + rb_ref[...]


def _first_argmax(vals, valid, lane):
    v = jnp.where(valid, vals, -jnp.inf)
    m = jnp.max(v, axis=-1, keepdims=True)
    idx = jnp.min(jnp.where(jnp.logical_and(valid, v == m), lane, jnp.int32(1 << 20)),
                  axis=-1, keepdims=True)
    return m, idx


def _route_group(logits):
    lane = lax.broadcasted_iota(jnp.int32, logits.shape, 1)
    mg, gi = _first_argmax(logits, lane < N_GROUPS, lane)
    onehot_g = jnp.where(lane == gi, 1.0, 0.0)
    row = lax.broadcasted_iota(jnp.int32, (TOK_TILE, TOK_TILE), 0)
    col = lax.broadcasted_iota(jnp.int32, (TOK_TILE, TOK_TILE), 1)
    earlier = jnp.where(col < row, 1.0, 0.0).astype(BF16)
    rank = jnp.dot(earlier, onehot_g.astype(BF16), preferred_element_type=F32)
    return mg, gi, onehot_g, rank


def _route_weights(logits, group, comb_ref, cnt_ref):
    mg, gi, onehot_g, rank = group
    lane = lax.broadcasted_iota(jnp.int32, logits.shape, 1)
    is_g = lane < N_GROUPS
    pg = 1.0 / jnp.sum(jnp.where(is_g, jnp.exp(logits - mg), 0.0), axis=-1, keepdims=True)
    e_lane0 = EXPERT_LANE0 + gi * EXP_PER_GROUP
    is_e = jnp.logical_and(lane >= e_lane0, lane < e_lane0 + EXP_PER_GROUP)
    m1, i1 = _first_argmax(logits, is_e, lane)
    m2, i2 = _first_argmax(logits, jnp.logical_and(is_e, lane != i1), lane)
    e2 = jnp.exp(m2 - m1)
    den = 1.0 + e2
    w1 = (1.0 / den) * pg
    w2 = (e2 / den) * pg

    cnt = jnp.sum(onehot_g, axis=0, keepdims=True)
    padded = jnp.floor((cnt + (ROW_ALIGN - 1)) * (1.0 / ROW_ALIGN)) * ROW_ALIGN
    padded8 = jnp.broadcast_to(padded, (SUBLANES, ROUTER_LANES))
    run0 = (pltpu.roll(padded8, 1, 1) + pltpu.roll(padded8, 2, 1) + pltpu.roll(padded8, 3, 1))[0:1, :]
    pos = jnp.sum(onehot_g * (rank + run0), axis=-1, keepdims=True)

    comb = jnp.where(lane == i1, w1, jnp.where(lane == i2, w2, jnp.where(lane == POS_LANE, pos, onehot_g)))
    comb_ref[...] = comb
    cnt_ref[...] = cnt
    run_rows = [jnp.sum(jnp.where(lane[0:1, :] == g, padded, 0.0)).astype(jnp.int32) for g in range(N_GROUPS)]
    return comb, run_rows


def _sort_tile(h_hi, comb, hc_ref, cc_ref):
    pos_t = comb.T[POS_LANE:POS_LANE + 1, :]
    dest = lax.broadcasted_iota(jnp.int32, (TILE_ROWS, TOK_TILE), 0).astype(F32)
    perm = jnp.where(pos_t == dest, 1.0, 0.0).astype(BF16)
    hc_ref[...] = jnp.dot(perm, h_hi, preferred_element_type=F32).astype(BF16)
    c_parts = _split_bf16(comb, 3)
    cc_ref[...] = (jnp.dot(perm, c_parts[0], preferred_element_type=F32)
                   + jnp.dot(perm, c_parts[1], preferred_element_type=F32)
                   + jnp.dot(perm, c_parts[2], preferred_element_type=F32))


W_STAGE_COLS = 512


def _out_route_kernel(a_ref, w_hbm, gate_ref, g_ref, sc_ref, sh_ref, rw_ref, rb_ref, *refs, n_x, w_idx):
    x_refs = refs[:n_x]
    (xo_ref, hs_ref, cs_ref, comb_ref, cnt_ref, wb_ref, stage_ref, prev_ref, hcs_ref, ccs_ref,
     zh_ref, zc_ref, fill_ref, sent_ref, sem, sem_run, sem_zero) = refs[n_x:]
    i = pl.program_id(0)
    slot = i % 2

    @pl.when(i == 0)
    def _():
        prev_ref[...] = jnp.zeros_like(prev_ref)
        zh_ref[...] = jnp.zeros_like(zh_ref)
        zc_ref[...] = jnp.zeros_like(zc_ref)
        for g in range(N_GROUPS):
            fill_ref[g] = 0
        n_chunks = D // W_STAGE_COLS

        def chunk_copy(c):
            cols = pl.ds(c * W_STAGE_COLS, W_STAGE_COLS)
            return pltpu.make_async_copy(w_hbm.at[w_idx, :, cols], stage_ref.at[c % 2], sem.at[c % 2])

        chunk_copy(0).start()
        chunk_copy(1).start()
        for c in range(n_chunks):
            chunk_copy(c).wait()
            wb_ref[:, c * W_STAGE_COLS:(c + 1) * W_STAGE_COLS] = stage_ref[c % 2].astype(BF16)
            if c + 2 < n_chunks:
                chunk_copy(c + 2).start()

    h_prev = prev_ref[...]
    logits = _router_logits(h_prev, rw_ref, rb_ref)
    group = _route_group(logits)
    y = jnp.dot(a_ref[...], wb_ref[...], preferred_element_type=F32)
    x_mid = _token_rows(x_refs, jnp.minimum(i, N_TOK_TILES - 1), TOK_TILE) + gate_ref[...] * y
    xo_ref[...] = x_mid
    comb, run_rows = _route_weights(logits, group, comb_ref, cnt_ref)
    _sort_tile(h_prev[:, :D], comb, hcs_ref.at[slot], ccs_ref.at[slot])
    prev_ref[...] = _moe_prenorm(x_mid, g_ref, sc_ref, sh_ref)

    def for_each_copy(sl, rows, filled, fn):
        src0 = 0
        for g in range(N_GROUPS):
            for b, bit in enumerate(RUN_BITS):
                done = rows[g] & (-2 * bit)

                @pl.when((rows[g] & bit) != 0)
                def _():
                    src = pl.ds(pl.multiple_of(src0 + done, ROW_ALIGN), bit)
                    dst = pl.ds(pl.multiple_of(g * GROUP_CAP + filled[g] + done, ROW_ALIGN), bit)
                    fn(pltpu.make_async_copy(hcs_ref.at[sl, src, :], hs_ref.at[dst, :], sem_run.at[sl, 0, g, b]))
                    fn(pltpu.make_async_copy(ccs_ref.at[sl, src, :], cs_ref.at[dst, :], sem_run.at[sl, 1, g, b]))

            src0 = src0 + rows[g]

    @pl.when(i >= 1)
    def _():
        filled = [fill_ref[g] for g in range(N_GROUPS)]
        for_each_copy(slot, run_rows, filled, lambda cp: cp.start())
        for g in range(N_GROUPS):
            fill_ref[g] = filled[g] + run_rows[g]

        @pl.when(i >= 2)
        def _():
            sent = [sent_ref[g] for g in range(N_GROUPS)]
            for_each_copy(1 - slot, sent, [0] * N_GROUPS, lambda cp: cp.wait())

        for g in range(N_GROUPS):
            sent_ref[g] = run_rows[g]

    @pl.when(i == pl.num_programs(0) - 1)
    def _():
        for_each_copy(slot, [sent_ref[g] for g in range(N_GROUPS)], [0] * N_GROUPS, lambda cp: cp.wait())
        zero_copies = []
        for g in range(N_GROUPS):
            dst = pl.ds(pl.multiple_of(g * GROUP_CAP + fill_ref[g], ROW_ALIGN), ZERO_ROWS)
            zero_copies.append(pltpu.make_async_copy(zh_ref, hs_ref.at[dst, :], sem_zero.at[0, g]))
            zero_copies.append(pltpu.make_async_copy(zc_ref, cs_ref.at[dst, :], sem_zero.at[1, g]))
        for cp in zero_copies:
            cp.start()
        for cp in zero_copies:
            cp.wait()


def _out_proj_and_route(a_bf16, w, w_idx, x, mod, layer, norm_g, rw, rb):
    tm = TOK_TILE
    k = a_bf16.shape[1]
    last = N_TOK_TILES - 1
    proj = lambda i: jnp.minimum(i, last)
    routed = lambda i: jnp.maximum(i - 1, 0)
    if isinstance(x, tuple):
        n_ctx_tiles = N_CTX // tm
        x_specs = [pl.BlockSpec((tm, D), lambda i: (jnp.minimum(proj(i), n_ctx_tiles - 1), 0)),
                   pl.BlockSpec((tm, D), lambda i: (jnp.maximum(proj(i) - n_ctx_tiles, 0), 0))]
        x_args = list(x)
    else:
        x_specs = [pl.BlockSpec((tm, D), lambda i: (proj(i), 0))]
        x_args = [x]

    def mod_of_proj(which):
        return pl.BlockSpec((None, None, None, 1, D),
                            lambda i: (layer, _cond_of_row(proj(i) * tm), which, 0, 0))

    return pl.pallas_call(
        functools.partial(_out_route_kernel, n_x=len(x_args), w_idx=w_idx),
        grid=(N_TOK_TILES + 1,),
        in_specs=[
            pl.BlockSpec((tm, k), lambda i: (proj(i), 0)),
            pl.BlockSpec(memory_space=pl.ANY),
            mod_of_proj(2),
            pl.BlockSpec((1, D), lambda i: (0, 0)),
            mod_of_proj(4),
            mod_of_proj(3),
            pl.BlockSpec((2 * D, ROUTER_LANES), lambda i: (0, 0)),
            pl.BlockSpec((1, ROUTER_LANES), lambda i: (0, 0)),
        ] + x_specs,
        out_specs=[
            pl.BlockSpec((tm, D), lambda i: (proj(i), 0)),
            pl.BlockSpec(memory_space=pl.ANY),
            pl.BlockSpec(memory_space=pl.ANY),
            pl.BlockSpec((tm, ROUTER_LANES), lambda i: (routed(i), 0)),
            pl.BlockSpec((None, 1, ROUTER_LANES), lambda i: (routed(i), 0, 0)),
        ],
        out_shape=[
            jax.ShapeDtypeStruct((N_TOK, D), F32),
            jax.ShapeDtypeStruct((ROW_CAP, D), BF16),
            jax.ShapeDtypeStruct((ROW_CAP, ROUTER_LANES), F32),
            jax.ShapeDtypeStruct((N_TOK, ROUTER_LANES), F32),
            jax.ShapeDtypeStruct((N_TOK_TILES, 1, ROUTER_LANES), F32),
        ],
        scratch_shapes=[
            pltpu.VMEM((k, D), BF16),
            pltpu.VMEM((2, k, W_STAGE_COLS), F32),
            pltpu.VMEM((tm, 2 * D), BF16),
            pltpu.VMEM((2, TILE_ROWS, D), BF16),
            pltpu.VMEM((2, TILE_ROWS, ROUTER_LANES), F32),
            pltpu.VMEM((ZERO_ROWS, D), BF16),
            pltpu.VMEM((ZERO_ROWS, ROUTER_LANES), F32),
            pltpu.SMEM((N_GROUPS,), jnp.int32),
            pltpu.SMEM((N_GROUPS,), jnp.int32),
            pltpu.SemaphoreType.DMA((2,)),
            pltpu.SemaphoreType.DMA((2, 2, N_GROUPS, len(RUN_BITS))),
            pltpu.SemaphoreType.DMA((2, N_GROUPS)),
        ],
        compiler_params=_params("arbitrary"),
        name="out_proj_route",
    )(a_bf16, w, mod, norm_g.reshape(1, D), mod, mod, rw, rb, *x_args)


def _router_weights(rgw, rgb, rew, reb):
    n_out = N_GROUPS + N_EXPERTS
    rw = jnp.concatenate([rgw, rew.reshape(D, N_EXPERTS)], axis=1)
    hi = rw.astype(BF16)
    lo = (rw - hi.astype(F32)).astype(BF16)
    zeros = lambda n: jnp.zeros((D, n), BF16)
    top = jnp.concatenate([hi, zeros(LOW_LANE0 - n_out), lo, zeros(ROUTER_LANES - LOW_LANE0 - n_out)], axis=1)
    bottom = jnp.concatenate([hi, zeros(ROUTER_LANES - n_out)], axis=1)
    rb = jnp.concatenate([rgb, reb.reshape(N_EXPERTS)])
    rb = jnp.pad(rb, (0, ROUTER_LANES - n_out)).reshape(1, ROUTER_LANES)
    return jnp.concatenate([top, bottom], axis=0), rb


def _moe_plan(cnt):
    i32 = jnp.int32
    padded = (cnt + (ROW_ALIGN - 1)) // ROW_ALIGN * ROW_ALIGN
    run0 = jnp.cumsum(padded, axis=1) - padded
    length = jnp.sum(padded, axis=0)
    start = jnp.arange(N_GROUPS, dtype=i32) * GROUP_CAP
    off = start[None, :] + jnp.cumsum(padded, axis=0) - padded
    need = length
    ntile = (need + MOE_TILE - 1) // MOE_TILE
    cum = jnp.cumsum(ntile)
    total = cum[-1]
    k = jnp.arange(MOE_TILES_MAX, dtype=i32)
    kk = jnp.minimum(k, jnp.maximum(total - 1, 0))
    grp = jnp.minimum(jnp.sum((kk[:, None] >= cum[None, :]).astype(i32), axis=1), N_GROUPS - 1)
    j = kk - (cum - ntile)[grp]
    blk = start[grp] // MOE_TILE + j
    nsub = jnp.clip((need[grp] - j * MOE_TILE + MOE_SUB - 1) // MOE_SUB, 0, MOE_TILE // MOE_SUB)
    nsub = jnp.where(k < total, nsub, 0)
    runs = tuple(a.reshape(-1).astype(i32) for a in (off, run0, padded))
    return runs, (blk.astype(i32), grp.astype(i32), nsub.astype(i32))


def _run_pieces(run_refs, tile, g):
    off_ref, run0_ref, padded_ref = run_refs
    idx = tile * N_GROUPS + g
    n = padded_ref[idx]
    src0 = run0_ref[idx]
    dst0 = off_ref[idx]
    pieces = []
    for b, bit in enumerate(RUN_BITS):
        done = n & (-2 * bit)
        pieces.append((b, (n & bit) != 0, pl.multiple_of(src0 + done, ROW_ALIGN),
                       pl.multiple_of(dst0 + done, ROW_ALIGN), bit))
    return pieces


def _moe_kernel(blk_ref, grp_ref, nsub_ref, h_ref, c_ref, w1_ref, w3_ref, w2_ref, o_ref, hid_ref):
    k = pl.program_id(0)
    c = pl.program_id(1)
    ns = nsub_ref[k]
    n_sub_max = MOE_TILE // MOE_SUB

    def for_row_count(fn):
        for m in range(1, n_sub_max + 1):
            @pl.when(ns == m)
            def _():
                fn(m * MOE_SUB)

    @pl.when(jnp.logical_and(ns > 0, c < N_FF_CHUNKS))
    def _():
        w1 = w1_ref[...].astype(BF16)
        w3 = w3_ref[...].astype(BF16)
        e_lane = EXPERT_LANE0 + grp_ref[k] * EXP_PER_GROUP + c * FF_CHUNK // D_EXPERT

        def up(rows):
            h = h_ref[0:rows, :]
            a = jnp.dot(h, w1, preferred_element_type=F32)
            b = jnp.dot(h, w3, preferred_element_type=F32)
            comb = c_ref[0:rows, :]
            lane = lax.broadcasted_iota(jnp.int32, comb.shape, 1)
            cw = jnp.sum(jnp.where(lane == e_lane, comb, 0.0), axis=-1, keepdims=True)
            hid_ref[c, 0:rows, :] = (_silu(a) * b * cw).astype(BF16)

        for_row_count(up)

    @pl.when(jnp.logical_and(ns > 0, c >= N_FF_CHUNKS))
    def _():
        w2 = w2_ref[...].reshape(EXP_PER_GROUP * D_EXPERT, OUT_CHUNK).astype(BF16)

        def down(rows):
            hid = jnp.concatenate([hid_ref[j, 0:rows, :] for j in range(N_FF_CHUNKS)], axis=1)
            o_ref[0:rows, :] = jnp.dot(hid, w2, preferred_element_type=F32).astype(BF16)
            if rows < MOE_TILE:
                o_ref[rows:, :] = jnp.zeros((MOE_TILE - rows, OUT_CHUNK), BF16)

        for_row_count(down)


def _moe_experts(hs, cs, w1, w3, w2, layer, plan):
    per_e = D_EXPERT // FF_CHUNK
    n_steps = N_FF_CHUNKS + N_OUT_CHUNKS

    def step_of(k, c, nsub_ref):
        return jnp.where(nsub_ref[k] > 0, c, n_steps - 1)

    def w13_map(k, c, blk_ref, grp_ref, nsub_ref):
        cc = jnp.minimum(step_of(k, c, nsub_ref), N_FF_CHUNKS - 1)
        return (layer, grp_ref[k] * EXP_PER_GROUP + cc // per_e, 0, cc % per_e)

    def out_chunk(k, c, nsub_ref):
        return jnp.maximum(step_of(k, c, nsub_ref) - N_FF_CHUNKS, 0)

    w2_map = lambda k, c, blk_ref, grp_ref, nsub_ref: (layer, grp_ref[k], 0, out_chunk(k, c, nsub_ref))
    row_map = lambda k, c, blk_ref, grp_ref, nsub_ref: (blk_ref[k], 0)
    out_map = lambda k, c, blk_ref, grp_ref, nsub_ref: (blk_ref[k], out_chunk(k, c, nsub_ref))
    grid_spec = pltpu.PrefetchScalarGridSpec(
        num_scalar_prefetch=3,
        grid=(MOE_TILES_MAX, n_steps),
        in_specs=[
            pl.BlockSpec((MOE_TILE, D), row_map),
            pl.BlockSpec((MOE_TILE, ROUTER_LANES), row_map),
            pl.BlockSpec((None, None, D, FF_CHUNK), w13_map),
            pl.BlockSpec((None, None, D, FF_CHUNK), w13_map),
            pl.BlockSpec((None, EXP_PER_GROUP, D_EXPERT, OUT_CHUNK), w2_map),
        ],
        out_specs=pl.BlockSpec((MOE_TILE, OUT_CHUNK), out_map),
        scratch_shapes=[pltpu.VMEM((N_FF_CHUNKS, MOE_TILE, FF_CHUNK), BF16)],
    )
    return pl.pallas_call(
        _moe_kernel,
        grid_spec=grid_spec,
        out_shape=jax.ShapeDtypeStruct((ROW_CAP, D), BF16),
        compiler_params=_params("arbitrary", "arbitrary"),
        name="moe_experts",
    )(*plan, hs, cs, w1, w3, w2)


def _combine_kernel(off_ref, run0_ref, padded_ref, x_ref, gate_ref, comb_ref, ys_ref, *refs,
                    split_out, emit_next):
    if emit_next:
        (ng_ref, nsc_ref, nsh_ref), refs = refs[:3], refs[3:]
    out_refs, (yw_ref, sem) = refs[:-2], refs[-2:]
    run_refs = (off_ref, run0_ref, padded_ref)
    i = pl.program_id(0)
    n_steps = pl.num_programs(0)
    slot = i % 2

    def for_each_copy(tile, sl, fn):
        for g in range(N_GROUPS):
            for b, pred, src, dst, rows in _run_pieces(run_refs, tile, g):
                @pl.when(pred)
                def _():
                    fn(pltpu.make_async_copy(ys_ref.at[pl.ds(dst, rows), :],
                                             yw_ref.at[sl, pl.ds(src, rows), :], sem.at[sl, g, b]))

    def fetch(tile, sl):
        yw_ref[sl, TOK_TILE:, :] = jnp.zeros((TILE_ROWS - TOK_TILE, D), BF16)
        for_each_copy(tile, sl, lambda cp: cp.start())

    @pl.when(i == 0)
    def _():
        fetch(i, slot)

    @pl.when(i + 1 < n_steps)
    def _():
        fetch(i + 1, 1 - slot)

    for_each_copy(i, slot, lambda cp: cp.wait())

    pos = comb_ref[:, POS_LANE:POS_LANE + 1].astype(jnp.int32)
    wcol = lax.broadcasted_iota(jnp.int32, (TOK_TILE, TILE_ROWS), 1)
    perm = jnp.where(wcol == pos, 1.0, 0.0).astype(BF16)
    y = jnp.dot(perm, yw_ref[slot], preferred_element_type=F32)
    out = x_ref[...] + gate_ref[...] * y
    if emit_next:
        out_refs[1][...] = _modnorm(out, ng_ref[...], nsc_ref[...], nsh_ref[...]).astype(BF16)
    if not split_out:
        out_refs[0][...] = out
    else:
        @pl.when(i < N_CTX // TOK_TILE)
        def _():
            out_refs[0][...] = out

        @pl.when(i >= N_CTX // TOK_TILE)
        def _():
            out_refs[1][...] = out


def _combine(x, mod, layer, comb, ys, runs, next_norm_g):
    n_ctx_tiles = N_CTX // TOK_TILE
    split_out = next_norm_g is None
    tile_spec = pl.BlockSpec((TOK_TILE, D), lambda i, *_: (i, 0))
    in_specs = [
        tile_spec,
        _mod_spec(layer, 5, TOK_TILE, row_axis=0),
        pl.BlockSpec((TOK_TILE, ROUTER_LANES), lambda i, *_: (i, 0)),
        pl.BlockSpec(memory_space=pl.ANY),
    ]
    args = [x, mod, comb, ys]
    if split_out:
        out_specs = [pl.BlockSpec((TOK_TILE, D), lambda i, *_: (jnp.minimum(i, n_ctx_tiles - 1), 0)),
                     pl.BlockSpec((TOK_TILE, D), lambda i, *_: (jnp.maximum(i - n_ctx_tiles, 0), 0))]
        out_shape = [jax.ShapeDtypeStruct((N_CTX, D), F32), jax.ShapeDtypeStruct((N_LAT, D), F32)]
    else:
        in_specs += [pl.BlockSpec((1, D), lambda i, *_: (0, 0)),
                     _mod_spec(layer + 1, 1, TOK_TILE, row_axis=0),
                     _mod_spec(layer + 1, 0, TOK_TILE, row_axis=0)]
        args += [next_norm_g.reshape(1, D), mod, mod]
        out_specs = [tile_spec, tile_spec]
        out_shape = [jax.ShapeDtypeStruct((N_TOK, D), F32), jax.ShapeDtypeStruct((N_TOK, D), BF16)]
    grid_spec = pltpu.PrefetchScalarGridSpec(
        num_scalar_prefetch=3,
        grid=(N_TOK_TILES,),
        in_specs=in_specs,
        out_specs=out_specs,
        scratch_shapes=[
            pltpu.VMEM((2, TILE_ROWS, D), BF16),
            pltpu.SemaphoreType.DMA((2, N_GROUPS, len(RUN_BITS))),
        ],
    )
    return pl.pallas_call(
        functools.partial(_combine_kernel, split_out=split_out, emit_next=not split_out),
        grid_spec=grid_spec,
        out_shape=out_shape,
        compiler_params=_params("arbitrary"),
        name="moe_combine",
    )(*runs, *args)


def _moe_experts_and_combine(x, routed, mod, layer, w1, w3, w2, next_norm_g):
    hs, cs, comb, cnt = routed
    runs, plan = _moe_plan(cnt[:, 0, :N_GROUPS].astype(jnp.int32))
    ys = _moe_experts(hs, cs, w1, w3, w2, layer, plan)
    return _combine(x, mod, layer, comb, ys, runs, next_norm_g)


def kernel(x_prompt, x_sample, cache_k, cache_v, c, c_ctx, ada_w, ada_b, norm_mix_g, norm_ffn_g,
           mix_in_w, pool_w, pool_scale, conv_w, conv_b, conv_ln_g, conv_ln_b, mix_out_w,
           qkv_w, q_norm_g, k_norm_g, rpb, attn_out_w, router_g_w, router_g_b, router_e_w,
           router_e_b, exp_w1, exp_w3, exp_w2):
    x = (x_prompt.reshape(N_CTX, D), x_sample.reshape(N_LAT, D))
    cvec8 = jnp.concatenate([c_ctx[None, :], c, jnp.zeros((8 - 1 - N_LAT_SEQ, D), F32)], axis=0)
    mod = _mod_table(_adaln_first_layer(cvec8, ada_w, ada_b))
    n_attn = DEPTH // 2
    cache_k2 = cache_k.reshape(N_LAT_SEQ, n_attn, CTX_SEQ, D)
    cache_v2 = cache_v.reshape(N_LAT_SEQ, n_attn, CTX_SEQ, D)

    caches = None
    h = None
    for l in range(DEPTH):
        e = l // 2
        if l % 2 == 0:
            u = _mm_prologue(x, mod, l, norm_mix_g[l], mix_in_w, e, h=h)
            mixed, *later_mod = _seq_mixer(u, pool_w[e].astype(BF16), pool_scale[e], conv_w[e], conv_b[e],
                                           conv_ln_g[e], conv_ln_b[e],
                                           ada=(cvec8, ada_w, ada_b) if l == 0 else None)
            if later_mod:
                mod = jnp.concatenate([mod, _mod_table(later_mod[0])], axis=0)
            out_w = mix_out_w
        else:
            q, k, v, new_k, new_v = _qkv_proj(h, e, qkv_w, q_norm_g[e], k_norm_g[e], caches)
            caches = (new_k, new_v)
            mixed = _ctx_attention(q, k, v)
            mixed = _na_attention(q, k, v, cache_k2[:, e], cache_v2[:, e], rpb[e], mixed)
            out_w = attn_out_w
        rw, rb = _router_weights(router_g_w[l], router_g_b[l], router_e_w[l], router_e_b[l])
        x, *routed = _out_proj_and_route(mixed, out_w, e, x, mod, l, norm_ffn_g[l], rw, rb)
        if l < DEPTH - 1:
            x, h = _moe_experts_and_combine(x, routed, mod, l, exp_w1, exp_w3, exp_w2,
                                            next_norm_g=norm_mix_g[l + 1])
        else:
            y_ctx, y_lat = _moe_experts_and_combine(x, routed, mod, l, exp_w1, exp_w3, exp_w2,
                                                    next_norm_g=None)

    y_prompt = y_ctx.reshape(N_CTX_SEQ, CTX_SEQ, D)
    y_sample = y_lat.reshape(N_LAT_SEQ, LAT_SEQ, D)
    cache_dims = (N_CTX_SEQ, n_attn, CTX_SEQ, N_HEADS, HEAD_DIM)
    return (y_prompt, y_sample, caches[0].reshape(cache_dims), caches[1].reshape(cache_dims))
```
